```python
import math
import jax, jax.numpy as jnp
from jax import lax
import numpy as np

D_MODEL = 1024
BATCH = 8
SEQ = 4096
DEPTH = 4
DEC_BATCH = 4
DEC_SEQ = 8192
PAST_LEN = 128

HEAD_DIM = 64
N_HEADS_A = D_MODEL // 128
N_KV_A = N_HEADS_A // 4
GQA = N_HEADS_A // N_KV_A
ATTN_W = N_HEADS_A * HEAD_DIM
WINDOW = 128
BLOCK = 128
N_BUCKETS = 32
MAX_DISTANCE = 128
SGU_W = D_MODEL // 2
N_GROUPS_B = 8
CH_B = SGU_W // N_GROUPS_B
CHUNK = 128
AB_IN = ATTN_W + 2 * N_KV_A * HEAD_DIM + 2 * SGU_W
AB_OUT = ATTN_W + SGU_W
D_HY = D_MODEL
FILTER_EMB = 33
FILTER_BANDS = (FILTER_EMB - 1) // 2
FILTER_WIDTH = 64
DECAY_TARGET = 1e-2
FAST_DECAY_PCT = 0.3
SLOW_DECAY_PCT = 1.5
MIN_DECAY = math.log(DECAY_TARGET) / SLOW_DECAY_PCT
MAX_DECAY = math.log(DECAY_TARGET) / FAST_DECAY_PCT
FFN_HIDDEN = 2816
N_EVEN = (DEPTH + 1) // 2
N_ODD = DEPTH // 2
EPS = 1e-6
NEG = -1e30

kernel_name = 'hybrid_bidir_swa_sgu_hyena_encoder'


def rmsnorm(x, g):
    xf = x.astype(jnp.float32)
    y = xf * lax.rsqrt(jnp.mean(xf * xf, axis=-1, keepdims=True) + EPS)
    return (y * g.astype(jnp.float32)).astype(x.dtype)


def dwconv3(x, w, b):
    xp = jnp.pad(x, ((0, 0), (1, 1), (0, 0)))
    return xp[:, :-2] * w[0] + xp[:, 1:-1] * w[1] + xp[:, 2:] * w[2] + b


def t5_bucket(rel):
    half = N_BUCKETS // 2
    max_exact = half // 2
    ret = jnp.where(rel > 0, half, 0)
    n = jnp.abs(rel)
    nf = jnp.maximum(n, 1).astype(jnp.float32)
    large = max_exact + (jnp.log(nf / max_exact) / math.log(MAX_DISTANCE / max_exact)
                         * (half - max_exact)).astype(jnp.int32)
    large = jnp.minimum(large, half - 1)
    return ret + jnp.where(n < max_exact, n, large)


def windowed_gqa(q, k, v, q_gain, k_gain, sink, rel_bias):
    B, L = q.shape[0], q.shape[1]
    nb = L // BLOCK
    q = rmsnorm(q.reshape(B, L, N_HEADS_A, HEAD_DIM), q_gain) * (HEAD_DIM ** -0.5)
    k = rmsnorm(k.reshape(B, L, N_KV_A, HEAD_DIM), k_gain)
    v = v.reshape(B, L, N_KV_A, HEAD_DIM)
    qb = q.reshape(B, nb, BLOCK, N_KV_A, GQA, HEAD_DIM)

    def band(t):
        tp = jnp.pad(t, ((0, 0), (BLOCK, BLOCK), (0, 0), (0, 0))).reshape(B, nb + 2, BLOCK, N_KV_A, HEAD_DIM)
        return jnp.concatenate([tp[:, :-2], tp[:, 1:-1], tp[:, 2:]], axis=2)

    kb, vb = band(k), band(v)
    vp = jnp.pad(jnp.ones((L,), bool), (BLOCK, BLOCK)).reshape(nb + 2, BLOCK)
    valid = jnp.concatenate([vp[:-2], vp[1:-1], vp[2:]], axis=1)
    rel = (jnp.arange(3 * BLOCK)[None, :] - BLOCK) - jnp.arange(BLOCK)[:, None]
    mask = (jnp.abs(rel) <= WINDOW)[None] & valid[:, None, :]
    bias = rel_bias.astype(jnp.float32)[t5_bucket(rel)]
    bias = bias.transpose(2, 0, 1).reshape(N_KV_A, GQA, BLOCK, 3 * BLOCK)
    s = jnp.einsum('bnqhgd,bnshd->bnhgqs', qb, kb).astype(jnp.float32) + bias
    s = jnp.where(mask[None, :, None, None], s, NEG)
    sink_l = jnp.broadcast_to(sink.astype(jnp.float32).reshape(N_KV_A, GQA)[None, None, :, :, None, None],
                              s.shape[:-1] + (1,))
    p = jax.nn.softmax(jnp.concatenate([s, sink_l], axis=-1), axis=-1)[..., :-1].astype(v.dtype)
    o = jnp.einsum('bnhgqs,bnshd->bnqhgd', p, vb)
    return o.reshape(B, L, ATTN_W)


def spatial_gating(su, sv, v_gain, w_s, b_s):
    B, L = su.shape[0], su.shape[1]
    nc = L // CHUNK
    sv = rmsnorm(sv.reshape(B, nc, CHUNK, N_GROUPS_B, CH_B), v_gain)
    sv = jnp.einsum('gpq,bnqgc->bnpgc', w_s, sv) + b_s.T[None, None, :, :, None]
    return su * sv.reshape(B, L, SGU_W)


def hyena_filter(L, w1, b1, f1, w2, b2, f2, w3, b3, f3, wout):
    f32 = jnp.float32
    t = jnp.linspace(0.0, 1.0, L, dtype=f32)[:, None]
    w = 2.0 * math.pi * jnp.arange(L, dtype=f32)[:, None] / L
    fr = jnp.linspace(1e-4, FILTER_BANDS - 1, FILTER_BANDS, dtype=f32)[None, :]
    feats = jnp.concatenate([t, jnp.cos(fr * w), -jnp.sin(fr * w)], axis=-1)
    h = jnp.sin(f1.astype(f32) * (feats @ w1.astype(f32) + b1.astype(f32)))
    h = jnp.sin(f2.astype(f32) * (h @ w2.astype(f32) + b2.astype(f32)))
    h = jnp.sin(f3.astype(f32) * (h @ w3.astype(f32) + b3.astype(f32)))
    h = h @ wout.astype(f32)
    deltas = jnp.abs(jnp.linspace(MIN_DECAY, MAX_DECAY, D_HY, dtype=f32))
    decay = jnp.exp(-t * deltas[None, :])
    h_fwd = h[:, :D_HY] * decay
    h_bwd = h[:, D_HY:] * decay
    return jnp.concatenate([h_fwd, jnp.zeros((1, D_HY), f32), h_bwd[:0:-1]], axis=0)


def fftconv(z, kfull):
    B, L, C = z.shape
    Z = jnp.fft.rfft(z.astype(jnp.float32), n=2 * L, axis=1)
    K = jnp.fft.rfft(kfull, n=2 * L, axis=0)
    y = jnp.fft.irfft(Z * K[None], n=2 * L, axis=1)[:, :L]
    return y.astype(z.dtype)


def ab_mixer(xn, p, i):
    proj = xn @ p['ab_w_in'][i]
    q, k, v, su, sv = jnp.split(proj, [ATTN_W, ATTN_W + N_KV_A * HEAD_DIM,
                                        ATTN_W + 2 * N_KV_A * HEAD_DIM,
                                        ATTN_W + 2 * N_KV_A * HEAD_DIM + SGU_W], axis=-1)
    attn = windowed_gqa(q, k, v, p['q_norm'][i], p['k_norm'][i], p['attn_sink'][i], p['rel_bias'])
    sgu = spatial_gating(jax.nn.gelu(su), jax.nn.gelu(sv), p['sgu_v_norm'][i], p['sgu_w'][i], p['sgu_b'][i])
    return jnp.concatenate([attn, sgu], axis=-1) @ p['ab_w_out'][i]


def hyena_mixer(xn, p, i):
    L = xn.shape[1]
    u = dwconv3(xn @ p['hy_w_in'][i], p['hy_conv_w'][i], p['hy_conv_b'][i])
    x0, x1, v = jnp.split(u, 3, axis=-1)
    kfull = hyena_filter(L, p['hy_f_w1'][i], p['hy_f_b1'][i], p['hy_f_freq1'][i],
                         p['hy_f_w2'][i], p['hy_f_b2'][i], p['hy_f_freq2'][i],
                         p['hy_f_w3'][i], p['hy_f_b3'][i], p['hy_f_freq3'][i], p['hy_f_wout'][i])
    z = x1 * v
    z = fftconv(z, kfull) + z * p['hy_d'][i]
    return (x0 * z) @ p['hy_w_out'][i]


def conv_ffn(xn, p, l):
    h = dwconv3(xn @ p['ffn_w_up'][l], p['ffn_conv_w'][l], p['ffn_conv_b'][l])
    g, u = jnp.split(h, 2, axis=-1)
    return (jax.nn.gelu(g) * u) @ p['ffn_w_down'][l]


def trunk(x, p):
    for l in range(DEPTH):
        xn = rmsnorm(x, p['mix_norm'][l])
        if l % 2 == 0:
            x = x + ab_mixer(xn, p, l // 2)
        else:
            x = x + hyena_mixer(xn, p, l // 2)
        x = x + conv_ffn(rmsnorm(x, p['ffn_norm'][l]), p, l)
    return x


def setup_inputs(seed: int = 0) -> dict:
    key = jax.random.key(seed)
    ks = iter(jax.random.split(key, 64))
    f32 = jnp.float32

    def nrm(shape, scale):
        return jax.random.normal(next(ks), shape, f32) * scale

    def gain(shape):
        return 1.0 + nrm(shape, 0.02)

    FILTER_OUT_SCALE = 0.05
    return {
        'x_prompt': nrm((BATCH, SEQ, D_MODEL), 1.0),
        'x_sample': nrm((DEC_BATCH, DEC_SEQ, D_MODEL), 1.0),
        'rel_bias': nrm((N_BUCKETS, N_HEADS_A), 0.1),
        'mix_norm': gain((DEPTH, D_MODEL)),
        'ffn_norm': gain((DEPTH, D_MODEL)),
        'ab_w_in': nrm((N_EVEN, D_MODEL, AB_IN), D_MODEL ** -0.5),
        'q_norm': gain((N_EVEN, HEAD_DIM)),
        'k_norm': gain((N_EVEN, HEAD_DIM)),
        'attn_sink': nrm((N_EVEN, N_HEADS_A), 0.5),
        'sgu_v_norm': gain((N_EVEN, N_GROUPS_B, CH_B)),
        'sgu_w': nrm((N_EVEN, N_GROUPS_B, CHUNK, CHUNK), CHUNK ** -0.5),
        'sgu_b': 1.0 + nrm((N_EVEN, N_GROUPS_B, CHUNK), 0.1),
        'ab_w_out': nrm((N_EVEN, AB_OUT, D_MODEL), AB_OUT ** -0.5),
        'hy_w_in': nrm((N_ODD, D_MODEL, 3 * D_HY), D_MODEL ** -0.5),
        'hy_conv_w': nrm((N_ODD, 3, 3 * D_HY), 3 ** -0.5),
        'hy_conv_b': nrm((N_ODD, 3 * D_HY), 0.02),
        'hy_f_w1': nrm((N_ODD, FILTER_EMB, FILTER_WIDTH), FILTER_EMB ** -0.5),
        'hy_f_b1': nrm((N_ODD, FILTER_WIDTH), 0.1),
        'hy_f_freq1': gain((N_ODD, FILTER_WIDTH)),
        'hy_f_w2': nrm((N_ODD, FILTER_WIDTH, FILTER_WIDTH), FILTER_WIDTH ** -0.5),
        'hy_f_b2': nrm((N_ODD, FILTER_WIDTH), 0.1),
        'hy_f_freq2': gain((N_ODD, FILTER_WIDTH)),
        'hy_f_w3': nrm((N_ODD, FILTER_WIDTH, FILTER_WIDTH), FILTER_WIDTH ** -0.5),
        'hy_f_b3': nrm((N_ODD, FILTER_WIDTH), 0.1),
        'hy_f_freq3': gain((N_ODD, FILTER_WIDTH)),
        'hy_f_wout': nrm((N_ODD, FILTER_WIDTH, 2 * D_HY), FILTER_WIDTH ** -0.5 * FILTER_OUT_SCALE),
        'hy_d': nrm((N_ODD, D_HY), 0.1),
        'hy_w_out': nrm((N_ODD, D_HY, D_MODEL), D_HY ** -0.5),
        'ffn_w_up': nrm((DEPTH, D_MODEL, 2 * FFN_HIDDEN), D_MODEL ** -0.5),
        'ffn_conv_w': nrm((DEPTH, 3, 2 * FFN_HIDDEN), 3 ** -0.5),
        'ffn_conv_b': nrm((DEPTH, 2 * FFN_HIDDEN), 0.02),
        'ffn_w_down': nrm((DEPTH, FFN_HIDDEN, D_MODEL), FFN_HIDDEN ** -0.5),
    }


def reference(x_prompt, x_sample, rel_bias, mix_norm, ffn_norm, ab_w_in, q_norm, k_norm, attn_sink,
              sgu_v_norm, sgu_w, sgu_b, ab_w_out, hy_w_in, hy_conv_w, hy_conv_b,
              hy_f_w1, hy_f_b1, hy_f_freq1, hy_f_w2, hy_f_b2, hy_f_freq2, hy_f_w3, hy_f_b3, hy_f_freq3,
              hy_f_wout, hy_d, hy_w_out, ffn_w_up, ffn_conv_w, ffn_conv_b, ffn_w_down):
    p = dict(rel_bias=rel_bias, mix_norm=mix_norm, ffn_norm=ffn_norm, ab_w_in=ab_w_in,
             q_norm=q_norm, k_norm=k_norm, attn_sink=attn_sink, sgu_v_norm=sgu_v_norm,
             sgu_w=sgu_w, sgu_b=sgu_b, ab_w_out=ab_w_out, hy_w_in=hy_w_in, hy_conv_w=hy_conv_w,
             hy_conv_b=hy_conv_b, hy_f_w1=hy_f_w1, hy_f_b1=hy_f_b1, hy_f_freq1=hy_f_freq1,
             hy_f_w2=hy_f_w2, hy_f_b2=hy_f_b2, hy_f_freq2=hy_f_freq2, hy_f_w3=hy_f_w3,
             hy_f_b3=hy_f_b3, hy_f_freq3=hy_f_freq3, hy_f_wout=hy_f_wout, hy_d=hy_d,
             hy_w_out=hy_w_out, ffn_w_up=ffn_w_up, ffn_conv_w=ffn_conv_w, ffn_conv_b=ffn_conv_b,
             ffn_w_down=ffn_w_down)
    y_prompt = trunk(x_prompt, p)
    y_sample = trunk(x_sample, p)
    return (y_prompt, y_sample)
```

```python
import functools
import math

import numpy as np
import jax
import jax.numpy as jnp
from jax import lax
from jax.experimental import pallas as pl
from jax.experimental.pallas import tpu as pltpu

F32 = jnp.float32
BF16 = jnp.bfloat16

D_MODEL = 1024
HEAD_DIM = 64
N_HEADS_A = 8
N_KV_A = 2
GQA = 4
ATTN_W = 512
WINDOW = 128
BLOCK = 128
N_BUCKETS = 32
MAX_DISTANCE = 128
SGU_W = 512
N_GROUPS_B = 8
CH_B = 64
AB_IN = 1792
FILTER_EMB = 33
FILTER_BANDS = 16
FILTER_WIDTH = 64
DECAY_TARGET = 1e-2
MIN_DECAY = math.log(DECAY_TARGET) / 1.5
MAX_DECAY = math.log(DECAY_TARGET) / 0.3
FFN_HIDDEN = 2816
EPS = 1e-6
NEG = -1e30

FFN_CHUNK = 256
HY_CHUNK = 256
HALO = 16
FFT_N2 = 128
VMEM_LIMIT = 56 * 1024 * 1024


def _gelu(x):
    return 0.5 * x * (1.0 + jnp.tanh(0.7978845608028654 * (x + 0.044715 * (x * x * x))))


def _rms(x, gain):
    return x * lax.rsqrt(jnp.mean(x * x, axis=-1, keepdims=True) + EPS) * gain


def _const_spec(shape):
    nd = len(shape)
    return pl.BlockSpec(shape, lambda *_: (0,) * nd, pipeline_mode=pl.Buffered(1))


def _params(n_axes):
    return pltpu.CompilerParams(dimension_semantics=("arbitrary",) * n_axes,
                                vmem_limit_bytes=VMEM_LIMIT)


def _conv3_rows(h, cw, cb, rows):
    total = rows + 2 * HALO
    prev = pltpu.roll(h, 1, 0)[HALO:HALO + rows]
    nxt = pltpu.roll(h, total - 1, 0)[HALO:HALO + rows]
    cur = h[HALO:HALO + rows]
    return prev * cw[0:1] + cur * cw[1:2] + nxt * cw[2:3] + cb


def _fill_normed_ext(xe_ref, x, xp, xn, gain, rows, first, last):
    pm = jnp.where(first, 0.0, 1.0)
    nm = jnp.where(last, 0.0, 1.0)
    xe_ref[0:HALO, :] = (_rms(xp, gain) * pm).astype(BF16)
    xe_ref[HALO:HALO + rows, :] = _rms(x, gain).astype(BF16)
    xe_ref[HALO + rows:HALO + rows + HALO, :] = (_rms(xn, gain) * nm).astype(BF16)


def _halo_specs(T, L, D):
    hb = T // HALO
    last = L // HALO - 1
    return [
        pl.BlockSpec((None, T, D), lambda b, i: (b, i, 0)),
        pl.BlockSpec((None, HALO, D), lambda b, i: (b, jnp.maximum(i * hb - 1, 0), 0)),
        pl.BlockSpec((None, HALO, D), lambda b, i: (b, jnp.minimum((i + 1) * hb, last), 0)),
    ]


def _ffn_kernel(x_ref, xp_ref, xn_ref, g_ref, wup_ref, cw_ref, cb_ref, wdn_ref, o_ref,
                xe_ref, a_ref, *, T, nchunk):
    i = pl.program_id(1)
    x = x_ref[...]
    _fill_normed_ext(xe_ref, x, xp_ref[...], xn_ref[...], g_ref[...], T,
                     i == 0, i == pl.num_programs(1) - 1)
    xe = xe_ref[...]
    hc = FFN_CHUNK
    for j in range(nchunk):
        h = jnp.dot(xe, wup_ref[j], preferred_element_type=F32)
        c = _conv3_rows(h, cw_ref[j], cb_ref[j], T)
        a_ref[:, j * hc:(j + 1) * hc] = (_gelu(c[:, :hc]) * c[:, hc:]).astype(BF16)
    o_ref[...] = x + jnp.dot(a_ref[...], wdn_ref[...], preferred_element_type=F32)


def _ffn(x, gain, wup_r, cw_r, cb_r, wdn, T):
    B, L, D = x.shape
    nchunk = wup_r.shape[0]
    kern = functools.partial(_ffn_kernel, T=T, nchunk=nchunk)
    return pl.pallas_call(
        kern,
        grid=(B, L // T),
        in_specs=_halo_specs(T, L, D) + [
            _const_spec((1, D)), _const_spec(wup_r.shape), _const_spec(cw_r.shape),
            _const_spec(cb_r.shape), _const_spec(wdn.shape)],
        out_specs=pl.BlockSpec((None, T, D), lambda b, i: (b, i, 0)),
        out_shape=jax.ShapeDtypeStruct(x.shape, F32),
        scratch_shapes=[pltpu.VMEM((T + 2 * HALO, D), BF16), pltpu.VMEM((T, FFN_HIDDEN), BF16)],
        compiler_params=_params(2),
        name="conv_ffn",
    )(x, x, x, gain, wup_r, cw_r, cb_r, wdn)


def _hy_in_kernel(x_ref, xp_ref, xn_ref, g_ref, win_ref, cw_ref, cb_ref, x0_ref, z_ref,
                  xe_ref, *, T, nchunk):
    i = pl.program_id(1)
    _fill_normed_ext(xe_ref, x_ref[...], xp_ref[...], xn_ref[...], g_ref[...], T,
                     i == 0, i == pl.num_programs(1) - 1)
    xe = xe_ref[...]
    hc = HY_CHUNK
    for j in range(nchunk):
        h = jnp.dot(xe, win_ref[j], preferred_element_type=F32)
        u = _conv3_rows(h, cw_ref[j], cb_ref[j], T)
        x0_ref[:, j * hc:(j + 1) * hc] = u[:, :hc]
        z_ref[:, j * hc:(j + 1) * hc] = u[:, hc:2 * hc] * u[:, 2 * hc:]


def _hy_in(x, gain, win_r, cw_r, cb_r, T):
    B, L, D = x.shape
    nchunk = win_r.shape[0]
    kern = functools.partial(_hy_in_kernel, T=T, nchunk=nchunk)
    out_spec = pl.BlockSpec((None, T, D), lambda b, i: (b, i, 0))
    return pl.pallas_call(
        kern,
        grid=(B, L // T),
        in_specs=_halo_specs(T, L, D) + [
            _const_spec((1, D)), _const_spec(win_r.shape), _const_spec(cw_r.shape),
            _const_spec(cb_r.shape)],
        out_specs=[out_spec, out_spec],
        out_shape=[jax.ShapeDtypeStruct(x.shape, F32)] * 2,
        scratch_shapes=[pltpu.VMEM((T + 2 * HALO, D), BF16)],
        compiler_params=_params(2),
        name="hyena_in",
    )(x, x, x, gain, win_r, cw_r, cb_r)


def _hy_out_kernel(x_ref, x0_ref, z_ref, y_ref, d_ref, w_ref, o_ref):
    zz = y_ref[...] + z_ref[...] * d_ref[...]
    m = (x0_ref[...] * zz).astype(BF16)
    o_ref[...] = x_ref[...] + jnp.dot(m, w_ref[...], preferred_element_type=F32)


def _hy_out(x, x0, z, y, d, w, T):
    B, L, D = x.shape
    spec = pl.BlockSpec((None, T, D), lambda b, i: (b, i, 0))
    return pl.pallas_call(
        _hy_out_kernel,
        grid=(B, L // T),
        in_specs=[spec, spec, spec, spec, _const_spec((1, D)), _const_spec(w.shape)],
        out_specs=spec,
        out_shape=jax.ShapeDtypeStruct(x.shape, F32),
        compiler_params=_params(2),
        name="hyena_out",
    )(x, x0, z, y, d, w)


def _dot_f32(a, b):
    return jnp.dot(a, b, preferred_element_type=F32, precision=lax.Precision.HIGHEST)


def _filter_kernel(frl_ref, w1_ref, b1_ref, f1_ref, w2_ref, b2_ref, f2_ref, w3_ref, b3_ref,
                   f3_ref, wout_ref, delta_ref, o_ref, *, L, R):
    r0 = pl.program_id(0) * R
    m = r0 + lax.broadcasted_iota(jnp.int32, (R, 1), 0)
    j = jnp.where(m > L, 2 * L - m, m).astype(F32)
    t = j / float(L - 1)
    w = (2.0 * math.pi) * j / float(L)
    lane = lax.broadcasted_iota(jnp.int32, (R, 128), 1)
    ang = w * frl_ref[...]
    feats = jnp.where(lane == 0, t,
                      jnp.where(lane <= FILTER_BANDS, jnp.cos(ang),
                                jnp.where(lane <= 2 * FILTER_BANDS, -jnp.sin(ang), 0.0)))
    h = jnp.sin(f1_ref[...] * (_dot_f32(feats, w1_ref[...]) + b1_ref[...]))
    h = jnp.sin(f2_ref[...] * (_dot_f32(h, w2_ref[...]) + b2_ref[...]))
    h = jnp.sin(f3_ref[...] * (_dot_f32(h, w3_ref[...]) + b3_ref[...]))
    h = _dot_f32(h, wout_ref[...])
    decay = jnp.exp(-t * delta_ref[...])
    o_ref[...] = jnp.where(m == L, 0.0, h * decay)


def _filter(L, frl, w1p, b1, f1, w2, b2, f2, w3, b3, f3, wout2, delta):
    R = 512
    half = L // R
    kern = functools.partial(_filter_kernel, L=L, R=R)
    small = [frl, w1p, b1, f1, w2, b2, f2, w3, b3, f3]
    return pl.pallas_call(
        kern,
        grid=(2 * L // R,),
        in_specs=[_const_spec(a.shape) for a in small] + [
            pl.BlockSpec((None, FILTER_WIDTH, D_MODEL), lambda i: (i // half, 0, 0)),
            _const_spec(delta.shape)],
        out_specs=pl.BlockSpec((R, D_MODEL), lambda i: (i, 0)),
        out_shape=jax.ShapeDtypeStruct((2 * L, D_MODEL), F32),
        compiler_params=_params(1),
        name="hyena_filter",
    )(*small, wout2, delta)


def _dft_consts(L):
    N = 2 * L
    N2 = FFT_N2
    N1 = N // N2
    a1 = -2.0 * np.pi * np.outer(np.arange(N1), np.arange(N1)) / N1
    f1r, f1i = np.cos(a1), np.sin(a1)
    a2 = -2.0 * np.pi * np.outer(np.arange(N2), np.arange(N2)) / N2
    f2r, f2i = np.cos(a2), np.sin(a2)
    at = -2.0 * np.pi * np.outer(np.arange(N1), np.arange(N2)) / N
    h = N1 // 2
    m1 = np.block([[f1r[:, :h], -f1i[:, :h]], [f1i[:, :h], f1r[:, :h]]])
    m1f = np.concatenate([f1r, f1i], axis=0)
    m2f = np.block([[f2r, -f2i], [f2i, f2r]])
    m2i = np.block([[f2r, f2i], [-f2i, f2r]])
    m3 = np.block([[f1r[:h], f1i[:h]], [-f1i[:h], f1r[:h]]])
    c = lambda a: jnp.asarray(a, dtype=BF16)
    tw = lambda a: jnp.asarray(a.reshape(N1, N2, 1), dtype=F32)
    return dict(N1=N1, N2=N2, m1=c(m1), m1f=c(m1f), m2f=c(m2f), m2i=c(m2i), m3=c(m3),
                twr=tw(np.cos(at)), twi=tw(np.sin(at)))


def _fft_s1_kernel(x_ref, m_ref, o_ref, *, N1):
    xs = x_ref[...]
    st = xs.reshape(xs.shape[0] * xs.shape[1], xs.shape[2]).astype(BF16)
    r = jnp.dot(m_ref[...], st, preferred_element_type=F32)
    o_ref[0] = r[:N1]
    o_ref[1] = r[N1:]


def _fft_s1(xv, m, N1, W):
    P, R, rows, cols = xv.shape
    kern = functools.partial(_fft_s1_kernel, N1=N1)
    return pl.pallas_call(
        kern,
        grid=(P, cols // W),
        in_specs=[pl.BlockSpec((None, R, rows, W), lambda p, w: (p, 0, 0, w)),
                  _const_spec(m.shape)],
        out_specs=pl.BlockSpec((None, 2, N1, W), lambda p, w: (p, 0, 0, w)),
        out_shape=jax.ShapeDtypeStruct((P, 2, N1, cols), F32),
        compiler_params=_params(2),
        name="fft_stage1",
    )(xv, m)


def _cmul(ar, ai, br, bi):
    return ar * br - ai * bi, ar * bi + ai * br


def _fft_s2_filter_kernel(a_ref, twr_ref, twi_ref, m2f_ref, o_ref, *, N2, scale):
    ar, ai = _cmul(a_ref[0], a_ref[1], twr_ref[...], twi_ref[...])
    st = jnp.concatenate([ar, ai], axis=0).astype(BF16)
    X = jnp.dot(m2f_ref[...], st, preferred_element_type=F32)
    o_ref[0] = X[:N2] * scale
    o_ref[1] = X[N2:] * scale


def _fft_s2_filter(a5, c, scale):
    _, _, N1, N2, C = a5.shape
    kern = functools.partial(_fft_s2_filter_kernel, N2=N2, scale=scale)
    tws = pl.BlockSpec((None, N2, 1), lambda k: (k, 0, 0))
    return pl.pallas_call(
        kern,
        grid=(N1,),
        in_specs=[pl.BlockSpec((None, 2, None, N2, C), lambda k: (0, 0, k, 0, 0)), tws, tws,
                  _const_spec(c["m2f"].shape)],
        out_specs=pl.BlockSpec((2, None, N2, C), lambda k: (0, k, 0, 0)),
        out_shape=jax.ShapeDtypeStruct((2, N1, N2, C), F32),
        compiler_params=_params(1),
        name="fft_filter_stage2",
    )(a5, c["twr"], c["twi"], c["m2f"])


def _fft_s2_kernel(a_ref, k_ref, twr_ref, twi_ref, m2f_ref, m2i_ref, o_ref, *, N2):
    twr, twi = twr_ref[...], twi_ref[...]
    ar, ai = _cmul(a_ref[0], a_ref[1], twr, twi)
    st = jnp.concatenate([ar, ai], axis=0).astype(BF16)
    X = jnp.dot(m2f_ref[...], st, preferred_element_type=F32)
    yr, yi = _cmul(X[:N2], X[N2:], k_ref[0], k_ref[1])
    st2 = jnp.concatenate([yr, yi], axis=0).astype(BF16)
    Bm = jnp.dot(m2i_ref[...], st2, preferred_element_type=F32)
    br, bi = _cmul(Bm[:N2], Bm[N2:], twr, -twi)
    o_ref[0] = br
    o_ref[1] = bi


def _fft_s2(a5, kf, c):
    P, _, N1, N2, C = a5.shape
    kern = functools.partial(_fft_s2_kernel, N2=N2)
    tws = pl.BlockSpec((None, N2, 1), lambda k, p: (k, 0, 0))
    blk = pl.BlockSpec((None, 2, None, N2, C), lambda k, p: (p, 0, k, 0, 0))
    return pl.pallas_call(
        kern,
        grid=(N1, P),
        in_specs=[blk, pl.BlockSpec((2, None, N2, C), lambda k, p: (0, k, 0, 0)), tws, tws,
                  _const_spec(c["m2f"].shape), _const_spec(c["m2i"].shape)],
        out_specs=blk,
        out_shape=jax.ShapeDtypeStruct(a5.shape, F32),
        compiler_params=_params(2),
        name="fft_stage2",
    )(a5, kf, c["twr"], c["twi"], c["m2f"], c["m2i"])


def _fft_s3_kernel(b_ref, m_ref, o_ref, *, h):
    bs = b_ref[...]
    st = bs.reshape(bs.shape[0] * bs.shape[1], bs.shape[2]).astype(BF16)
    r = jnp.dot(m_ref[...], st, preferred_element_type=F32)
    o_ref[0] = r[:h]
    o_ref[1] = r[h:]


def _fft_s3(b4, m3, W):
    P, _, N1, cols = b4.shape
    h = N1 // 2
    kern = functools.partial(_fft_s3_kernel, h=h)
    return pl.pallas_call(
        kern,
        grid=(P, cols // W),
        in_specs=[pl.BlockSpec((None, 2, N1, W), lambda p, w: (p, 0, 0, w)),
                  _const_spec(m3.shape)],
        out_specs=pl.BlockSpec((None, 2, h, W), lambda p, w: (p, 0, 0, w)),
        out_shape=jax.ShapeDtypeStruct((P, 2, h, cols), F32),
        compiler_params=_params(2),
        name="fft_stage3",
    )(b4, m3)


def _fft_width(cols):
    return min(cols, 4096)


def _filter_spectrum(kfull, c):
    N, C = kfull.shape
    N1, N2 = c["N1"], c["N2"]
    kv = kfull.reshape(1, 1, N1, N2 * C)
    a = _fft_s1(kv, c["m1f"], N1, _fft_width(N2 * C))
    return _fft_s2_filter(a.reshape(1, 2, N1, N2, C), c, 1.0 / N)


def _fftconv(z, kf, c):
    B, L, C = z.shape
    N1, N2 = c["N1"], c["N2"]
    W = _fft_width(N2 * C)
    zv = z.reshape(B // 2, 2, N1 // 2, N2 * C)
    a = _fft_s1(zv, c["m1"], N1, W)
    b = _fft_s2(a.reshape(B // 2, 2, N1, N2, C), kf, c)
    y = _fft_s3(b.reshape(B // 2, 2, N1, N2 * C), c["m3"], W)
    return y.reshape(B, L, C)


def _t5_bucket(rel):
    half = N_BUCKETS // 2
    max_exact = half // 2
    ret = jnp.where(rel > 0, half, 0)
    n = jnp.abs(rel)
    nf = jnp.maximum(n, 1).astype(jnp.float32)
    large = max_exact + (jnp.log(nf / max_exact) / math.log(MAX_DISTANCE / max_exact)
                         * (half - max_exact)).astype(jnp.int32)
    large = jnp.minimum(large, half - 1)
    return ret + jnp.where(n < max_exact, n, large)


def _bias_kernel(rb_ref, bucket_ref, rel_ref, o_ref):
    bucket = bucket_ref[...]
    inside = jnp.abs(rel_ref[...]) <= WINDOW
    for h in range(N_HEADS_A):
        acc = jnp.zeros(bucket.shape, F32)
        for b in range(N_BUCKETS):
            acc = jnp.where(bucket == b, rb_ref[b, h], acc)
        o_ref[h] = jnp.where(inside, acc, NEG)


def _bias_table(rel_bias):
    rel = (jnp.arange(3 * BLOCK)[None, :] - BLOCK) - jnp.arange(BLOCK)[:, None]
    rel = rel.astype(jnp.int32)
    bucket = _t5_bucket(rel).astype(jnp.int32)
    return pl.pallas_call(
        _bias_kernel,
        in_specs=[pl.BlockSpec(memory_space=pltpu.SMEM),
                  pl.BlockSpec(memory_space=pltpu.VMEM), pl.BlockSpec(memory_space=pltpu.VMEM)],
        out_specs=pl.BlockSpec(memory_space=pltpu.VMEM),
        out_shape=jax.ShapeDtypeStruct((N_HEADS_A, BLOCK, 3 * BLOCK), F32),
        name="rel_bias_table",
    )(rel_bias.astype(F32), bucket, rel)


def _group_ms(v, gmat):
    sq = v * v
    hi = sq.astype(BF16)
    lo = (sq - hi.astype(F32)).astype(BF16)
    return (jnp.dot(hi, gmat, preferred_element_type=F32)
            + jnp.dot(lo, gmat, preferred_element_type=F32))


def _ab_kernel(sink_ref, x_ref, xp_ref, xn_ref, g_ref, win_ref, wout_ref, gm_ref, qg_ref,
               kg_ref, bias_ref, vg_ref, ws_ref, bs_ref, o_ref, *, T):
    i = pl.program_id(1)
    nb = T // BLOCK
    nblocks = pl.num_programs(1) * nb
    gain = g_ref[...]
    x = x_ref[...]
    proj = jnp.dot(_rms(x, gain).astype(BF16), win_ref[...], preferred_element_type=F32)
    xh = jnp.concatenate([_rms(xp_ref[...], gain), _rms(xn_ref[...], gain)], axis=0)
    kvh = jnp.dot(xh.astype(BF16), win_ref[:, ATTN_W:ATTN_W + 2 * BLOCK],
                  preferred_element_type=F32)
    gm = gm_ref[...]
    gm_k = gm[:BLOCK, :BLOCK]

    q = proj[:, :ATTN_W]
    qn = q * lax.rsqrt(_group_ms(q, gm) + EPS) * qg_ref[...]
    k_all = jnp.concatenate([kvh[:BLOCK, :BLOCK], proj[:, ATTN_W:ATTN_W + BLOCK],
                             kvh[BLOCK:, :BLOCK]], axis=0)
    v_all = jnp.concatenate([kvh[:BLOCK, BLOCK:], proj[:, ATTN_W + BLOCK:ATTN_W + 2 * BLOCK],
                             kvh[BLOCK:, BLOCK:]], axis=0)
    kn = k_all * lax.rsqrt(_group_ms(k_all, gm_k) + EPS) * kg_ref[...]

    lane = lax.broadcasted_iota(jnp.int32, (1, 3 * BLOCK), 1)
    attn_rows = []
    for n in range(nb):
        blk = i * nb + n
        edge = (jnp.where((lane < BLOCK) & (blk == 0), NEG, 0.0)
                + jnp.where((lane >= 2 * BLOCK) & (blk == nblocks - 1), NEG, 0.0))
        heads = []
        for hk in range(N_KV_A):
            lo = hk * HEAD_DIM
            kb = kn[n * BLOCK:n * BLOCK + 3 * BLOCK, lo:lo + HEAD_DIM].astype(BF16)
            vb = v_all[n * BLOCK:n * BLOCK + 3 * BLOCK, lo:lo + HEAD_DIM].astype(BF16)
            qs = jnp.concatenate(
                [qn[n * BLOCK:(n + 1) * BLOCK, (hk * GQA + g) * HEAD_DIM:(hk * GQA + g + 1) * HEAD_DIM]
                 for g in range(GQA)], axis=0).astype(BF16)
            s = lax.dot_general(qs, kb, (((1,), (1,)), ((), ())), preferred_element_type=F32)
            bias = jnp.concatenate([bias_ref[hk * GQA + g] for g in range(GQA)], axis=0)
            s = s + bias + edge
            sk = jnp.concatenate([jnp.full((BLOCK, 1), sink_ref[hk * GQA + g], F32)
                                  for g in range(GQA)], axis=0)
            m = jnp.maximum(jnp.max(s, axis=-1, keepdims=True), sk)
            p = jnp.exp(s - m)
            den = jnp.sum(p, axis=-1, keepdims=True) + jnp.exp(sk - m)
            o = jnp.dot(p.astype(BF16), vb, preferred_element_type=F32) / den
            heads += [o[g * BLOCK:(g + 1) * BLOCK] for g in range(GQA)]
        attn_rows.append(jnp.concatenate(heads, axis=1))
    attn = jnp.concatenate(attn_rows, axis=0)

    su = _gelu(proj[:, ATTN_W + 2 * BLOCK:ATTN_W + 2 * BLOCK + SGU_W])
    sv = _gelu(proj[:, ATTN_W + 2 * BLOCK + SGU_W:])
    svn = (sv * lax.rsqrt(_group_ms(sv, gm) + EPS) * vg_ref[...]).astype(BF16)
    low = lax.broadcasted_iota(jnp.int32, (1, BLOCK * nb), 1) % BLOCK < CH_B
    slabs = []
    for j in range(SGU_W // BLOCK):
        rhs = jnp.concatenate([svn[n * BLOCK:(n + 1) * BLOCK, j * BLOCK:(j + 1) * BLOCK]
                               for n in range(nb)], axis=1)
        a = jnp.dot(ws_ref[2 * j], rhs, preferred_element_type=F32)
        b = jnp.dot(ws_ref[2 * j + 1], rhs, preferred_element_type=F32)
        slabs.append(jnp.where(low, a, b))
    mixed = jnp.concatenate(
        [jnp.concatenate([slabs[j][:, n * BLOCK:(n + 1) * BLOCK] for j in range(SGU_W // BLOCK)],
                         axis=1) + bs_ref[...] for n in range(nb)], axis=0)
    sgu = su * mixed

    cat = jnp.concatenate([attn, sgu], axis=1).astype(BF16)
    o_ref[...] = x + jnp.dot(cat, wout_ref[...], preferred_element_type=F32)


def _ab_layer(x, sink, gain, win, wout, gm, qg, kg, bias_tab, vg, ws, bs, T):
    B, L, D = x.shape
    hb = T // BLOCK
    last = L // BLOCK - 1
    kern = functools.partial(_ab_kernel, T=T)
    consts = [gain, win, wout, gm, qg, kg, bias_tab, vg, ws, bs]
    return pl.pallas_call(
        kern,
        grid=(B, L // T),
        in_specs=[pl.BlockSpec(memory_space=pltpu.SMEM),
                  pl.BlockSpec((None, T, D), lambda b, i: (b, i, 0)),
                  pl.BlockSpec((None, BLOCK, D), lambda b, i: (b, jnp.maximum(i * hb - 1, 0), 0)),
                  pl.BlockSpec((None, BLOCK, D), lambda b, i: (b, jnp.minimum((i + 1) * hb, last), 0)),
                  ] + [_const_spec(a.shape) for a in consts],
        out_specs=pl.BlockSpec((None, T, D), lambda b, i: (b, i, 0)),
        out_shape=jax.ShapeDtypeStruct(x.shape, F32),
        compiler_params=_params(2),
        name="attn_sgu_mixer",
    )(sink, x, x, x, *consts)


def _chunk_cols(w, parts, chunk):
    lead = w.shape[:-1]
    n = w.shape[-1] // (parts * chunk)
    w = w.reshape(lead + (parts, n, chunk))
    w = jnp.moveaxis(w, -2, 0)
    return w.reshape((n,) + lead + (parts * chunk,))


def _prepare(p):
    depth = p["mix_norm"].shape[0]
    q = dict(depth=depth)
    q["mix_norm"] = [p["mix_norm"][l][None, :] for l in range(depth)]
    q["ffn_norm"] = [p["ffn_norm"][l][None, :] for l in range(depth)]
    q["ffn_wup"] = [_chunk_cols(p["ffn_w_up"][l], 2, FFN_CHUNK).astype(BF16) for l in range(depth)]
    q["ffn_cw"] = [_chunk_cols(p["ffn_conv_w"][l], 2, FFN_CHUNK) for l in range(depth)]
    q["ffn_cb"] = [_chunk_cols(p["ffn_conv_b"][l][None, :], 2, FFN_CHUNK) for l in range(depth)]
    q["ffn_wdn"] = [p["ffn_w_down"][l].astype(BF16) for l in range(depth)]

    n_even = p["ab_w_in"].shape[0]
    q["ab_win"] = [p["ab_w_in"][i].astype(BF16) for i in range(n_even)]
    q["ab_wout"] = [p["ab_w_out"][i].astype(BF16) for i in range(n_even)]
    q["qg"] = [jnp.tile(p["q_norm"][i], N_HEADS_A)[None, :] * (HEAD_DIM ** -0.5) for i in range(n_even)]
    q["kg"] = [jnp.tile(p["k_norm"][i], N_KV_A)[None, :] for i in range(n_even)]
    q["sink"] = [p["attn_sink"][i].astype(F32) for i in range(n_even)]
    q["vg"] = [p["sgu_v_norm"][i].reshape(1, SGU_W) for i in range(n_even)]
    q["ws"] = [p["sgu_w"][i].astype(BF16) for i in range(n_even)]
    q["bs"] = [jnp.repeat(p["sgu_b"][i].T, CH_B, axis=1) for i in range(n_even)]
    grp = np.arange(ATTN_W) // HEAD_DIM
    q["gm"] = jnp.asarray((grp[:, None] == grp[None, :]) / float(HEAD_DIM), dtype=BF16)

    n_odd = p["hy_w_in"].shape[0]
    q["hy_win"] = [_chunk_cols(p["hy_w_in"][i], 3, HY_CHUNK).astype(BF16) for i in range(n_odd)]
    q["hy_cw"] = [_chunk_cols(p["hy_conv_w"][i], 3, HY_CHUNK) for i in range(n_odd)]
    q["hy_cb"] = [_chunk_cols(p["hy_conv_b"][i][None, :], 3, HY_CHUNK) for i in range(n_odd)]
    q["hy_d"] = [p["hy_d"][i][None, :] for i in range(n_odd)]
    q["hy_wout"] = [p["hy_w_out"][i].astype(BF16) for i in range(n_odd)]
    lanes = np.arange(128)
    fr = np.linspace(1e-4, FILTER_BANDS - 1, FILTER_BANDS).astype(np.float32)
    frl = np.where((lanes >= 1) & (lanes <= 2 * FILTER_BANDS), fr[(lanes - 1) % FILTER_BANDS], 0.0)
    q["frl"] = jnp.asarray(frl[None, :], dtype=F32)
    q["delta"] = jnp.abs(jnp.linspace(MIN_DECAY, MAX_DECAY, D_MODEL, dtype=F32))[None, :]
    filt = []
    for i in range(n_odd):
        w1p = jnp.zeros((128, FILTER_WIDTH), F32).at[:FILTER_EMB].set(p["hy_f_w1"][i])
        row = lambda a: a[None, :].astype(F32)
        wout2 = p["hy_f_wout"][i].reshape(FILTER_WIDTH, 2, D_MODEL).transpose(1, 0, 2)
        filt.append((w1p, row(p["hy_f_b1"][i]), row(p["hy_f_freq1"][i]),
                     p["hy_f_w2"][i], row(p["hy_f_b2"][i]), row(p["hy_f_freq2"][i]),
                     p["hy_f_w3"][i], row(p["hy_f_b3"][i]), row(p["hy_f_freq3"][i]), wout2))
    q["filt"] = filt
    q["bias_tab"] = _bias_table(p["rel_bias"])
    return q


def _trunk(x, q, T=512):
    B, L, D = x.shape
    consts = _dft_consts(L)
    for l in range(q["depth"]):
        i = l // 2
        if l % 2 == 0:
            x = _ab_layer(x, q["sink"][i], q["mix_norm"][l], q["ab_win"][i], q["ab_wout"][i],
                          q["gm"], q["qg"][i], q["kg"][i], q["bias_tab"], q["vg"][i], q["ws"][i],
                          q["bs"][i], min(T, 256))
        else:
            kfull = _filter(L, q["frl"], *q["filt"][i], q["delta"])
            kf = _filter_spectrum(kfull, consts)
            x0, z = _hy_in(x, q["mix_norm"][l], q["hy_win"][i], q["hy_cw"][i], q["hy_cb"][i], T)
            y = _fftconv(z, kf, consts)
            x = _hy_out(x, x0, z, y, q["hy_d"][i], q["hy_wout"][i], T)
        x = _ffn(x, q["ffn_norm"][l], q["ffn_wup"][l], q["ffn_cw"][l], q["ffn_cb"][l],
                 q["ffn_wdn"][l], T)
    return x


def kernel(x_prompt, x_sample, rel_bias, mix_norm, ffn_norm, ab_w_in, q_norm, k_norm, attn_sink, sgu_v_norm, sgu_w, sgu_b, ab_w_out, hy_w_in, hy_conv_w, hy_conv_b, hy_f_w1, hy_f_b1, hy_f_freq1, hy_f_w2, hy_f_b2, hy_f_freq2, hy_f_w3, hy_f_b3, hy_f_freq3, hy_f_wout, hy_d, hy_w_out, ffn_w_up, ffn_conv_w, ffn_conv_b, ffn_w_down):
    p = dict(rel_bias=rel_bias, mix_norm=mix_norm, ffn_norm=ffn_norm, ab_w_in=ab_w_in,
             q_norm=q_norm, k_norm=k_norm, attn_sink=attn_sink, sgu_v_norm=sgu_v_norm,
             sgu_w=sgu_w, sgu_b=sgu_b, ab_w_out=ab_w_out, hy_w_in=hy_w_in, hy_conv_w=hy_conv_w,
             hy_conv_b=hy_conv_b, hy_f_w1=hy_f_w1, hy_f_b1=hy_f_b1, hy_f_freq1=hy_f_freq1,
             hy_f_w2=hy_f_w2, hy_f_b2=hy_f_b2, hy_f_freq2=hy_f_freq2, hy_f_w3=hy_f_w3,
             hy_f_b3=hy_f_b3, hy_f_freq3=hy_f_freq3, hy_f_wout=hy_f_wout, hy_d=hy_d,
             hy_w_out=hy_w_out, ffn_w_up=ffn_w_up, ffn_conv_w=ffn_conv_w, ffn_conv_b=ffn_conv_b,
             ffn_w_down=ffn_w_down)
    q = _prepare(p)
    return (_trunk(x_prompt, q), _trunk(x_sample, q))
```

```python
import functools
import math

import numpy as np
import jax
import jax.numpy as jnp
from jax import lax
from jax.experimental import pallas as pl
from jax.experimental.pallas import tpu as pltpu

F32 = jnp.float32
BF16 = jnp.bfloat16

D_MODEL = 1024
HEAD_DIM = 64
N_HEADS_A = 8
N_KV_A = 2
GQA = 4
ATTN_W = 512
WINDOW = 128
BLOCK = 128
N_BUCKETS = 32
MAX_DISTANCE = 128
SGU_W = 512
N_GROUPS_B = 8
CH_B = 64
FILTER_EMB = 33
FILTER_BANDS = 16
FILTER_WIDTH = 64
DECAY_TARGET = 1e-2
MIN_DECAY = math.log(DECAY_TARGET) / 1.5
MAX_DECAY = math.log(DECAY_TARGET) / 0.3
FFN_HIDDEN = 2816
EPS = 1e-6
NEG = -1e30

LANES = 128
SUBLANES = 8
FFN_CHUNK = 256
HY_CHUNK = 256
HALO = 16
ROW_BLOCK = 64
FFT_N2 = LANES
FFT_CH = 32
VMEM_LIMIT = 56 * 1024 * 1024


def _gelu(x):
    t = jnp.tanh(x * (0.7978845608028654 + (0.7978845608028654 * 0.044715) * (x * x)))
    hx = 0.5 * x
    return hx + hx * t


def _rms(x, gain):
    return x * lax.rsqrt(jnp.mean(x * x, axis=-1, keepdims=True) + EPS) * gain


def _const_spec(shape):
    nd = len(shape)
    return pl.BlockSpec(shape, lambda *_: (0,) * nd, pipeline_mode=pl.Buffered(1))


def _params(n_axes):
    return pltpu.CompilerParams(dimension_semantics=("arbitrary",) * n_axes,
                                vmem_limit_bytes=VMEM_LIMIT)


def _park_slabs(s_ref, base, h):
    for s in range(h.shape[1] // LANES):
        s_ref[base + s] = h[:, s * LANES:(s + 1) * LANES]


def _conv3_slab(s_ref, slab, w, b, r0, rows):
    lo = HALO + r0
    return (s_ref[slab, lo - 1:lo - 1 + rows, :] * w[0:1]
            + s_ref[slab, lo:lo + rows, :] * w[1:2]
            + s_ref[slab, lo + 1:lo + 1 + rows, :] * w[2:3]
            + b)


def _fill_normed_ext(xe_ref, x, xp, xn, gain, rows, first, last):
    pm = jnp.where(first, 0.0, 1.0)
    nm = jnp.where(last, 0.0, 1.0)
    xe_ref[0:HALO, :] = (_rms(xp, gain) * pm).astype(BF16)
    xe_ref[HALO:HALO + rows, :] = _rms(x, gain).astype(BF16)
    xe_ref[HALO + rows:HALO + rows + HALO, :] = (_rms(xn, gain) * nm).astype(BF16)


def _halo_specs(T, L, D):
    hb = T // HALO
    last = L // HALO - 1
    return [
        pl.BlockSpec((None, T, D), lambda b, i: (b, i, 0)),
        pl.BlockSpec((None, HALO, D), lambda b, i: (b, jnp.maximum(i * hb - 1, 0), 0)),
        pl.BlockSpec((None, HALO, D), lambda b, i: (b, jnp.minimum((i + 1) * hb, last), 0)),
    ]


def _ffn_kernel(x_ref, xp_ref, xn_ref, g_ref, wup_ref, cw_ref, cb_ref, wdn_ref, o_ref,
                xe_ref, a_ref, s_ref, *, T, nchunk):
    i = pl.program_id(1)
    x = x_ref[...]
    _fill_normed_ext(xe_ref, x, xp_ref[...], xn_ref[...], g_ref[...], T,
                     i == 0, i == pl.num_programs(1) - 1)
    xe = xe_ref[...]
    ns = FFN_CHUNK // LANES
    for j in range(nchunk):
        base = (j % 2) * 2 * ns
        _park_slabs(s_ref, base, jnp.dot(xe, wup_ref[j], preferred_element_type=F32))
        cw, cb = cw_ref[j], cb_ref[j]
        for s in range(ns):
            gl, ul = s * LANES, (ns + s) * LANES
            wg, bg = cw[:, gl:gl + LANES], cb[:, gl:gl + LANES]
            wu, bu = cw[:, ul:ul + LANES], cb[:, ul:ul + LANES]
            col = j * FFN_CHUNK + s * LANES
            for r0 in range(0, T, ROW_BLOCK):
                g = _conv3_slab(s_ref, base + s, wg, bg, r0, ROW_BLOCK)
                u = _conv3_slab(s_ref, base + ns + s, wu, bu, r0, ROW_BLOCK)
                a_ref[r0:r0 + ROW_BLOCK, col:col + LANES] = (_gelu(g) * u).astype(BF16)
    o_ref[...] = x + jnp.dot(a_ref[...], wdn_ref[...], preferred_element_type=F32)


def _ffn(x, gain, wup_r, cw_r, cb_r, wdn, T):
    B, L, D = x.shape
    nchunk = wup_r.shape[0]
    kern = functools.partial(_ffn_kernel, T=T, nchunk=nchunk)
    return pl.pallas_call(
        kern,
        grid=(B, L // T),
        in_specs=_halo_specs(T, L, D) + [
            _const_spec((1, D)), _const_spec(wup_r.shape), _const_spec(cw_r.shape),
            _const_spec(cb_r.shape), _const_spec(wdn.shape)],
        out_specs=pl.BlockSpec((None, T, D), lambda b, i: (b, i, 0)),
        out_shape=jax.ShapeDtypeStruct(x.shape, F32),
        scratch_shapes=[pltpu.VMEM((T + 2 * HALO, D), BF16), pltpu.VMEM((T, FFN_HIDDEN), BF16),
                        pltpu.VMEM((2 * 2 * FFN_CHUNK // LANES, T + 2 * HALO, LANES), F32)],
        compiler_params=_params(2),
        name="conv_ffn",
    )(x, x, x, gain, wup_r, cw_r, cb_r, wdn)


def _hy_in_kernel(x_ref, xp_ref, xn_ref, g_ref, win_ref, cw_ref, cb_ref, x0_ref, zt_ref,
                  xe_ref, s_ref, *, T, nchunk):
    i = pl.program_id(1)
    _fill_normed_ext(xe_ref, x_ref[...], xp_ref[...], xn_ref[...], g_ref[...], T,
                     i == 0, i == pl.num_programs(1) - 1)
    xe = xe_ref[...]
    ns = HY_CHUNK // LANES
    for j in range(nchunk):
        base = (j % 2) * 3 * ns
        _park_slabs(s_ref, base, jnp.dot(xe, win_ref[j], preferred_element_type=F32))
        cw, cb = cw_ref[j], cb_ref[j]
        for s in range(ns):
            col = j * HY_CHUNK + s * LANES
            w = [cw[:, (k * ns + s) * LANES:(k * ns + s + 1) * LANES] for k in range(3)]
            b = [cb[:, (k * ns + s) * LANES:(k * ns + s + 1) * LANES] for k in range(3)]
            for n in range(T // BLOCK):
                r0 = n * BLOCK
                x0_ref[r0:r0 + BLOCK, col:col + LANES] = _conv3_slab(s_ref, base + s, w[0], b[0], r0, BLOCK)
                z = (_conv3_slab(s_ref, base + ns + s, w[1], b[1], r0, BLOCK)
                     * _conv3_slab(s_ref, base + 2 * ns + s, w[2], b[2], r0, BLOCK))
                zt_ref[n, col:col + LANES, :] = z.T


def _hy_in(x, gain, win_r, cw_r, cb_r, T):
    B, L, D = x.shape
    nchunk = win_r.shape[0]
    kern = functools.partial(_hy_in_kernel, T=T, nchunk=nchunk)
    return pl.pallas_call(
        kern,
        grid=(B, L // T),
        in_specs=_halo_specs(T, L, D) + [
            _const_spec((1, D)), _const_spec(win_r.shape), _const_spec(cw_r.shape),
            _const_spec(cb_r.shape)],
        out_specs=[pl.BlockSpec((None, T, D), lambda b, i: (b, i, 0)),
                   pl.BlockSpec((None, T // BLOCK, D, LANES), lambda b, i: (b, i, 0, 0))],
        out_shape=[jax.ShapeDtypeStruct(x.shape, F32),
                   jax.ShapeDtypeStruct((B, L // BLOCK, D, LANES), F32)],
        scratch_shapes=[pltpu.VMEM((T + 2 * HALO, D), BF16),
                        pltpu.VMEM((2 * 3 * HY_CHUNK // LANES, T + 2 * HALO, LANES), F32)],
        compiler_params=_params(2),
        name="hyena_in",
    )(x, x, x, gain, win_r, cw_r, cb_r)


def _hy_out_kernel(x_ref, x0_ref, yt_ref, w_ref, o_ref, m_ref, *, T):
    for n in range(T // BLOCK):
        rows = slice(n * BLOCK, (n + 1) * BLOCK)
        m_ref[rows, :] = (x0_ref[rows, :] * yt_ref[n].T).astype(BF16)
    o_ref[...] = x_ref[...] + jnp.dot(m_ref[...], w_ref[...], preferred_element_type=F32)


def _hy_out(x, x0, yt, w, T):
    B, L, D = x.shape
    spec = pl.BlockSpec((None, T, D), lambda b, i: (b, i, 0))
    kern = functools.partial(_hy_out_kernel, T=T)
    return pl.pallas_call(
        kern,
        grid=(B, L // T),
        in_specs=[spec, spec, pl.BlockSpec((None, T // BLOCK, D, LANES), lambda b, i: (b, i, 0, 0)),
                  _const_spec(w.shape)],
        out_specs=spec,
        out_shape=jax.ShapeDtypeStruct(x.shape, F32),
        scratch_shapes=[pltpu.VMEM((T, D), BF16)],
        compiler_params=_params(2),
        name="hyena_out",
    )(x, x0, yt, w)


def _dot_f32(a, b):
    return jnp.dot(a, b, preferred_element_type=F32, precision=lax.Precision.HIGHEST)


def _filter_kernel(frl_ref, w1_ref, b1_ref, f1_ref, w2_ref, b2_ref, f2_ref, w3_ref, b3_ref,
                   f3_ref, wout_ref, delta_ref, hf_ref, hb_ref, *, L, R):
    jidx = pl.program_id(0) * R + lax.broadcasted_iota(jnp.int32, (R, 1), 0)
    j = jidx.astype(F32)
    t = j / float(L - 1)
    w = (2.0 * math.pi) * j / float(L)
    lane = lax.broadcasted_iota(jnp.int32, (R, LANES), 1)
    ang = w * frl_ref[...]
    feats = jnp.where(lane == 0, t,
                      jnp.where(lane <= FILTER_BANDS, jnp.cos(ang),
                                jnp.where(lane <= 2 * FILTER_BANDS, -jnp.sin(ang), 0.0)))
    h = jnp.sin(f1_ref[...] * (_dot_f32(feats, w1_ref[...]) + b1_ref[...]))
    h = jnp.sin(f2_ref[...] * (_dot_f32(h, w2_ref[...]) + b2_ref[...]))
    h = jnp.sin(f3_ref[...] * (_dot_f32(h, w3_ref[...]) + b3_ref[...]))
    h = _dot_f32(h, wout_ref[...])
    decay = jnp.exp(-t * delta_ref[...])
    hf = h[:, :D_MODEL] * decay
    hb = jnp.where(jidx == 0, 0.0, h[:, D_MODEL:] * decay)
    for n in range(R // BLOCK):
        hf_ref[n] = hf[n * BLOCK:(n + 1) * BLOCK, :].T
        hb_ref[n] = hb[n * BLOCK:(n + 1) * BLOCK, :].T


def _filter(L, frl, w1p, b1, f1, w2, b2, f2, w3, b3, f3, wout, delta):
    R = 512
    kern = functools.partial(_filter_kernel, L=L, R=R)
    consts = [frl, w1p, b1, f1, w2, b2, f2, w3, b3, f3, wout, delta]
    out_spec = pl.BlockSpec((R // BLOCK, D_MODEL, LANES), lambda i: (i, 0, 0))
    shape = jax.ShapeDtypeStruct((L // BLOCK, D_MODEL, LANES), F32)
    return pl.pallas_call(
        kern,
        grid=(L // R,),
        in_specs=[_const_spec(a.shape) for a in consts],
        out_specs=[out_spec, out_spec],
        out_shape=[shape, shape],
        compiler_params=_params(1),
        name="hyena_filter",
    )(*consts)


def _dft_consts(L):
    N = 2 * L
    N2 = FFT_N2
    N1 = N // N2
    h = N1 // 2
    a1 = -2.0 * np.pi * np.outer(np.arange(N1), np.arange(N1)) / N1
    f1r, f1i = np.cos(a1), np.sin(a1)
    a2 = -2.0 * np.pi * np.outer(np.arange(N2), np.arange(N2)) / N2
    f2r, f2i = np.cos(a2), np.sin(a2)
    at = -2.0 * np.pi * np.outer(np.arange(N1), np.arange(N2)) / N
    eye = np.eye(SUBLANES)
    m1 = np.block([[f1r[:, :h], -f1i[:, :h]], [f1i[:, :h], f1r[:, :h]]])
    m1f = np.concatenate([f1r[:, :h], f1i[:, :h]], axis=0)
    m3 = np.block([[f1r[:h], f1i[:h]], [-f1i[:h], f1r[:h]]])
    c = lambda a: jnp.asarray(a, dtype=BF16)
    rep = lambda a: jnp.asarray(np.repeat(a, SUBLANES, axis=0), dtype=F32)
    return dict(N1=N1, k1=c(np.kron(m1, eye)), k1f=c(np.kron(m1f, eye)), k3=c(np.kron(m3, eye)),
                r2f=c(np.block([[f2r, f2i], [-f2i, f2r]])),
                r2i=c(np.block([[f2r, -f2i], [f2i, f2r]])),
                twr=rep(np.cos(at)), twi=rep(np.sin(at)))


def _cmul(ar, ai, br, bi):
    return ar * br - ai * bi, ar * bi + ai * br


def _rows8(ref, idx, c0, n):
    v = ref[idx + (slice(None), slice(c0, c0 + SUBLANES), slice(None))]
    return v.reshape(n * SUBLANES, LANES)


def _twiddle_dft2(a, twr, twi, r2_ref, R):
    lhs = []
    for s in range(2):
        tr, ti = _cmul(a[:R, s * LANES:(s + 1) * LANES], a[R:, s * LANES:(s + 1) * LANES], twr, twi)
        lhs.append(jnp.concatenate([tr, ti], axis=1))
    lhs = jnp.concatenate(lhs, axis=0).astype(BF16)
    return jnp.dot(lhs, r2_ref[...], preferred_element_type=F32)


def _fft_kernel(x_ref, kf_ref, k1_ref, k3_ref, twr_ref, twi_ref, r2f_ref, r2i_ref, o_ref,
                *, N1, nc):
    h = N1 // 2
    R = N1 * SUBLANES
    twr, twi = twr_ref[...], twi_ref[...]
    for g in range(nc // (2 * SUBLANES)):
        subs = (g * 2 * SUBLANES, g * 2 * SUBLANES + SUBLANES)
        v = jnp.concatenate(
            [jnp.concatenate([_rows8(x_ref, (0,), c0, h), _rows8(x_ref, (1,), c0, h)], axis=0)
             for c0 in subs], axis=1).astype(BF16)
        a = jnp.dot(k1_ref[...], v, preferred_element_type=F32)
        X = _twiddle_dft2(a, twr, twi, r2f_ref, R)
        ys = []
        for s, c0 in enumerate(subs):
            yr, yi = _cmul(X[s * R:(s + 1) * R, :LANES], X[s * R:(s + 1) * R, LANES:],
                           _rows8(kf_ref, (0,), c0, N1), _rows8(kf_ref, (1,), c0, N1))
            ys.append(jnp.concatenate([yr, yi], axis=1))
        ys = jnp.concatenate(ys, axis=0).astype(BF16)
        Bm = jnp.dot(ys, r2i_ref[...], preferred_element_type=F32)
        cols = []
        for s in range(2):
            br, bi = Bm[s * R:(s + 1) * R, :LANES], Bm[s * R:(s + 1) * R, LANES:]
            cols.append(jnp.concatenate([br * twr + bi * twi, bi * twr - br * twi], axis=0))
        rhs = jnp.concatenate(cols, axis=1).astype(BF16)
        y = jnp.dot(k3_ref[...], rhs, preferred_element_type=F32)
        hr = h * SUBLANES
        for s, c0 in enumerate(subs):
            o_ref[0, :, c0:c0 + SUBLANES, :] = y[:hr, s * LANES:(s + 1) * LANES].reshape(h, SUBLANES, LANES)
            o_ref[1, :, c0:c0 + SUBLANES, :] = y[hr:, s * LANES:(s + 1) * LANES].reshape(h, SUBLANES, LANES)


def _fftconv(zt, kf, c):
    B, h, C, _ = zt.shape
    N1 = c["N1"]
    nc = FFT_CH
    kern = functools.partial(_fft_kernel, N1=N1, nc=nc)
    blk = pl.BlockSpec((2, h, nc, LANES), lambda cb, p: (p, 0, cb, 0))
    consts = [c["k1"], c["k3"], c["twr"], c["twi"], c["r2f"], c["r2i"]]
    return pl.pallas_call(
        kern,
        grid=(C // nc, B // 2),
        in_specs=[blk, pl.BlockSpec((2, N1, nc, LANES), lambda cb, p: (0, 0, cb, 0))]
        + [_const_spec(a.shape) for a in consts],
        out_specs=blk,
        out_shape=jax.ShapeDtypeStruct(zt.shape, F32),
        compiler_params=_params(2),
        name="fft_conv",
    )(zt, kf, *consts)


def _fspec_kernel(hf_ref, hb_ref, d_ref, k1f_ref, twr_ref, twi_ref, r2f_ref, o_ref,
                  *, N1, nc, scale):
    h = N1 // 2
    R = N1 * SUBLANES
    twr, twi = twr_ref[...], twi_ref[...]
    for s8 in range(nc // SUBLANES):
        c0 = s8 * SUBLANES
        v = jnp.concatenate([_rows8(hf_ref, (), c0, h), _rows8(hb_ref, (), c0, h)],
                            axis=1).astype(BF16)
        a = jnp.dot(k1f_ref[...], v, preferred_element_type=F32)
        X = _twiddle_dft2(a, twr, twi, r2f_ref, R)
        d8 = jnp.tile(d_ref[c0:c0 + SUBLANES, :], (N1, 1))
        kr = (X[:R, :LANES] + X[R:, :LANES] + d8) * scale
        ki = (X[:R, LANES:] - X[R:, LANES:]) * scale
        o_ref[0, :, c0:c0 + SUBLANES, :] = kr.reshape(N1, SUBLANES, LANES)
        o_ref[1, :, c0:c0 + SUBLANES, :] = ki.reshape(N1, SUBLANES, LANES)


def _filter_spectrum(hf, hb, d, c):
    h, C, _ = hf.shape
    N1 = c["N1"]
    nc = FFT_CH
    kern = functools.partial(_fspec_kernel, N1=N1, nc=nc, scale=1.0 / (N1 * FFT_N2))
    blk = pl.BlockSpec((h, nc, LANES), lambda cb: (0, cb, 0))
    consts = [c["k1f"], c["twr"], c["twi"], c["r2f"]]
    return pl.pallas_call(
        kern,
        grid=(C // nc,),
        in_specs=[blk, blk, pl.BlockSpec((nc, 1), lambda cb: (cb, 0))]
        + [_const_spec(a.shape) for a in consts],
        out_specs=pl.BlockSpec((2, N1, nc, LANES), lambda cb: (0, 0, cb, 0)),
        out_shape=jax.ShapeDtypeStruct((2, N1, C, LANES), F32),
        compiler_params=_params(1),
        name="filter_spectrum",
    )(hf, hb, d, *consts)


def _t5_bucket(rel):
    half = N_BUCKETS // 2
    max_exact = half // 2
    ret = jnp.where(rel > 0, half, 0)
    n = jnp.abs(rel)
    nf = jnp.maximum(n, 1).astype(jnp.float32)
    large = max_exact + (jnp.log(nf / max_exact) / math.log(MAX_DISTANCE / max_exact)
                         * (half - max_exact)).astype(jnp.int32)
    large = jnp.minimum(large, half - 1)
    return ret + jnp.where(n < max_exact, n, large)


def _bias_kernel(rb_ref, bucket_ref, rel_ref, o_ref):
    bucket = bucket_ref[...]
    inside = jnp.abs(rel_ref[...]) <= WINDOW
    for h in range(N_HEADS_A):
        acc = jnp.zeros(bucket.shape, F32)
        for b in range(N_BUCKETS):
            acc = jnp.where(bucket == b, rb_ref[b, h], acc)
        o_ref[h] = jnp.where(inside, acc, NEG)


def _bias_table(rel_bias):
    rel = (jnp.arange(3 * BLOCK)[:, None] - BLOCK) - jnp.arange(BLOCK)[None, :]
    rel = rel.astype(jnp.int32)
    bucket = _t5_bucket(rel).astype(jnp.int32)
    return pl.pallas_call(
        _bias_kernel,
        in_specs=[pl.BlockSpec(memory_space=pltpu.SMEM),
                  pl.BlockSpec(memory_space=pltpu.VMEM), pl.BlockSpec(memory_space=pltpu.VMEM)],
        out_specs=pl.BlockSpec(memory_space=pltpu.VMEM),
        out_shape=jax.ShapeDtypeStruct((N_HEADS_A, 3 * BLOCK, BLOCK), F32),
        name="rel_bias_table",
    )(rel_bias.astype(F32), bucket, rel)


def _group_ms(v, gmat):
    return jnp.dot((v * v).astype(BF16), gmat, preferred_element_type=F32)


def _ab_kernel(sink_ref, x_ref, xp_ref, xn_ref, g_ref, win_ref, wout_ref, gm_ref, qg_ref,
               kg_ref, bias_ref, vg_ref, ws_ref, bs_ref, o_ref, *, T):
    i = pl.program_id(1)
    nb = T // BLOCK
    nblocks = pl.num_programs(1) * nb
    gain = g_ref[...]
    x = x_ref[...]
    proj = jnp.dot(_rms(x, gain).astype(BF16), win_ref[...], preferred_element_type=F32)
    xh = jnp.concatenate([_rms(xp_ref[...], gain), _rms(xn_ref[...], gain)], axis=0)
    kvh = jnp.dot(xh.astype(BF16), win_ref[:, ATTN_W:ATTN_W + 2 * BLOCK],
                  preferred_element_type=F32)
    gm = gm_ref[...]
    gm_k = gm[:BLOCK, :BLOCK]

    q = proj[:, :ATTN_W]
    qn = q * lax.rsqrt(_group_ms(q, gm) + EPS) * qg_ref[...]
    k_all = jnp.concatenate([kvh[:BLOCK, :BLOCK], proj[:, ATTN_W:ATTN_W + BLOCK],
                             kvh[BLOCK:, :BLOCK]], axis=0)
    v_all = jnp.concatenate([kvh[:BLOCK, BLOCK:], proj[:, ATTN_W + BLOCK:ATTN_W + 2 * BLOCK],
                             kvh[BLOCK:, BLOCK:]], axis=0)
    kn = k_all * lax.rsqrt(_group_ms(k_all, gm_k) + EPS) * kg_ref[...]

    qt = qn.T.astype(BF16)
    vt = v_all.T.astype(BF16)
    knb = kn.astype(BF16)
    zeros_q = jnp.zeros((HEAD_DIM, GQA * BLOCK), BF16)
    out_cols = []
    for n in range(nb):
        blk = i * nb + n
        neg_prev = jnp.where(blk == 0, NEG, 0.0)
        neg_next = jnp.where(blk == nblocks - 1, NEG, 0.0)
        kb = knb[n * BLOCK:n * BLOCK + 3 * BLOCK, :]
        vtb = vt[:, n * BLOCK:n * BLOCK + 3 * BLOCK]
        head_rows = []
        for hk in range(N_KV_A):
            qh = jnp.concatenate(
                [qt[(hk * GQA + g) * HEAD_DIM:(hk * GQA + g + 1) * HEAD_DIM, n * BLOCK:(n + 1) * BLOCK]
                 for g in range(GQA)], axis=1)
            qz = jnp.concatenate([qh, zeros_q] if hk == 0 else [zeros_q, qh], axis=0)
            s = jnp.dot(kb, qz, preferred_element_type=F32)
            s = s + jnp.concatenate([bias_ref[hk * GQA + g] for g in range(GQA)], axis=1)
            s = jnp.concatenate([s[:BLOCK] + neg_prev, s[BLOCK:2 * BLOCK], s[2 * BLOCK:] + neg_next],
                                axis=0)
            sk = jnp.concatenate([jnp.full((1, BLOCK), sink_ref[hk * GQA + g], F32)
                                  for g in range(GQA)], axis=1)
            m = jnp.maximum(jnp.max(s, axis=0, keepdims=True), sk)
            p = jnp.exp(s - m)
            den = jnp.sum(p, axis=0, keepdims=True) + jnp.exp(sk - m)
            pv = jnp.dot(vtb, p.astype(BF16), preferred_element_type=F32)
            o = pv[hk * HEAD_DIM:(hk + 1) * HEAD_DIM] / den
            head_rows += [o[:, g * BLOCK:(g + 1) * BLOCK] for g in range(GQA)]
        out_cols.append(jnp.concatenate(head_rows, axis=0))
    attn = jnp.concatenate(out_cols, axis=1).T

    su = _gelu(proj[:, ATTN_W + 2 * BLOCK:ATTN_W + 2 * BLOCK + SGU_W])
    sv = _gelu(proj[:, ATTN_W + 2 * BLOCK + SGU_W:])
    svn = (sv * lax.rsqrt(_group_ms(sv, gm) + EPS) * vg_ref[...]).astype(BF16)
    low = lax.broadcasted_iota(jnp.int32, (1, BLOCK * nb), 1) % BLOCK < CH_B
    slabs = []
    for j in range(SGU_W // BLOCK):
        rhs = jnp.concatenate([svn[n * BLOCK:(n + 1) * BLOCK, j * BLOCK:(j + 1) * BLOCK]
                               for n in range(nb)], axis=1)
        a = jnp.dot(ws_ref[2 * j], rhs, preferred_element_type=F32)
        b = jnp.dot(ws_ref[2 * j + 1], rhs, preferred_element_type=F32)
        slabs.append(jnp.where(low, a, b))
    mixed = jnp.concatenate(
        [jnp.concatenate([slabs[j][:, n * BLOCK:(n + 1) * BLOCK] for j in range(SGU_W // BLOCK)],
                         axis=1) + bs_ref[...] for n in range(nb)], axis=0)
    sgu = su * mixed

    cat = jnp.concatenate([attn, sgu], axis=1).astype(BF16)
    o_ref[...] = x + jnp.dot(cat, wout_ref[...], preferred_element_type=F32)


def _ab_layer(x, sink, gain, win, wout, gm, qg, kg, bias_tab, vg, ws, bs, T):
    B, L, D = x.shape
    hb = T // BLOCK
    last = L // BLOCK - 1
    kern = functools.partial(_ab_kernel, T=T)
    consts = [gain, win, wout, gm, qg, kg, bias_tab, vg, ws, bs]
    return pl.pallas_call(
        kern,
        grid=(B, L // T),
        in_specs=[pl.BlockSpec(memory_space=pltpu.SMEM),
                  pl.BlockSpec((None, T, D), lambda b, i: (b, i, 0)),
                  pl.BlockSpec((None, BLOCK, D), lambda b, i: (b, jnp.maximum(i * hb - 1, 0), 0)),
                  pl.BlockSpec((None, BLOCK, D), lambda b, i: (b, jnp.minimum((i + 1) * hb, last), 0)),
                  ] + [_const_spec(a.shape) for a in consts],
        out_specs=pl.BlockSpec((None, T, D), lambda b, i: (b, i, 0)),
        out_shape=jax.ShapeDtypeStruct(x.shape, F32),
        compiler_params=_params(2),
        name="attn_sgu_mixer",
    )(sink, x, x, x, *consts)


def _chunk_cols(w, parts, chunk):
    lead = w.shape[:-1]
    n = w.shape[-1] // (parts * chunk)
    w = w.reshape(lead + (parts, n, chunk))
    w = jnp.moveaxis(w, -2, 0)
    return w.reshape((n,) + lead + (parts * chunk,))


def _prepare(p):
    depth = p["mix_norm"].shape[0]
    q = dict(depth=depth)
    q["mix_norm"] = [p["mix_norm"][l][None, :] for l in range(depth)]
    q["ffn_norm"] = [p["ffn_norm"][l][None, :] for l in range(depth)]
    q["ffn_wup"] = [_chunk_cols(p["ffn_w_up"][l], 2, FFN_CHUNK).astype(BF16) for l in range(depth)]
    q["ffn_cw"] = [_chunk_cols(p["ffn_conv_w"][l], 2, FFN_CHUNK) for l in range(depth)]
    q["ffn_cb"] = [_chunk_cols(p["ffn_conv_b"][l][None, :], 2, FFN_CHUNK) for l in range(depth)]
    q["ffn_wdn"] = [p["ffn_w_down"][l].astype(BF16) for l in range(depth)]

    n_even = p["ab_w_in"].shape[0]
    q["ab_win"] = [p["ab_w_in"][i].astype(BF16) for i in range(n_even)]
    q["ab_wout"] = [p["ab_w_out"][i].astype(BF16) for i in range(n_even)]
    q["qg"] = [jnp.tile(p["q_norm"][i], N_HEADS_A)[None, :] * (HEAD_DIM ** -0.5) for i in range(n_even)]
    q["kg"] = [jnp.tile(p["k_norm"][i], N_KV_A)[None, :] for i in range(n_even)]
    q["sink"] = [p["attn_sink"][i].astype(F32) for i in range(n_even)]
    q["vg"] = [p["sgu_v_norm"][i].reshape(1, SGU_W) for i in range(n_even)]
    q["ws"] = [p["sgu_w"][i].astype(BF16) for i in range(n_even)]
    q["bs"] = [jnp.repeat(p["sgu_b"][i].T, CH_B, axis=1) for i in range(n_even)]
    grp = np.arange(ATTN_W) // HEAD_DIM
    q["gm"] = jnp.asarray((grp[:, None] == grp[None, :]) / float(HEAD_DIM), dtype=BF16)

    n_odd = p["hy_w_in"].shape[0]
    q["hy_win"] = [_chunk_cols(p["hy_w_in"][i], 3, HY_CHUNK).astype(BF16) for i in range(n_odd)]
    q["hy_cw"] = [_chunk_cols(p["hy_conv_w"][i], 3, HY_CHUNK) for i in range(n_odd)]
    q["hy_cb"] = [_chunk_cols(p["hy_conv_b"][i][None, :], 3, HY_CHUNK) for i in range(n_odd)]
    q["hy_d"] = [p["hy_d"][i][:, None].astype(F32) for i in range(n_odd)]
    q["hy_wout"] = [p["hy_w_out"][i].astype(BF16) for i in range(n_odd)]
    lanes = np.arange(LANES)
    fr = np.linspace(1e-4, FILTER_BANDS - 1, FILTER_BANDS).astype(np.float32)
    frl = np.where((lanes >= 1) & (lanes <= 2 * FILTER_BANDS), fr[(lanes - 1) % FILTER_BANDS], 0.0)
    q["frl"] = jnp.asarray(frl[None, :], dtype=F32)
    q["delta"] = jnp.abs(jnp.linspace(MIN_DECAY, MAX_DECAY, D_MODEL, dtype=F32))[None, :]
    filt = []
    for i in range(n_odd):
        w1p = jnp.zeros((LANES, FILTER_WIDTH), F32).at[:FILTER_EMB].set(p["hy_f_w1"][i])
        row = lambda a: a[None, :].astype(F32)
        filt.append((w1p, row(p["hy_f_b1"][i]), row(p["hy_f_freq1"][i]),
                     p["hy_f_w2"][i], row(p["hy_f_b2"][i]), row(p["hy_f_freq2"][i]),
                     p["hy_f_w3"][i], row(p["hy_f_b3"][i]), row(p["hy_f_freq3"][i]),
                     p["hy_f_wout"][i]))
    q["filt"] = filt
    q["bias_tab"] = _bias_table(p["rel_bias"])
    return q


def _hyena_layer(x, q, l, consts, T):
    i = l // 2
    L = x.shape[1]
    hf, hb = _filter(L, q["frl"], *q["filt"][i], q["delta"])
    kf = _filter_spectrum(hf, hb, q["hy_d"][i], consts)
    x0, zt = _hy_in(x, q["mix_norm"][l], q["hy_win"][i], q["hy_cw"][i], q["hy_cb"][i], T)
    yt = _fftconv(zt, kf, consts)
    return _hy_out(x, x0, yt, q["hy_wout"][i], T)


def _trunk(x, q, T=512):
    B, L, D = x.shape
    consts = _dft_consts(L)
    for l in range(q["depth"]):
        i = l // 2
        if l % 2 == 0:
            x = _ab_layer(x, q["sink"][i], q["mix_norm"][l], q["ab_win"][i], q["ab_wout"][i],
                          q["gm"], q["qg"][i], q["kg"][i], q["bias_tab"], q["vg"][i], q["ws"][i],
                          q["bs"][i], T)
        else:
            x = _hyena_layer(x, q, l, consts, T)
        x = _ffn(x, q["ffn_norm"][l], q["ffn_wup"][l], q["ffn_cw"][l], q["ffn_cb"][l],
                 q["ffn_wdn"][l], T)
    return x


def kernel(x_prompt, x_sample, rel_bias, mix_norm, ffn_norm, ab_w_in, q_norm, k_norm, attn_sink, sgu_v_norm, sgu_w, sgu_b, ab_w_out, hy_w_in, hy_conv_w, hy_conv_b, hy_f_w1, hy_f_b1, hy_f_freq1, hy_f_w2, hy_f_b2, hy_f_freq2, hy_f_w3, hy_f_b3, hy_f_freq3, hy_f_wout, hy_d, hy_w_out, ffn_w_up, ffn_conv_w, ffn_conv_b, ffn_w_down):
    p = dict(rel_bias=rel_bias, mix_norm=mix_norm, ffn_norm=ffn_norm, ab_w_in=ab_w_in,
             q_norm=q_norm, k_norm=k_norm, attn_sink=attn_sink, sgu_v_norm=sgu_v_norm,
             sgu_w=sgu_w, sgu_b=sgu_b, ab_w_out=ab_w_out, hy_w_in=hy_w_in, hy_conv_w=hy_conv_w,
             hy_conv_b=hy_conv_b, hy_f_w1=hy_f_w1, hy_f_b1=hy_f_b1, hy_f_freq1=hy_f_freq1,
             hy_f_w2=hy_f_w2, hy_f_b2=hy_f_b2, hy_f_freq2=hy_f_freq2, hy_f_w3=hy_f_w3,
             hy_f_b3=hy_f_b3, hy_f_freq3=hy_f_freq3, hy_f_wout=hy_f_wout, hy_d=hy_d,
             hy_w_out=hy_w_out, ffn_w_up=ffn_w_up, ffn_conv_w=ffn_conv_w, ffn_conv_b=ffn_conv_b,
             ffn_w_down=ffn_w_down)
    q = _prepare(p)
    return (_trunk(x_prompt, q), _trunk(x_sample, q))
```

```python
import functools
import math

import numpy as np
import jax
import jax.numpy as jnp
from jax import lax
from jax.experimental import pallas as pl
from jax.experimental.pallas import tpu as pltpu

F32 = jnp.float32
BF16 = jnp.bfloat16

D_MODEL = 1024
HEAD_DIM = 64
N_HEADS_A = 8
N_KV_A = 2
GQA = 4
ATTN_W = 512
WINDOW = 128
BLOCK = 128
N_BUCKETS = 32
MAX_DISTANCE = 128
SGU_W = 512
N_GROUPS_B = 8
CH_B = 64
FILTER_EMB = 33
FILTER_BANDS = 16
FILTER_WIDTH = 64
DECAY_TARGET = 1e-2
MIN_DECAY = math.log(DECAY_TARGET) / 1.5
MAX_DECAY = math.log(DECAY_TARGET) / 0.3
FFN_HIDDEN = 2816
EPS = 1e-6
NEG = -1e30

LANES = 128
SUBLANES = 8
FFN_CHUNK = 256
HY_CHUNK = 256
HALO = 16
ROW_BLOCK = 64
FFT_N2 = LANES
FFT_CH = 32
VMEM_LIMIT = 56 * 1024 * 1024


def _gelu(x):
    t = jnp.tanh(x * (0.7978845608028654 + (0.7978845608028654 * 0.044715) * (x * x)))
    hx = 0.5 * x
    return hx + hx * t


def _rms(x, gain):
    return x * lax.rsqrt(jnp.mean(x * x, axis=-1, keepdims=True) + EPS) * gain


def _const_spec(shape):
    nd = len(shape)
    return pl.BlockSpec(shape, lambda *_: (0,) * nd, pipeline_mode=pl.Buffered(1))


def _params(n_axes):
    return pltpu.CompilerParams(dimension_semantics=("arbitrary",) * n_axes,
                                vmem_limit_bytes=VMEM_LIMIT)


def _park_slabs(s_ref, base, h):
    for s in range(h.shape[1] // LANES):
        s_ref[base + s] = h[:, s * LANES:(s + 1) * LANES]


def _conv3_slab(s_ref, slab, w, b, r0, rows):
    lo = HALO + r0
    return (s_ref[slab, lo - 1:lo - 1 + rows, :] * w[0:1]
            + s_ref[slab, lo:lo + rows, :] * w[1:2]
            + s_ref[slab, lo + 1:lo + 1 + rows, :] * w[2:3]
            + b)


def _fill_normed_ext(xe_ref, x, xp, xn, gain, rows, first, last):
    pm = jnp.where(first, 0.0, 1.0)
    nm = jnp.where(last, 0.0, 1.0)
    xe_ref[0:HALO, :] = (_rms(xp, gain) * pm).astype(BF16)
    xe_ref[HALO:HALO + rows, :] = _rms(x, gain).astype(BF16)
    xe_ref[HALO + rows:HALO + rows + HALO, :] = (_rms(xn, gain) * nm).astype(BF16)


def _halo_specs(T, L, D):
    hb = T // HALO
    last = L // HALO - 1
    return [
        pl.BlockSpec((None, T, D), lambda b, i: (b, i, 0)),
        pl.BlockSpec((None, HALO, D), lambda b, i: (b, jnp.maximum(i * hb - 1, 0), 0)),
        pl.BlockSpec((None, HALO, D), lambda b, i: (b, jnp.minimum((i + 1) * hb, last), 0)),
    ]


def _ffn_kernel(x_ref, xp_ref, xn_ref, g_ref, wup_ref, cw_ref, cb_ref, wdn_ref, o_ref,
                xe_ref, a_ref, s_ref, *, T, nchunk):
    i = pl.program_id(1)
    x = x_ref[...]
    _fill_normed_ext(xe_ref, x, xp_ref[...], xn_ref[...], g_ref[...], T,
                     i == 0, i == pl.num_programs(1) - 1)
    xe = xe_ref[...]
    ns = FFN_CHUNK // LANES
    for j in range(nchunk):
        base = (j % 2) * 2 * ns
        _park_slabs(s_ref, base, jnp.dot(xe, wup_ref[j], preferred_element_type=F32))
        cw, cb = cw_ref[j], cb_ref[j]
        for s in range(ns):
            gl, ul = s * LANES, (ns + s) * LANES
            wg, bg = cw[:, gl:gl + LANES], cb[:, gl:gl + LANES]
            wu, bu = cw[:, ul:ul + LANES], cb[:, ul:ul + LANES]
            col = j * FFN_CHUNK + s * LANES
            for r0 in range(0, T, ROW_BLOCK):
                g = _conv3_slab(s_ref, base + s, wg, bg, r0, ROW_BLOCK)
                u = _conv3_slab(s_ref, base + ns + s, wu, bu, r0, ROW_BLOCK)
                a_ref[r0:r0 + ROW_BLOCK, col:col + LANES] = (_gelu(g) * u).astype(BF16)
    o_ref[...] = x + jnp.dot(a_ref[...], wdn_ref[...], preferred_element_type=F32)


def _ffn(x, gain, wup_r, cw_r, cb_r, wdn, T):
    B, L, D = x.shape
    nchunk = wup_r.shape[0]
    kern = functools.partial(_ffn_kernel, T=T, nchunk=nchunk)
    return pl.pallas_call(
        kern,
        grid=(B, L // T),
        in_specs=_halo_specs(T, L, D) + [
            _const_spec((1, D)), _const_spec(wup_r.shape), _const_spec(cw_r.shape),
            _const_spec(cb_r.shape), _const_spec(wdn.shape)],
        out_specs=pl.BlockSpec((None, T, D), lambda b, i: (b, i, 0)),
        out_shape=jax.ShapeDtypeStruct(x.shape, F32),
        scratch_shapes=[pltpu.VMEM((T + 2 * HALO, D), BF16), pltpu.VMEM((T, FFN_HIDDEN), BF16),
                        pltpu.VMEM((2 * 2 * FFN_CHUNK // LANES, T + 2 * HALO, LANES), F32)],
        compiler_params=_params(2),
        name="conv_ffn",
    )(x, x, x, gain, wup_r, cw_r, cb_r, wdn)


def _hy_in_kernel(x_ref, xp_ref, xn_ref, g_ref, win_ref, cw_ref, cb_ref, x0_ref, zt_ref,
                  xe_ref, s_ref, *, T, nchunk):
    i = pl.program_id(1)
    _fill_normed_ext(xe_ref, x_ref[...], xp_ref[...], xn_ref[...], g_ref[...], T,
                     i == 0, i == pl.num_programs(1) - 1)
    xe = xe_ref[...]
    ns = HY_CHUNK // LANES
    for j in range(nchunk):
        base = (j % 2) * 3 * ns
        _park_slabs(s_ref, base, jnp.dot(xe, win_ref[j], preferred_element_type=F32))
        cw, cb = cw_ref[j], cb_ref[j]
        for s in range(ns):
            col = j * HY_CHUNK + s * LANES
            w = [cw[:, (k * ns + s) * LANES:(k * ns + s + 1) * LANES] for k in range(3)]
            b = [cb[:, (k * ns + s) * LANES:(k * ns + s + 1) * LANES] for k in range(3)]
            for n in range(T // BLOCK):
                r0 = n * BLOCK
                x0_ref[r0:r0 + BLOCK, col:col + LANES] = _conv3_slab(s_ref, base + s, w[0], b[0], r0, BLOCK)
                z = (_conv3_slab(s_ref, base + ns + s, w[1], b[1], r0, BLOCK)
                     * _conv3_slab(s_ref, base + 2 * ns + s, w[2], b[2], r0, BLOCK))
                zt_ref[n, col:col + LANES, :] = z.T


def _hy_in(x, gain, win_r, cw_r, cb_r, T):
    B, L, D = x.shape
    nchunk = win_r.shape[0]
    kern = functools.partial(_hy_in_kernel, T=T, nchunk=nchunk)
    return pl.pallas_call(
        kern,
        grid=(B, L // T),
        in_specs=_halo_specs(T, L, D) + [
            _const_spec((1, D)), _const_spec(win_r.shape), _const_spec(cw_r.shape),
            _const_spec(cb_r.shape)],
        out_specs=[pl.BlockSpec((None, T, D), lambda b, i: (b, i, 0)),
                   pl.BlockSpec((None, T // BLOCK, D, LANES), lambda b, i: (b, i, 0, 0))],
        out_shape=[jax.ShapeDtypeStruct(x.shape, F32),
                   jax.ShapeDtypeStruct((B, L // BLOCK, D, LANES), F32)],
        scratch_shapes=[pltpu.VMEM((T + 2 * HALO, D), BF16),
                        pltpu.VMEM((2 * 3 * HY_CHUNK // LANES, T + 2 * HALO, LANES), F32)],
        compiler_params=_params(2),
        name="hyena_in",
    )(x, x, x, gain, win_r, cw_r, cb_r)


def _hy_out_kernel(x_ref, x0_ref, yt_ref, w_ref, o_ref, m_ref, *, T):
    for n in range(T // BLOCK):
        rows = slice(n * BLOCK, (n + 1) * BLOCK)
        m_ref[rows, :] = (x0_ref[rows, :] * yt_ref[n].T).astype(BF16)
    o_ref[...] = x_ref[...] + jnp.dot(m_ref[...], w_ref[...], preferred_element_type=F32)


def _hy_out(x, x0, yt, w, T):
    B, L, D = x.shape
    spec = pl.BlockSpec((None, T, D), lambda b, i: (b, i, 0))
    kern = functools.partial(_hy_out_kernel, T=T)
    return pl.pallas_call(
        kern,
        grid=(B, L // T),
        in_specs=[spec, spec, pl.BlockSpec((None, T // BLOCK, D, LANES), lambda b, i: (b, i, 0, 0)),
                  _const_spec(w.shape)],
        out_specs=spec,
        out_shape=jax.ShapeDtypeStruct(x.shape, F32),
        scratch_shapes=[pltpu.VMEM((T, D), BF16)],
        compiler_params=_params(2),
        name="hyena_out",
    )(x, x0, yt, w)


def _dot_f32(a, b):
    return jnp.dot(a, b, preferred_element_type=F32, precision=lax.Precision.HIGHEST)


def _filter_kernel(frl_ref, ph_ref, w1_ref, b1_ref, f1_ref, w2_ref, b2_ref, f2_ref, w3_ref, b3_ref,
                   f3_ref, wout_ref, delta_ref, hf_ref, hb_ref, *, L, R):
    hr = R // 2
    ja = pl.program_id(0) * R + lax.broadcasted_iota(jnp.int32, (hr, 1), 0)
    jb = ja + hr
    lane = lax.broadcasted_iota(jnp.int32, (hr, LANES), 1)
    j = jnp.where(lane < FILTER_WIDTH, ja, jb).astype(F32)
    t = j / float(L - 1)
    w = (2.0 * math.pi) * j / float(L)
    feats = jnp.where(lane % FILTER_WIDTH == 0, t, jnp.cos(w * frl_ref[...] + ph_ref[...]))
    h = jnp.sin(f1_ref[...] * (_dot_f32(feats, w1_ref[...]) + b1_ref[...]))
    h = jnp.sin(f2_ref[...] * (_dot_f32(h, w2_ref[...]) + b2_ref[...]))
    h = jnp.sin(f3_ref[...] * (_dot_f32(h, w3_ref[...]) + b3_ref[...]))
    h = _dot_f32(h, wout_ref[...])
    for half, jidx in enumerate((ja, jb)):
        decay = jnp.exp(-(jidx.astype(F32) / float(L - 1)) * delta_ref[...])
        base = half * 2 * D_MODEL
        hf = h[:, base:base + D_MODEL] * decay
        hb = jnp.where(jidx == 0, 0.0, h[:, base + D_MODEL:base + 2 * D_MODEL] * decay)
        for n in range(hr // BLOCK):
            blk = half * (hr // BLOCK) + n
            hf_ref[blk] = hf[n * BLOCK:(n + 1) * BLOCK, :].T
            hb_ref[blk] = hb[n * BLOCK:(n + 1) * BLOCK, :].T


def _filter(L, frl, ph, w1, b1, f1, w2, b2, f2, w3, b3, f3, wout, delta):
    R = 512
    kern = functools.partial(_filter_kernel, L=L, R=R)
    consts = [frl, ph, w1, b1, f1, w2, b2, f2, w3, b3, f3, wout, delta]
    out_spec = pl.BlockSpec((R // BLOCK, D_MODEL, LANES), lambda i: (i, 0, 0))
    shape = jax.ShapeDtypeStruct((L // BLOCK, D_MODEL, LANES), F32)
    return pl.pallas_call(
        kern,
        grid=(L // R,),
        in_specs=[_const_spec(a.shape) for a in consts],
        out_specs=[out_spec, out_spec],
        out_shape=[shape, shape],
        compiler_params=_params(1),
        name="hyena_filter",
    )(*consts)


def _dft_consts(L):
    N = 2 * L
    N2 = FFT_N2
    N1 = N // N2
    h = N1 // 2
    a1 = -2.0 * np.pi * np.outer(np.arange(N1), np.arange(N1)) / N1
    f1r, f1i = np.cos(a1), np.sin(a1)
    a2 = -2.0 * np.pi * np.outer(np.arange(N2), np.arange(N2)) / N2
    f2r, f2i = np.cos(a2), np.sin(a2)
    at = -2.0 * np.pi * np.outer(np.arange(N1), np.arange(N2)) / N
    m1 = np.block([[f1r[:, :h], -f1i[:, :h]], [f1i[:, :h], f1r[:, :h]]])
    m1f = np.concatenate([f1r[:, :h], f1i[:, :h]], axis=0)
    m3 = np.block([[f1r[:h], f1i[:h]], [-f1i[:h], f1r[:h]]])
    c = lambda a: jnp.asarray(a, dtype=BF16)
    f = lambda a: jnp.asarray(a, dtype=F32)
    return dict(N1=N1, m1=c(m1), m1f=c(m1f), m3=c(m3),
                r2f=c(np.block([[f2r, f2i], [-f2i, f2r]])),
                r2i=c(np.block([[f2r, -f2i], [f2i, f2r]])),
                twr=f(np.cos(at)), twi=f(np.sin(at)))


def _cmul(ar, ai, br, bi):
    return ar * br - ai * bi, ar * bi + ai * br


def _slab8(ref, idx, c0, n):
    v = ref[idx + (slice(None), slice(c0, c0 + SUBLANES), slice(None))]
    return v.reshape(n * SUBLANES, LANES)


def _chan_rows(s_ref, base, c, n):
    return s_ref[pl.ds(base + c, n, stride=SUBLANES), :]


def _fft_kernel(x_ref, kf_ref, m1_ref, m3_ref, twr_ref, twi_ref, r2f_ref, r2i_ref, o_ref,
                *scratch, N1, nc):
    h = N1 // 2
    h8 = h * SUBLANES
    nsub = nc // SUBLANES
    twr, twi = twr_ref[...], twi_ref[...]
    cols = []
    for s8 in range(nsub):
        sin_ref = scratch[s8]
        sin_ref[0:h8] = _slab8(x_ref, (0,), s8 * SUBLANES, h)
        sin_ref[h8:2 * h8] = _slab8(x_ref, (1,), s8 * SUBLANES, h)
        cols += [jnp.concatenate([_chan_rows(sin_ref, 0, c, h), _chan_rows(sin_ref, h8, c, h)], axis=0)
                 for c in range(SUBLANES)]
    a = jnp.dot(m1_ref[...], jnp.concatenate(cols, axis=1).astype(BF16),
                preferred_element_type=F32)
    lhs = []
    for c in range(nc):
        tr, ti = _cmul(a[:N1, c * LANES:(c + 1) * LANES], a[N1:, c * LANES:(c + 1) * LANES], twr, twi)
        lhs.append(jnp.concatenate([tr, ti], axis=1))
    lhs = jnp.concatenate(lhs, axis=0).astype(BF16)
    X = jnp.dot(lhs, r2f_ref[...], preferred_element_type=F32)
    yr, yi = _cmul(X[:, :LANES], X[:, LANES:],
                   kf_ref[0].reshape(nc * N1, LANES), kf_ref[1].reshape(nc * N1, LANES))
    Bm = jnp.dot(jnp.concatenate([yr, yi], axis=1).astype(BF16), r2i_ref[...],
                 preferred_element_type=F32)
    re_cols, im_cols = [], []
    for c in range(nc):
        br, bi = Bm[c * N1:(c + 1) * N1, :LANES], Bm[c * N1:(c + 1) * N1, LANES:]
        re_cols.append(br * twr + bi * twi)
        im_cols.append(bi * twr - br * twi)
    rhs = jnp.concatenate([jnp.concatenate(re_cols, axis=1),
                           jnp.concatenate(im_cols, axis=1)], axis=0).astype(BF16)
    y = jnp.dot(m3_ref[...], rhs, preferred_element_type=F32)
    for s8 in range(nsub):
        c0 = s8 * SUBLANES
        sout_ref = scratch[nsub + s8]
        for c in range(SUBLANES):
            lo = (c0 + c) * LANES
            sout_ref[pl.ds(c, h, stride=SUBLANES), :] = y[:h, lo:lo + LANES]
            sout_ref[pl.ds(h8 + c, h, stride=SUBLANES), :] = y[h:, lo:lo + LANES]
        o_ref[0, :, c0:c0 + SUBLANES, :] = sout_ref[0:h8].reshape(h, SUBLANES, LANES)
        o_ref[1, :, c0:c0 + SUBLANES, :] = sout_ref[h8:2 * h8].reshape(h, SUBLANES, LANES)


def _fftconv(zt, kf, c):
    B, h, C, _ = zt.shape
    N1 = c["N1"]
    nc = FFT_CH
    kern = functools.partial(_fft_kernel, N1=N1, nc=nc)
    blk = pl.BlockSpec((2, h, nc, LANES), lambda cb, p: (p, 0, cb, 0))
    consts = [c["m1"], c["m3"], c["twr"], c["twi"], c["r2f"], c["r2i"]]
    return pl.pallas_call(
        kern,
        grid=(C // nc, B // 2),
        in_specs=[blk, pl.BlockSpec((2, nc, N1, LANES), lambda cb, p: (0, cb, 0, 0))]
        + [_const_spec(a.shape) for a in consts],
        out_specs=blk,
        out_shape=jax.ShapeDtypeStruct(zt.shape, F32),
        scratch_shapes=[pltpu.VMEM((N1 * SUBLANES, LANES), F32)] * (2 * nc // SUBLANES),
        compiler_params=_params(2),
        name="fft_conv",
    )(zt, kf, *consts)


def _fspec_kernel(hf_ref, hb_ref, d_ref, m1f_ref, twr_ref, twi_ref, r2f_ref, o_ref,
                  *scratch, N1, nc, scale):
    h = N1 // 2
    h8 = h * SUBLANES
    twr, twi = twr_ref[...], twi_ref[...]
    cols = []
    for s8 in range(nc // SUBLANES):
        s_ref = scratch[s8]
        s_ref[0:h8] = _slab8(hf_ref, (), s8 * SUBLANES, h)
        s_ref[h8:2 * h8] = _slab8(hb_ref, (), s8 * SUBLANES, h)
        for c in range(SUBLANES):
            cols += [_chan_rows(s_ref, 0, c, h), _chan_rows(s_ref, h8, c, h)]
    a = jnp.dot(m1f_ref[...], jnp.concatenate(cols, axis=1).astype(BF16),
                preferred_element_type=F32)
    lhs = []
    for k in range(2 * nc):
        tr, ti = _cmul(a[:N1, k * LANES:(k + 1) * LANES], a[N1:, k * LANES:(k + 1) * LANES], twr, twi)
        lhs.append(jnp.concatenate([tr, ti], axis=1))
    lhs = jnp.concatenate(lhs, axis=0).astype(BF16)
    X = jnp.dot(lhs, r2f_ref[...], preferred_element_type=F32)
    for c in range(nc):
        xf = X[(2 * c) * N1:(2 * c + 1) * N1]
        xb = X[(2 * c + 1) * N1:(2 * c + 2) * N1]
        d = d_ref[c:c + 1, :]
        o_ref[0, c] = (xf[:, :LANES] + xb[:, :LANES] + d) * scale
        o_ref[1, c] = (xf[:, LANES:] - xb[:, LANES:]) * scale


def _filter_spectrum(hf, hb, d, c):
    h, C, _ = hf.shape
    N1 = c["N1"]
    nc = FFT_CH
    kern = functools.partial(_fspec_kernel, N1=N1, nc=nc, scale=1.0 / (N1 * FFT_N2))
    blk = pl.BlockSpec((h, nc, LANES), lambda cb: (0, cb, 0))
    consts = [c["m1f"], c["twr"], c["twi"], c["r2f"]]
    return pl.pallas_call(
        kern,
        grid=(C // nc,),
        in_specs=[blk, blk, pl.BlockSpec((nc, 1), lambda cb: (cb, 0))]
        + [_const_spec(a.shape) for a in consts],
        out_specs=pl.BlockSpec((2, nc, N1, LANES), lambda cb: (0, cb, 0, 0)),
        out_shape=jax.ShapeDtypeStruct((2, C, N1, LANES), F32),
        scratch_shapes=[pltpu.VMEM((N1 * SUBLANES, LANES), F32)] * (nc // SUBLANES),
        compiler_params=_params(1),
        name="filter_spectrum",
    )(hf, hb, d, *consts)


def _t5_bucket(rel):
    half = N_BUCKETS // 2
    max_exact = half // 2
    ret = jnp.where(rel > 0, half, 0)
    n = jnp.abs(rel)
    nf = jnp.maximum(n, 1).astype(jnp.float32)
    large = max_exact + (jnp.log(nf / max_exact) / math.log(MAX_DISTANCE / max_exact)
                         * (half - max_exact)).astype(jnp.int32)
    large = jnp.minimum(large, half - 1)
    return ret + jnp.where(n < max_exact, n, large)


def _bias_kernel(rb_ref, bucket_ref, rel_ref, o_ref):
    bucket = bucket_ref[...]
    inside = jnp.abs(rel_ref[...]) <= WINDOW
    for h in range(N_HEADS_A):
        acc = jnp.zeros(bucket.shape, F32)
        for b in range(N_BUCKETS):
            acc = jnp.where(bucket == b, rb_ref[b, h], acc)
        o_ref[h] = jnp.where(inside, acc, NEG)


def _bias_table(rel_bias):
    rel = (jnp.arange(3 * BLOCK)[:, None] - BLOCK) - jnp.arange(BLOCK)[None, :]
    rel = rel.astype(jnp.int32)
    bucket = _t5_bucket(rel).astype(jnp.int32)
    return pl.pallas_call(
        _bias_kernel,
        in_specs=[pl.BlockSpec(memory_space=pltpu.SMEM),
                  pl.BlockSpec(memory_space=pltpu.VMEM), pl.BlockSpec(memory_space=pltpu.VMEM)],
        out_specs=pl.BlockSpec(memory_space=pltpu.VMEM),
        out_shape=jax.ShapeDtypeStruct((N_HEADS_A, 3 * BLOCK, BLOCK), F32),
        name="rel_bias_table",
    )(rel_bias.astype(F32), bucket, rel)


def _group_ms(v, gmat):
    return jnp.dot((v * v).astype(BF16), gmat, preferred_element_type=F32)


def _ab_kernel(sink_ref, x_ref, xp_ref, xn_ref, g_ref, win_ref, wout_ref, gm_ref, qg_ref,
               kg_ref, bias_ref, vg_ref, ws_ref, bs_ref, o_ref, *, T):
    i = pl.program_id(1)
    nb = T // BLOCK
    nblocks = pl.num_programs(1) * nb
    gain = g_ref[...]
    x = x_ref[...]
    proj = jnp.dot(_rms(x, gain).astype(BF16), win_ref[...], preferred_element_type=F32)
    xh = jnp.concatenate([_rms(xp_ref[...], gain), _rms(xn_ref[...], gain)], axis=0)
    kvh = jnp.dot(xh.astype(BF16), win_ref[:, ATTN_W:ATTN_W + 2 * BLOCK],
                  preferred_element_type=F32)
    gm = gm_ref[...]
    gm_k = gm[:BLOCK, :BLOCK]

    q = proj[:, :ATTN_W]
    qn = q * lax.rsqrt(_group_ms(q, gm) + EPS) * qg_ref[...]
    k_all = jnp.concatenate([kvh[:BLOCK, :BLOCK], proj[:, ATTN_W:ATTN_W + BLOCK],
                             kvh[BLOCK:, :BLOCK]], axis=0)
    v_all = jnp.concatenate([kvh[:BLOCK, BLOCK:], proj[:, ATTN_W + BLOCK:ATTN_W + 2 * BLOCK],
                             kvh[BLOCK:, BLOCK:]], axis=0)
    kn = k_all * lax.rsqrt(_group_ms(k_all, gm_k) + EPS) * kg_ref[...]

    qt = qn.T.astype(BF16)
    vt = v_all.T.astype(BF16)
    knb = kn.astype(BF16)
    zeros_q = jnp.zeros((HEAD_DIM, GQA * BLOCK), BF16)
    out_cols = []
    for n in range(nb):
        blk = i * nb + n
        neg_prev = jnp.where(blk == 0, NEG, 0.0)
        neg_next = jnp.where(blk == nblocks - 1, NEG, 0.0)
        kb = knb[n * BLOCK:n * BLOCK + 3 * BLOCK, :]
        vtb = vt[:, n * BLOCK:n * BLOCK + 3 * BLOCK]
        head_rows = []
        for hk in range(N_KV_A):
            qh = jnp.concatenate(
                [qt[(hk * GQA + g) * HEAD_DIM:(hk * GQA + g + 1) * HEAD_DIM, n * BLOCK:(n + 1) * BLOCK]
                 for g in range(GQA)], axis=1)
            qz = jnp.concatenate([qh, zeros_q] if hk == 0 else [zeros_q, qh], axis=0)
            s = jnp.dot(kb, qz, preferred_element_type=F32)
            s = s + jnp.concatenate([bias_ref[hk * GQA + g] for g in range(GQA)], axis=1)
            s = jnp.concatenate([s[:BLOCK] + neg_prev, s[BLOCK:2 * BLOCK], s[2 * BLOCK:] + neg_next],
                                axis=0)
            sk = jnp.concatenate([jnp.full((1, BLOCK), sink_ref[hk * GQA + g], F32)
                                  for g in range(GQA)], axis=1)
            m = jnp.maximum(jnp.max(s, axis=0, keepdims=True), sk)
            p = jnp.exp(s - m)
            den = jnp.sum(p, axis=0, keepdims=True) + jnp.exp(sk - m)
            o = jnp.dot(vtb[hk * HEAD_DIM:(hk + 1) * HEAD_DIM], p.astype(BF16),
                        preferred_element_type=F32) / den
            head_rows += [o[:, g * BLOCK:(g + 1) * BLOCK] for g in range(GQA)]
        out_cols.append(jnp.concatenate(head_rows, axis=0))
    attn = jnp.concatenate(out_cols, axis=1).T

    su = _gelu(proj[:, ATTN_W + 2 * BLOCK:ATTN_W + 2 * BLOCK + SGU_W])
    sv = _gelu(proj[:, ATTN_W + 2 * BLOCK + SGU_W:])
    svn = (sv * lax.rsqrt(_group_ms(sv, gm) + EPS) * vg_ref[...]).astype(BF16)
    low = lax.broadcasted_iota(jnp.int32, (1, BLOCK * nb), 1) % BLOCK < CH_B
    slabs = []
    for j in range(SGU_W // BLOCK):
        rhs = jnp.concatenate([svn[n * BLOCK:(n + 1) * BLOCK, j * BLOCK:(j + 1) * BLOCK]
                               for n in range(nb)], axis=1)
        a = jnp.dot(ws_ref[2 * j], rhs, preferred_element_type=F32)
        b = jnp.dot(ws_ref[2 * j + 1], rhs, preferred_element_type=F32)
        slabs.append(jnp.where(low, a, b))
    mixed = jnp.concatenate(
        [jnp.concatenate([slabs[j][:, n * BLOCK:(n + 1) * BLOCK] for j in range(SGU_W // BLOCK)],
                         axis=1) + bs_ref[...] for n in range(nb)], axis=0)
    sgu = su * mixed

    cat = jnp.concatenate([attn, sgu], axis=1).astype(BF16)
    o_ref[...] = x + jnp.dot(cat, wout_ref[...], preferred_element_type=F32)


def _ab_layer(x, sink, gain, win, wout, gm, qg, kg, bias_tab, vg, ws, bs, T):
    B, L, D = x.shape
    hb = T // BLOCK
    last = L // BLOCK - 1
    kern = functools.partial(_ab_kernel, T=T)
    consts = [gain, win, wout, gm, qg, kg, bias_tab, vg, ws, bs]
    return pl.pallas_call(
        kern,
        grid=(B, L // T),
        in_specs=[pl.BlockSpec(memory_space=pltpu.SMEM),
                  pl.BlockSpec((None, T, D), lambda b, i: (b, i, 0)),
                  pl.BlockSpec((None, BLOCK, D), lambda b, i: (b, jnp.maximum(i * hb - 1, 0), 0)),
                  pl.BlockSpec((None, BLOCK, D), lambda b, i: (b, jnp.minimum((i + 1) * hb, last), 0)),
                  ] + [_const_spec(a.shape) for a in consts],
        out_specs=pl.BlockSpec((None, T, D), lambda b, i: (b, i, 0)),
        out_shape=jax.ShapeDtypeStruct(x.shape, F32),
        compiler_params=_params(2),
        name="attn_sgu_mixer",
    )(sink, x, x, x, *consts)


def _chunk_cols(w, parts, chunk):
    lead = w.shape[:-1]
    n = w.shape[-1] // (parts * chunk)
    w = w.reshape(lead + (parts, n, chunk))
    w = jnp.moveaxis(w, -2, 0)
    return w.reshape((n,) + lead + (parts * chunk,))


def _prepare(p):
    depth = p["mix_norm"].shape[0]
    q = dict(depth=depth)
    q["mix_norm"] = [p["mix_norm"][l][None, :] for l in range(depth)]
    q["ffn_norm"] = [p["ffn_norm"][l][None, :] for l in range(depth)]
    q["ffn_wup"] = [_chunk_cols(p["ffn_w_up"][l], 2, FFN_CHUNK).astype(BF16) for l in range(depth)]
    q["ffn_cw"] = [_chunk_cols(p["ffn_conv_w"][l], 2, FFN_CHUNK) for l in range(depth)]
    q["ffn_cb"] = [_chunk_cols(p["ffn_conv_b"][l][None, :], 2, FFN_CHUNK) for l in range(depth)]
    q["ffn_wdn"] = [p["ffn_w_down"][l].astype(BF16) for l in range(depth)]

    n_even = p["ab_w_in"].shape[0]
    q["ab_win"] = [p["ab_w_in"][i].astype(BF16) for i in range(n_even)]
    q["ab_wout"] = [p["ab_w_out"][i].astype(BF16) for i in range(n_even)]
    q["qg"] = [jnp.tile(p["q_norm"][i], N_HEADS_A)[None, :] * (HEAD_DIM ** -0.5) for i in range(n_even)]
    q["kg"] = [jnp.tile(p["k_norm"][i], N_KV_A)[None, :] for i in range(n_even)]
    q["sink"] = [p["attn_sink"][i].astype(F32) for i in range(n_even)]
    q["vg"] = [p["sgu_v_norm"][i].reshape(1, SGU_W) for i in range(n_even)]
    q["ws"] = [p["sgu_w"][i].astype(BF16) for i in range(n_even)]
    q["bs"] = [jnp.repeat(p["sgu_b"][i].T, CH_B, axis=1) for i in range(n_even)]
    grp = np.arange(ATTN_W) // HEAD_DIM
    q["gm"] = jnp.asarray((grp[:, None] == grp[None, :]) / float(HEAD_DIM), dtype=BF16)

    n_odd = p["hy_w_in"].shape[0]
    q["hy_win"] = [_chunk_cols(p["hy_w_in"][i], 3, HY_CHUNK).astype(BF16) for i in range(n_odd)]
    q["hy_cw"] = [_chunk_cols(p["hy_conv_w"][i], 3, HY_CHUNK) for i in range(n_odd)]
    q["hy_cb"] = [_chunk_cols(p["hy_conv_b"][i][None, :], 3, HY_CHUNK) for i in range(n_odd)]
    q["hy_d"] = [p["hy_d"][i][:, None].astype(F32) for i in range(n_odd)]
    q["hy_wout"] = [p["hy_w_out"][i].astype(BF16) for i in range(n_odd)]
    slot = np.arange(LANES) % FILTER_WIDTH
    fr = np.linspace(1e-4, FILTER_BANDS - 1, FILTER_BANDS).astype(np.float32)
    frl = np.where((slot >= 1) & (slot <= 2 * FILTER_BANDS), fr[(slot - 1) % FILTER_BANDS], 0.0)
    ph = np.where((slot > FILTER_BANDS) & (slot <= 2 * FILTER_BANDS), 0.5 * np.pi, 0.0)
    q["frl"] = jnp.asarray(frl[None, :], dtype=F32)
    q["ph"] = jnp.asarray(ph[None, :], dtype=F32)
    q["delta"] = jnp.abs(jnp.linspace(MIN_DECAY, MAX_DECAY, D_MODEL, dtype=F32))[None, :]

    def twice(w):
        z = jnp.zeros_like(w)
        return jnp.concatenate([jnp.concatenate([w, z], axis=1), jnp.concatenate([z, w], axis=1)], axis=0)

    filt = []
    for i in range(n_odd):
        w1p = jnp.zeros((FILTER_WIDTH, FILTER_WIDTH), F32).at[:FILTER_EMB].set(p["hy_f_w1"][i])
        row = lambda a: jnp.tile(a[None, :].astype(F32), (1, 2))
        filt.append((twice(w1p), row(p["hy_f_b1"][i]), row(p["hy_f_freq1"][i]),
                     twice(p["hy_f_w2"][i]), row(p["hy_f_b2"][i]), row(p["hy_f_freq2"][i]),
                     twice(p["hy_f_w3"][i]), row(p["hy_f_b3"][i]), row(p["hy_f_freq3"][i]),
                     twice(p["hy_f_wout"][i])))
    q["filt"] = filt
    q["bias_tab"] = _bias_table(p["rel_bias"])
    return q


def _hyena_layer(x, q, l, consts, T):
    i = l // 2
    L = x.shape[1]
    hf, hb = _filter(L, q["frl"], q["ph"], *q["filt"][i], q["delta"])
    kf = _filter_spectrum(hf, hb, q["hy_d"][i], consts)
    x0, zt = _hy_in(x, q["mix_norm"][l], q["hy_win"][i], q["hy_cw"][i], q["hy_cb"][i], T)
    yt = _fftconv(zt, kf, consts)
    return _hy_out(x, x0, yt, q["hy_wout"][i], T)


def _trunk(x, q, T=512):
    B, L, D = x.shape
    consts = _dft_consts(L)
    for l in range(q["depth"]):
        i = l // 2
        if l % 2 == 0:
            x = _ab_layer(x, q["sink"][i], q["mix_norm"][l], q["ab_win"][i], q["ab_wout"][i],
                          q["gm"], q["qg"][i], q["kg"][i], q["bias_tab"], q["vg"][i], q["ws"][i],
                          q["bs"][i], T)
        else:
            x = _hyena_layer(x, q, l, consts, T)
        x = _ffn(x, q["ffn_norm"][l], q["ffn_wup"][l], q["ffn_cw"][l], q["ffn_cb"][l],
                 q["ffn_wdn"][l], T)
    return x


def kernel(x_prompt, x_sample, rel_bias, mix_norm, ffn_norm, ab_w_in, q_norm, k_norm, attn_sink, sgu_v_norm, sgu_w, sgu_b, ab_w_out, hy_w_in, hy_conv_w, hy_conv_b, hy_f_w1, hy_f_b1, hy_f_freq1, hy_f_w2, hy_f_b2, hy_f_freq2, hy_f_w3, hy_f_b3, hy_f_freq3, hy_f_wout, hy_d, hy_w_out, ffn_w_up, ffn_conv_w, ffn_conv_b, ffn_w_down):
    p = dict(rel_bias=rel_bias, mix_norm=mix_norm, ffn_norm=ffn_norm, ab_w_in=ab_w_in,
             q_norm=q_norm, k_norm=k_norm, attn_sink=attn_sink, sgu_v_norm=sgu_v_norm,
             sgu_w=sgu_w, sgu_b=sgu_b, ab_w_out=ab_w_out, hy_w_in=hy_w_in, hy_conv_w=hy_conv_w,
             hy_conv_b=hy_conv_b, hy_f_w1=hy_f_w1, hy_f_b1=hy_f_b1, hy_f_freq1=hy_f_freq1,
             hy_f_w2=hy_f_w2, hy_f_b2=hy_f_b2, hy_f_freq2=hy_f_freq2, hy_f_w3=hy_f_w3,
             hy_f_b3=hy_f_b3, hy_f_freq3=hy_f_freq3, hy_f_wout=hy_f_wout, hy_d=hy_d,
             hy_w_out=hy_w_out, ffn_w_up=ffn_w_up, ffn_conv_w=ffn_conv_w, ffn_conv_b=ffn_conv_b,
             ffn_w_down=ffn_w_down)
    q = _prepare(p)
    return (_trunk(x_prompt, q), _trunk(x_sample, q))
```

```python
import functools
import math

import numpy as np
import jax
import jax.numpy as jnp
from jax import lax
from jax.experimental import pallas as pl
from jax.experimental.pallas import tpu as pltpu

F32 = jnp.float32
BF16 = jnp.bfloat16

D_MODEL = 1024
HEAD_DIM = 64
N_HEADS_A = 8
N_KV_A = 2
GQA = 4
ATTN_W = 512
WINDOW = 128
BLOCK = 128
N_BUCKETS = 32
MAX_DISTANCE = 128
SGU_W = 512
N_GROUPS_B = 8
CH_B = 64
FILTER_EMB = 33
FILTER_BANDS = 16
FILTER_WIDTH = 64
DECAY_TARGET = 1e-2
MIN_DECAY = math.log(DECAY_TARGET) / 1.5
MAX_DECAY = math.log(DECAY_TARGET) / 0.3
FFN_HIDDEN = 2816
EPS = 1e-6
NEG = -1e30

LANES = 128
SUBLANES = 8
FFN_CHUNK = 256
HY_CHUNK = 256
HALO = 16
ROW_BLOCK = 64
FFT_N2 = LANES
FFT_ROWS = 4096
VMEM_LIMIT = 56 * 1024 * 1024


def _gelu(x):
    t = jnp.tanh(x * (0.7978845608028654 + (0.7978845608028654 * 0.044715) * (x * x)))
    hx = 0.5 * x
    return hx + hx * t


def _rms(x, gain):
    return x * lax.rsqrt(jnp.mean(x * x, axis=-1, keepdims=True) + EPS) * gain


def _const_spec(shape):
    nd = len(shape)
    return pl.BlockSpec(shape, lambda *_: (0,) * nd, pipeline_mode=pl.Buffered(1))


def _params(n_axes):
    return pltpu.CompilerParams(dimension_semantics=("arbitrary",) * n_axes,
                                vmem_limit_bytes=VMEM_LIMIT)


def _park_slabs(s_ref, base, h):
    for s in range(h.shape[1] // LANES):
        s_ref[base + s] = h[:, s * LANES:(s + 1) * LANES]


def _conv3_slab(s_ref, slab, w, b, r0, rows):
    lo = HALO + r0
    return (s_ref[slab, lo - 1:lo - 1 + rows, :] * w[0:1]
            + s_ref[slab, lo:lo + rows, :] * w[1:2]
            + s_ref[slab, lo + 1:lo + 1 + rows, :] * w[2:3]
            + b)


def _fill_normed_ext(xe_ref, x, xp, xn, gain, rows, first, last):
    pm = jnp.where(first, 0.0, 1.0)
    nm = jnp.where(last, 0.0, 1.0)
    xe_ref[0:HALO, :] = (_rms(xp, gain) * pm).astype(BF16)
    xe_ref[HALO:HALO + rows, :] = _rms(x, gain).astype(BF16)
    xe_ref[HALO + rows:HALO + rows + HALO, :] = (_rms(xn, gain) * nm).astype(BF16)


def _halo_specs(T, L, D):
    hb = T // HALO
    last = L // HALO - 1
    return [
        pl.BlockSpec((None, T, D), lambda b, i: (b, i, 0)),
        pl.BlockSpec((None, HALO, D), lambda b, i: (b, jnp.maximum(i * hb - 1, 0), 0)),
        pl.BlockSpec((None, HALO, D), lambda b, i: (b, jnp.minimum((i + 1) * hb, last), 0)),
    ]


def _ffn_body(x, xp, xn, g_ref, wup_ref, cw_ref, cb_ref, wdn_ref, o_ref, xe_ref, a_ref, s_ref,
              T, nchunk):
    i = pl.program_id(1)
    _fill_normed_ext(xe_ref, x, xp, xn, g_ref[...], T, i == 0, i == pl.num_programs(1) - 1)
    xe = xe_ref[...]
    ns = FFN_CHUNK // LANES
    for j in range(nchunk):
        base = (j % 2) * 2 * ns
        _park_slabs(s_ref, base, jnp.dot(xe, wup_ref[j], preferred_element_type=F32))
        cw, cb = cw_ref[j], cb_ref[j]
        for s in range(ns):
            gl, ul = s * LANES, (ns + s) * LANES
            wg, bg = cw[:, gl:gl + LANES], cb[:, gl:gl + LANES]
            wu, bu = cw[:, ul:ul + LANES], cb[:, ul:ul + LANES]
            col = j * FFN_CHUNK + s * LANES
            for r0 in range(0, T, ROW_BLOCK):
                g = _conv3_slab(s_ref, base + s, wg, bg, r0, ROW_BLOCK)
                u = _conv3_slab(s_ref, base + ns + s, wu, bu, r0, ROW_BLOCK)
                a_ref[r0:r0 + ROW_BLOCK, col:col + LANES] = (_gelu(g) * u).astype(BF16)
    o_ref[...] = x + jnp.dot(a_ref[...], wdn_ref[...], preferred_element_type=F32)


def _ffn_kernel(x_ref, xp_ref, xn_ref, g_ref, wup_ref, cw_ref, cb_ref, wdn_ref, o_ref,
                xe_ref, a_ref, s_ref, *, T, nchunk):
    _ffn_body(x_ref[...], xp_ref[...], xn_ref[...], g_ref, wup_ref, cw_ref, cb_ref, wdn_ref,
              o_ref, xe_ref, a_ref, s_ref, T, nchunk)


def _ffn_scratch(T, D):
    return [pltpu.VMEM((T + 2 * HALO, D), BF16), pltpu.VMEM((T, FFN_HIDDEN), BF16),
            pltpu.VMEM((2 * 2 * FFN_CHUNK // LANES, T + 2 * HALO, LANES), F32)]


def _hy_ffn_kernel(x_ref, xp_ref, xn_ref, x0_ref, x0p_ref, x0n_ref, yt_ref, ytp_ref, ytn_ref,
                   wo_ref, g_ref, wup_ref, cw_ref, cb_ref, wdn_ref, o_ref,
                   xe_ref, a_ref, s_ref, m_ref, *, T, nchunk):
    m_ref[0:HALO, :] = (x0p_ref[...] * ytp_ref[...].T[BLOCK - HALO:, :]).astype(BF16)
    for n in range(T // BLOCK):
        lo = HALO + n * BLOCK
        m_ref[lo:lo + BLOCK, :] = (x0_ref[n * BLOCK:(n + 1) * BLOCK, :] * yt_ref[n].T).astype(BF16)
    m_ref[HALO + T:, :] = (x0n_ref[...] * ytn_ref[...].T[:HALO, :]).astype(BF16)
    mix = jnp.dot(m_ref[...], wo_ref[...], preferred_element_type=F32)
    _ffn_body(x_ref[...] + mix[HALO:HALO + T], xp_ref[...] + mix[:HALO], xn_ref[...] + mix[HALO + T:],
              g_ref, wup_ref, cw_ref, cb_ref, wdn_ref, o_ref, xe_ref, a_ref, s_ref, T, nchunk)


def _hy_ffn(x, x0, yt, wo, gain, wup_r, cw_r, cb_r, wdn, T):
    B, L, D = x.shape
    nchunk = wup_r.shape[0]
    nb = T // BLOCK
    last = L // BLOCK - 1
    kern = functools.partial(_hy_ffn_kernel, T=T, nchunk=nchunk)
    return pl.pallas_call(
        kern,
        grid=(B, L // T),
        in_specs=_halo_specs(T, L, D) + _halo_specs(T, L, D) + [
            pl.BlockSpec((None, nb, D, LANES), lambda b, i: (b, i, 0, 0)),
            pl.BlockSpec((None, None, D, LANES), lambda b, i: (b, jnp.maximum(i * nb - 1, 0), 0, 0)),
            pl.BlockSpec((None, None, D, LANES), lambda b, i: (b, jnp.minimum((i + 1) * nb, last), 0, 0)),
            _const_spec(wo.shape), _const_spec((1, D)), _const_spec(wup_r.shape),
            _const_spec(cw_r.shape), _const_spec(cb_r.shape), _const_spec(wdn.shape)],
        out_specs=pl.BlockSpec((None, T, D), lambda b, i: (b, i, 0)),
        out_shape=jax.ShapeDtypeStruct(x.shape, F32),
        scratch_shapes=_ffn_scratch(T, D) + [pltpu.VMEM((T + 2 * HALO, D), BF16)],
        compiler_params=_params(2),
        name="hyena_out_conv_ffn",
    )(x, x, x, x0, x0, x0, yt, yt, yt, wo, gain, wup_r, cw_r, cb_r, wdn)


def _ffn(x, gain, wup_r, cw_r, cb_r, wdn, T):
    B, L, D = x.shape
    nchunk = wup_r.shape[0]
    kern = functools.partial(_ffn_kernel, T=T, nchunk=nchunk)
    return pl.pallas_call(
        kern,
        grid=(B, L // T),
        in_specs=_halo_specs(T, L, D) + [
            _const_spec((1, D)), _const_spec(wup_r.shape), _const_spec(cw_r.shape),
            _const_spec(cb_r.shape), _const_spec(wdn.shape)],
        out_specs=pl.BlockSpec((None, T, D), lambda b, i: (b, i, 0)),
        out_shape=jax.ShapeDtypeStruct(x.shape, F32),
        scratch_shapes=_ffn_scratch(T, D),
        compiler_params=_params(2),
        name="conv_ffn",
    )(x, x, x, gain, wup_r, cw_r, cb_r, wdn)


def _hy_in_kernel(x_ref, xp_ref, xn_ref, g_ref, win_ref, cw_ref, cb_ref, x0_ref, zt_ref,
                  xe_ref, s_ref, *, T, nchunk):
    i = pl.program_id(1)
    _fill_normed_ext(xe_ref, x_ref[...], xp_ref[...], xn_ref[...], g_ref[...], T,
                     i == 0, i == pl.num_programs(1) - 1)
    xe = xe_ref[...]
    ns = HY_CHUNK // LANES
    for j in range(nchunk):
        base = (j % 2) * 3 * ns
        _park_slabs(s_ref, base, jnp.dot(xe, win_ref[j], preferred_element_type=F32))
        cw, cb = cw_ref[j], cb_ref[j]
        for s in range(ns):
            col = j * HY_CHUNK + s * LANES
            w = [cw[:, (k * ns + s) * LANES:(k * ns + s + 1) * LANES] for k in range(3)]
            b = [cb[:, (k * ns + s) * LANES:(k * ns + s + 1) * LANES] for k in range(3)]
            for n in range(T // BLOCK):
                r0 = n * BLOCK
                x0_ref[r0:r0 + BLOCK, col:col + LANES] = _conv3_slab(s_ref, base + s, w[0], b[0], r0, BLOCK)
                z = (_conv3_slab(s_ref, base + ns + s, w[1], b[1], r0, BLOCK)
                     * _conv3_slab(s_ref, base + 2 * ns + s, w[2], b[2], r0, BLOCK))
                zt_ref[n, col:col + LANES, :] = z.T


def _hy_in(x, gain, win_r, cw_r, cb_r, T):
    B, L, D = x.shape
    nchunk = win_r.shape[0]
    kern = functools.partial(_hy_in_kernel, T=T, nchunk=nchunk)
    return pl.pallas_call(
        kern,
        grid=(B, L // T),
        in_specs=_halo_specs(T, L, D) + [
            _const_spec((1, D)), _const_spec(win_r.shape), _const_spec(cw_r.shape),
            _const_spec(cb_r.shape)],
        out_specs=[pl.BlockSpec((None, T, D), lambda b, i: (b, i, 0)),
                   pl.BlockSpec((None, T // BLOCK, D, LANES), lambda b, i: (b, i, 0, 0))],
        out_shape=[jax.ShapeDtypeStruct(x.shape, F32),
                   jax.ShapeDtypeStruct((B, L // BLOCK, D, LANES), F32)],
        scratch_shapes=[pltpu.VMEM((T + 2 * HALO, D), BF16),
                        pltpu.VMEM((2 * 3 * HY_CHUNK // LANES, T + 2 * HALO, LANES), F32)],
        compiler_params=_params(2),
        name="hyena_in",
    )(x, x, x, gain, win_r, cw_r, cb_r)


def _dot_f32(a, b):
    return jnp.dot(a, b, preferred_element_type=F32, precision=lax.Precision.HIGHEST)


def _filter_kernel(frl_ref, ph_ref, w1_ref, b1_ref, f1_ref, w2_ref, b2_ref, f2_ref, w3_ref, b3_ref,
                   f3_ref, wout_ref, delta_ref, hf_ref, hb_ref, *, L, R):
    hr = R // 2
    ja = pl.program_id(0) * R + lax.broadcasted_iota(jnp.int32, (hr, 1), 0)
    jb = ja + hr
    lane = lax.broadcasted_iota(jnp.int32, (hr, LANES), 1)
    j = jnp.where(lane < FILTER_WIDTH, ja, jb).astype(F32)
    t = j / float(L - 1)
    w = (2.0 * math.pi) * j / float(L)
    feats = jnp.where(lane % FILTER_WIDTH == 0, t, jnp.cos(w * frl_ref[...] + ph_ref[...]))
    h = jnp.sin(f1_ref[...] * (_dot_f32(feats, w1_ref[...]) + b1_ref[...]))
    h = jnp.sin(f2_ref[...] * (_dot_f32(h, w2_ref[...]) + b2_ref[...]))
    h = jnp.sin(f3_ref[...] * (_dot_f32(h, w3_ref[...]) + b3_ref[...]))
    h = _dot_f32(h, wout_ref[...])
    for half, jidx in enumerate((ja, jb)):
        decay = jnp.exp(-(jidx.astype(F32) / float(L - 1)) * delta_ref[...])
        base = half * 2 * D_MODEL
        hf = h[:, base:base + D_MODEL] * decay
        hb = jnp.where(jidx == 0, 0.0, h[:, base + D_MODEL:base + 2 * D_MODEL] * decay)
        for n in range(hr // BLOCK):
            blk = half * (hr // BLOCK) + n
            hf_ref[blk] = hf[n * BLOCK:(n + 1) * BLOCK, :].T
            hb_ref[blk] = hb[n * BLOCK:(n + 1) * BLOCK, :].T


def _filter(L, frl, ph, w1, b1, f1, w2, b2, f2, w3, b3, f3, wout, delta):
    R = 512
    kern = functools.partial(_filter_kernel, L=L, R=R)
    consts = [frl, ph, w1, b1, f1, w2, b2, f2, w3, b3, f3, wout, delta]
    out_spec = pl.BlockSpec((R // BLOCK, D_MODEL, LANES), lambda i: (i, 0, 0))
    shape = jax.ShapeDtypeStruct((L // BLOCK, D_MODEL, LANES), F32)
    return pl.pallas_call(
        kern,
        grid=(L // R,),
        in_specs=[_const_spec(a.shape) for a in consts],
        out_specs=[out_spec, out_spec],
        out_shape=[shape, shape],
        compiler_params=_params(1),
        name="hyena_filter",
    )(*consts)


def _dft_consts(L):
    N = 2 * L
    N2 = FFT_N2
    N1 = N // N2
    h = N1 // 2
    a1 = -2.0 * np.pi * np.outer(np.arange(N1), np.arange(N1)) / N1
    f1r, f1i = np.cos(a1), np.sin(a1)
    a2 = -2.0 * np.pi * np.outer(np.arange(N2), np.arange(N2)) / N2
    f2r, f2i = np.cos(a2), np.sin(a2)
    at = -2.0 * np.pi * np.outer(np.arange(N1), np.arange(N2)) / N
    m1 = np.block([[f1r[:, :h], -f1i[:, :h]], [f1i[:, :h], f1r[:, :h]]])
    m1f = np.concatenate([f1r[:, :h], f1i[:, :h]], axis=0)
    m3 = np.block([[f1r[:h], f1i[:h]], [-f1i[:h], f1r[:h]]])
    c = lambda a: jnp.asarray(a, dtype=BF16)
    f = lambda a: jnp.asarray(a, dtype=F32)
    return dict(N1=N1, m1=c(m1), m1f=c(m1f), m3=c(m3),
                r2f=c(np.block([[f2r, f2i], [-f2i, f2r]])),
                r2i=c(np.block([[f2r, -f2i], [f2i, f2r]])),
                twr=f(np.cos(at)), twi=f(np.sin(at)))


def _cmul(ar, ai, br, bi):
    return ar * br - ai * bi, ar * bi + ai * br


def _slab8(ref, idx, c0, n):
    v = ref[idx + (slice(None), slice(c0, c0 + SUBLANES), slice(None))]
    return v.reshape(n * SUBLANES, LANES)


def _chan_rows(s_ref, base, c, n):
    return s_ref[pl.ds(base + c, n, stride=SUBLANES), :]


def _fft_kernel(x_ref, kf_ref, m1_ref, m3_ref, twr_ref, twi_ref, r2f_ref, r2i_ref, o_ref,
                *scratch, N1, nc):
    h = N1 // 2
    h8 = h * SUBLANES
    nsub = nc // SUBLANES
    twr, twi = twr_ref[...], twi_ref[...]
    cols = []
    for s8 in range(nsub):
        sin_ref = scratch[s8]
        sin_ref[0:h8] = _slab8(x_ref, (0,), s8 * SUBLANES, h)
        sin_ref[h8:2 * h8] = _slab8(x_ref, (1,), s8 * SUBLANES, h)
        cols += [jnp.concatenate([_chan_rows(sin_ref, 0, c, h), _chan_rows(sin_ref, h8, c, h)], axis=0)
                 for c in range(SUBLANES)]
    a = jnp.dot(m1_ref[...], jnp.concatenate(cols, axis=1).astype(BF16),
                preferred_element_type=F32)
    lhs = []
    for c in range(nc):
        tr, ti = _cmul(a[:N1, c * LANES:(c + 1) * LANES], a[N1:, c * LANES:(c + 1) * LANES], twr, twi)
        lhs.append(jnp.concatenate([tr, ti], axis=1))
    lhs = jnp.concatenate(lhs, axis=0).astype(BF16)
    X = jnp.dot(lhs, r2f_ref[...], preferred_element_type=F32)
    yr, yi = _cmul(X[:, :LANES], X[:, LANES:],
                   kf_ref[0].reshape(nc * N1, LANES), kf_ref[1].reshape(nc * N1, LANES))
    Bm = jnp.dot(jnp.concatenate([yr, yi], axis=1).astype(BF16), r2i_ref[...],
                 preferred_element_type=F32)
    re_cols, im_cols = [], []
    for c in range(nc):
        br, bi = Bm[c * N1:(c + 1) * N1, :LANES], Bm[c * N1:(c + 1) * N1, LANES:]
        re_cols.append(br * twr + bi * twi)
        im_cols.append(bi * twr - br * twi)
    rhs = jnp.concatenate([jnp.concatenate(re_cols, axis=1),
                           jnp.concatenate(im_cols, axis=1)], axis=0).astype(BF16)
    y = jnp.dot(m3_ref[...], rhs, preferred_element_type=F32)
    for s8 in range(nsub):
        c0 = s8 * SUBLANES
        sout_ref = scratch[nsub + s8]
        for c in range(SUBLANES):
            lo = (c0 + c) * LANES
            sout_ref[pl.ds(c, h, stride=SUBLANES), :] = y[:h, lo:lo + LANES]
            sout_ref[pl.ds(h8 + c, h, stride=SUBLANES), :] = y[h:, lo:lo + LANES]
        o_ref[0, :, c0:c0 + SUBLANES, :] = sout_ref[0:h8].reshape(h, SUBLANES, LANES)
        o_ref[1, :, c0:c0 + SUBLANES, :] = sout_ref[h8:2 * h8].reshape(h, SUBLANES, LANES)


def _fftconv(zt, kf, c):
    B, h, C, _ = zt.shape
    N1 = c["N1"]
    nc = min(FFT_ROWS // N1, C)
    kern = functools.partial(_fft_kernel, N1=N1, nc=nc)
    blk = pl.BlockSpec((2, h, nc, LANES), lambda cb, p: (p, 0, cb, 0))
    consts = [c["m1"], c["m3"], c["twr"], c["twi"], c["r2f"], c["r2i"]]
    return pl.pallas_call(
        kern,
        grid=(C // nc, B // 2),
        in_specs=[blk, pl.BlockSpec((2, nc, N1, LANES), lambda cb, p: (0, cb, 0, 0))]
        + [_const_spec(a.shape) for a in consts],
        out_specs=blk,
        out_shape=jax.ShapeDtypeStruct(zt.shape, F32),
        scratch_shapes=[pltpu.VMEM((N1 * SUBLANES, LANES), F32)] * (2 * nc // SUBLANES),
        compiler_params=_params(2),
        name="fft_conv",
    )(zt, kf, *consts)


def _fspec_kernel(hf_ref, hb_ref, d_ref, m1f_ref, twr_ref, twi_ref, r2f_ref, o_ref,
                  *scratch, N1, nc, scale):
    h = N1 // 2
    h8 = h * SUBLANES
    twr, twi = twr_ref[...], twi_ref[...]
    cols = []
    for s8 in range(nc // SUBLANES):
        s_ref = scratch[s8]
        s_ref[0:h8] = _slab8(hf_ref, (), s8 * SUBLANES, h)
        s_ref[h8:2 * h8] = _slab8(hb_ref, (), s8 * SUBLANES, h)
        for c in range(SUBLANES):
            cols += [_chan_rows(s_ref, 0, c, h), _chan_rows(s_ref, h8, c, h)]
    a = jnp.dot(m1f_ref[...], jnp.concatenate(cols, axis=1).astype(BF16),
                preferred_element_type=F32)
    lhs = []
    for k in range(2 * nc):
        tr, ti = _cmul(a[:N1, k * LANES:(k + 1) * LANES], a[N1:, k * LANES:(k + 1) * LANES], twr, twi)
        lhs.append(jnp.concatenate([tr, ti], axis=1))
    lhs = jnp.concatenate(lhs, axis=0).astype(BF16)
    X = jnp.dot(lhs, r2f_ref[...], preferred_element_type=F32)
    for c in range(nc):
        xf = X[(2 * c) * N1:(2 * c + 1) * N1]
        xb = X[(2 * c + 1) * N1:(2 * c + 2) * N1]
        d = d_ref[c:c + 1, :]
        o_ref[0, c] = (xf[:, :LANES] + xb[:, :LANES] + d) * scale
        o_ref[1, c] = (xf[:, LANES:] - xb[:, LANES:]) * scale


def _filter_spectrum(hf, hb, d, c):
    h, C, _ = hf.shape
    N1 = c["N1"]
    nc = min(FFT_ROWS // N1, C)
    kern = functools.partial(_fspec_kernel, N1=N1, nc=nc, scale=1.0 / (N1 * FFT_N2))
    blk = pl.BlockSpec((h, nc, LANES), lambda cb: (0, cb, 0))
    consts = [c["m1f"], c["twr"], c["twi"], c["r2f"]]
    return pl.pallas_call(
        kern,
        grid=(C // nc,),
        in_specs=[blk, blk, pl.BlockSpec((nc, 1), lambda cb: (cb, 0))]
        + [_const_spec(a.shape) for a in consts],
        out_specs=pl.BlockSpec((2, nc, N1, LANES), lambda cb: (0, cb, 0, 0)),
        out_shape=jax.ShapeDtypeStruct((2, C, N1, LANES), F32),
        scratch_shapes=[pltpu.VMEM((N1 * SUBLANES, LANES), F32)] * (nc // SUBLANES),
        compiler_params=_params(1),
        name="filter_spectrum",
    )(hf, hb, d, *consts)


def _t5_bucket(rel):
    half = N_BUCKETS // 2
    max_exact = half // 2
    ret = jnp.where(rel > 0, half, 0)
    n = jnp.abs(rel)
    nf = jnp.maximum(n, 1).astype(jnp.float32)
    large = max_exact + (jnp.log(nf / max_exact) / math.log(MAX_DISTANCE / max_exact)
                         * (half - max_exact)).astype(jnp.int32)
    large = jnp.minimum(large, half - 1)
    return ret + jnp.where(n < max_exact, n, large)


LOG2E = 1.4426950408889634


def _bias_kernel(rb_ref, bucket_ref, rel_ref, o_ref):
    bucket = bucket_ref[...]
    rel = rel_ref[...]
    key = lax.broadcasted_iota(jnp.int32, bucket.shape, 0)
    for h in range(N_HEADS_A):
        acc = jnp.zeros(bucket.shape, F32)
        for b in range(N_BUCKETS):
            acc = jnp.where(bucket == b, rb_ref[b, h], acc)
        acc = acc * LOG2E
        for v in range(4):
            ok = jnp.abs(rel) <= WINDOW
            if v & 1:
                ok = ok & (key >= BLOCK)
            if v & 2:
                ok = ok & (key < 2 * BLOCK)
            o_ref[v, h] = jnp.where(ok, acc, NEG)


def _bias_table(rel_bias):
    rel = (jnp.arange(3 * BLOCK)[:, None] - BLOCK) - jnp.arange(BLOCK)[None, :]
    rel = rel.astype(jnp.int32)
    bucket = _t5_bucket(rel).astype(jnp.int32)
    return pl.pallas_call(
        _bias_kernel,
        in_specs=[pl.BlockSpec(memory_space=pltpu.SMEM),
                  pl.BlockSpec(memory_space=pltpu.VMEM), pl.BlockSpec(memory_space=pltpu.VMEM)],
        out_specs=pl.BlockSpec(memory_space=pltpu.VMEM),
        out_shape=jax.ShapeDtypeStruct((4, N_HEADS_A, 3 * BLOCK, BLOCK), F32),
        name="rel_bias_table",
    )(rel_bias.astype(F32), bucket, rel)


def _group_ms(v, gmat):
    return jnp.dot((v * v).astype(BF16), gmat, preferred_element_type=F32)


def _ab_kernel(sink_ref, x_ref, xp_ref, xn_ref, g_ref, win_ref, wout_ref, gm_ref, qg_ref,
               kg_ref, bias_ref, vg_ref, ws_ref, bs_ref, o_ref, *, T):
    i = pl.program_id(1)
    nb = T // BLOCK
    nblocks = pl.num_programs(1) * nb
    gain = g_ref[...]
    x = x_ref[...]
    proj = jnp.dot(_rms(x, gain).astype(BF16), win_ref[...], preferred_element_type=F32)
    xh = jnp.concatenate([_rms(xp_ref[...], gain), _rms(xn_ref[...], gain)], axis=0)
    kvh = jnp.dot(xh.astype(BF16), win_ref[:, ATTN_W:ATTN_W + 2 * BLOCK],
                  preferred_element_type=F32)
    gm = gm_ref[...]
    gm_k = gm[:BLOCK, :BLOCK]

    q = proj[:, :ATTN_W]
    qn = q * lax.rsqrt(_group_ms(q, gm) + EPS) * qg_ref[...]
    k_all = jnp.concatenate([kvh[:BLOCK, :BLOCK], proj[:, ATTN_W:ATTN_W + BLOCK],
                             kvh[BLOCK:, :BLOCK]], axis=0)
    v_all = jnp.concatenate([kvh[:BLOCK, BLOCK:], proj[:, ATTN_W + BLOCK:ATTN_W + 2 * BLOCK],
                             kvh[BLOCK:, BLOCK:]], axis=0)
    kn = k_all * lax.rsqrt(_group_ms(k_all, gm_k) + EPS) * kg_ref[...]

    su = _gelu(proj[:, ATTN_W + 2 * BLOCK:ATTN_W + 2 * BLOCK + SGU_W])
    sv = _gelu(proj[:, ATTN_W + 2 * BLOCK + SGU_W:])
    svn = (sv * lax.rsqrt(_group_ms(sv, gm) + EPS) * vg_ref[...]).astype(BF16)
    low = lax.broadcasted_iota(jnp.int32, (1, BLOCK * nb), 1) % BLOCK < CH_B
    slabs = []
    for j in range(SGU_W // BLOCK):
        rhs = jnp.concatenate([svn[n * BLOCK:(n + 1) * BLOCK, j * BLOCK:(j + 1) * BLOCK]
                               for n in range(nb)], axis=1)
        a = jnp.dot(ws_ref[2 * j], rhs, preferred_element_type=F32)
        b = jnp.dot(ws_ref[2 * j + 1], rhs, preferred_element_type=F32)
        slabs.append(jnp.where(low, a, b))
    mixed = jnp.concatenate(
        [jnp.concatenate([slabs[j][:, n * BLOCK:(n + 1) * BLOCK] for j in range(SGU_W // BLOCK)],
                         axis=1) + bs_ref[...] for n in range(nb)], axis=0)
    sgu_out = jnp.dot((su * mixed).astype(BF16), wout_ref[ATTN_W:, :], preferred_element_type=F32)

    qt = qn.T.astype(BF16)
    vt = v_all.T.astype(BF16)
    knb = kn.astype(BF16)
    zeros_q = jnp.zeros((HEAD_DIM, GQA * BLOCK), BF16)
    ones_rows = jnp.ones((2 * SUBLANES, 3 * BLOCK), BF16)
    units = [(n, hk) for n in range(nb) for hk in range(N_KV_A)]
    sks = [jnp.concatenate([jnp.full((1, BLOCK), sink_ref[hk * GQA + g] * LOG2E, F32)
                            for g in range(GQA)], axis=1) for hk in range(N_KV_A)]
    scores = []
    for n, hk in units:
        blk = i * nb + n
        variant = jnp.where(blk == 0, 1, 0) + jnp.where(blk == nblocks - 1, 2, 0)
        qh = jnp.concatenate(
            [qt[(hk * GQA + g) * HEAD_DIM:(hk * GQA + g + 1) * HEAD_DIM, n * BLOCK:(n + 1) * BLOCK]
             for g in range(GQA)], axis=1)
        qz = jnp.concatenate([qh, zeros_q] if hk == 0 else [zeros_q, qh], axis=0)
        s = jnp.dot(knb[n * BLOCK:n * BLOCK + 3 * BLOCK, :], qz,
                    preferred_element_type=F32)
        scores.append(s + jnp.concatenate([bias_ref[variant, hk * GQA + g] for g in range(GQA)], axis=1))
    probs = []
    for (n, hk), s in zip(units, scores):
        m = jnp.maximum(jnp.max(s, axis=0, keepdims=True), sks[hk])
        probs.append((jnp.exp2(s - m).astype(BF16), jnp.exp2(sks[hk] - m)))
    outs = []
    for (n, hk), (p, psink) in zip(units, probs):
        lhs = jnp.concatenate([vt[hk * HEAD_DIM:(hk + 1) * HEAD_DIM, n * BLOCK:n * BLOCK + 3 * BLOCK],
                               ones_rows], axis=0)
        pv = jnp.dot(lhs, p, preferred_element_type=F32)
        outs.append(pv[:HEAD_DIM] / (pv[HEAD_DIM:HEAD_DIM + 1] + psink))
    out_cols = []
    for n in range(nb):
        out_cols.append(jnp.concatenate(
            [outs[n * N_KV_A + hk][:, g * BLOCK:(g + 1) * BLOCK]
             for hk in range(N_KV_A) for g in range(GQA)], axis=0))
    attn = jnp.concatenate(out_cols, axis=1).T.astype(BF16)
    o_ref[...] = x + sgu_out + jnp.dot(attn, wout_ref[:ATTN_W, :], preferred_element_type=F32)


def _ab_layer(x, sink, gain, win, wout, gm, qg, kg, bias_tab, vg, ws, bs, T):
    B, L, D = x.shape
    hb = T // BLOCK
    last = L // BLOCK - 1
    kern = functools.partial(_ab_kernel, T=T)
    consts = [gain, win, wout, gm, qg, kg, bias_tab, vg, ws, bs]
    return pl.pallas_call(
        kern,
        grid=(B, L // T),
        in_specs=[pl.BlockSpec(memory_space=pltpu.SMEM),
                  pl.BlockSpec((None, T, D), lambda b, i: (b, i, 0)),
                  pl.BlockSpec((None, BLOCK, D), lambda b, i: (b, jnp.maximum(i * hb - 1, 0), 0)),
                  pl.BlockSpec((None, BLOCK, D), lambda b, i: (b, jnp.minimum((i + 1) * hb, last), 0)),
                  ] + [_const_spec(a.shape) for a in consts],
        out_specs=pl.BlockSpec((None, T, D), lambda b, i: (b, i, 0)),
        out_shape=jax.ShapeDtypeStruct(x.shape, F32),
        compiler_params=_params(2),
        name="attn_sgu_mixer",
    )(sink, x, x, x, *consts)


def _chunk_cols(w, parts, chunk):
    lead = w.shape[:-1]
    n = w.shape[-1] // (parts * chunk)
    w = w.reshape(lead + (parts, n, chunk))
    w = jnp.moveaxis(w, -2, 0)
    return w.reshape((n,) + lead + (parts * chunk,))


def _prepare(p):
    depth = p["mix_norm"].shape[0]
    q = dict(depth=depth)
    q["mix_norm"] = [p["mix_norm"][l][None, :] for l in range(depth)]
    q["ffn_norm"] = [p["ffn_norm"][l][None, :] for l in range(depth)]
    q["ffn_wup"] = [_chunk_cols(p["ffn_w_up"][l], 2, FFN_CHUNK).astype(BF16) for l in range(depth)]
    q["ffn_cw"] = [_chunk_cols(p["ffn_conv_w"][l], 2, FFN_CHUNK) for l in range(depth)]
    q["ffn_cb"] = [_chunk_cols(p["ffn_conv_b"][l][None, :], 2, FFN_CHUNK) for l in range(depth)]
    q["ffn_wdn"] = [p["ffn_w_down"][l].astype(BF16) for l in range(depth)]

    n_even = p["ab_w_in"].shape[0]
    q["ab_win"] = [p["ab_w_in"][i].astype(BF16) for i in range(n_even)]
    q["ab_wout"] = [p["ab_w_out"][i].astype(BF16) for i in range(n_even)]
    q["qg"] = [jnp.tile(p["q_norm"][i], N_HEADS_A)[None, :] * (HEAD_DIM ** -0.5 * LOG2E)
               for i in range(n_even)]
    q["kg"] = [jnp.tile(p["k_norm"][i], N_KV_A)[None, :] for i in range(n_even)]
    q["sink"] = [p["attn_sink"][i].astype(F32) for i in range(n_even)]
    q["vg"] = [p["sgu_v_norm"][i].reshape(1, SGU_W) for i in range(n_even)]
    q["ws"] = [p["sgu_w"][i].astype(BF16) for i in range(n_even)]
    q["bs"] = [jnp.repeat(p["sgu_b"][i].T, CH_B, axis=1) for i in range(n_even)]
    grp = np.arange(ATTN_W) // HEAD_DIM
    q["gm"] = jnp.asarray((grp[:, None] == grp[None, :]) / float(HEAD_DIM), dtype=BF16)

    n_odd = p["hy_w_in"].shape[0]
    q["hy_win"] = [_chunk_cols(p["hy_w_in"][i], 3, HY_CHUNK).astype(BF16) for i in range(n_odd)]
    q["hy_cw"] = [_chunk_cols(p["hy_conv_w"][i], 3, HY_CHUNK) for i in range(n_odd)]
    q["hy_cb"] = [_chunk_cols(p["hy_conv_b"][i][None, :], 3, HY_CHUNK) for i in range(n_odd)]
    q["hy_d"] = [p["hy_d"][i][:, None].astype(F32) for i in range(n_odd)]
    q["hy_wout"] = [p["hy_w_out"][i].astype(BF16) for i in range(n_odd)]
    slot = np.arange(LANES) % FILTER_WIDTH
    fr = np.linspace(1e-4, FILTER_BANDS - 1, FILTER_BANDS).astype(np.float32)
    frl = np.where((slot >= 1) & (slot <= 2 * FILTER_BANDS), fr[(slot - 1) % FILTER_BANDS], 0.0)
    ph = np.where((slot > FILTER_BANDS) & (slot <= 2 * FILTER_BANDS), 0.5 * np.pi, 0.0)
    q["frl"] = jnp.asarray(frl[None, :], dtype=F32)
    q["ph"] = jnp.asarray(ph[None, :], dtype=F32)
    q["delta"] = jnp.abs(jnp.linspace(MIN_DECAY, MAX_DECAY, D_MODEL, dtype=F32))[None, :]

    def twice(w):
        z = jnp.zeros_like(w)
        return jnp.concatenate([jnp.concatenate([w, z], axis=1), jnp.concatenate([z, w], axis=1)], axis=0)

    filt = []
    for i in range(n_odd):
        w1p = jnp.zeros((FILTER_WIDTH, FILTER_WIDTH), F32).at[:FILTER_EMB].set(p["hy_f_w1"][i])
        row = lambda a: jnp.tile(a[None, :].astype(F32), (1, 2))
        filt.append((twice(w1p), row(p["hy_f_b1"][i]), row(p["hy_f_freq1"][i]),
                     twice(p["hy_f_w2"][i]), row(p["hy_f_b2"][i]), row(p["hy_f_freq2"][i]),
                     twice(p["hy_f_w3"][i]), row(p["hy_f_b3"][i]), row(p["hy_f_freq3"][i]),
                     twice(p["hy_f_wout"][i])))
    q["filt"] = filt
    q["bias_tab"] = _bias_table(p["rel_bias"])
    return q


def _hyena_conv(x, q, l, consts, T):
    i = l // 2
    L = x.shape[1]
    hf, hb = _filter(L, q["frl"], q["ph"], *q["filt"][i], q["delta"])
    kf = _filter_spectrum(hf, hb, q["hy_d"][i], consts)
    x0, zt = _hy_in(x, q["mix_norm"][l], q["hy_win"][i], q["hy_cw"][i], q["hy_cb"][i], T)
    return x0, _fftconv(zt, kf, consts)


def _trunk(x, q, T=512):
    B, L, D = x.shape
    consts = _dft_consts(L)
    for l in range(q["depth"]):
        i = l // 2
        ffn = (q["ffn_norm"][l], q["ffn_wup"][l], q["ffn_cw"][l], q["ffn_cb"][l], q["ffn_wdn"][l])
        if l % 2 == 0:
            x = _ab_layer(x, q["sink"][i], q["mix_norm"][l], q["ab_win"][i], q["ab_wout"][i],
                          q["gm"], q["qg"][i], q["kg"][i], q["bias_tab"], q["vg"][i], q["ws"][i],
                          q["bs"][i], T)
            x = _ffn(x, *ffn, T)
        else:
            x0, yt = _hyena_conv(x, q, l, consts, T)
            x = _hy_ffn(x, x0, yt, q["hy_wout"][i], *ffn, T)
    return x


def kernel(x_prompt, x_sample, rel_bias, mix_norm, ffn_norm, ab_w_in, q_norm, k_norm, attn_sink, sgu_v_norm, sgu_w, sgu_b, ab_w_out, hy_w_in, hy_conv_w, hy_conv_b, hy_f_w1, hy_f_b1, hy_f_freq1, hy_f_w2, hy_f_b2, hy_f_freq2, hy_f_w3, hy_f_b3, hy_f_freq3, hy_f_wout, hy_d, hy_w_out, ffn_w_up, ffn_conv_w, ffn_conv_b, ffn_w_down):
    p = dict(rel_bias=rel_bias, mix_norm=mix_norm, ffn_norm=ffn_norm, ab_w_in=ab_w_in,
             q_norm=q_norm, k_norm=k_norm, attn_sink=attn_sink, sgu_v_norm=sgu_v_norm,
             sgu_w=sgu_w, sgu_b=sgu_b, ab_w_out=ab_w_out, hy_w_in=hy_w_in, hy_conv_w=hy_conv_w,
             hy_conv_b=hy_conv_b, hy_f_w1=hy_f_w1, hy_f_b1=hy_f_b1, hy_f_freq1=hy_f_freq1,
             hy_f_w2=hy_f_w2, hy_f_b2=hy_f_b2, hy_f_freq2=hy_f_freq2, hy_f_w3=hy_f_w3,
             hy_f_b3=hy_f_b3, hy_f_freq3=hy_f_freq3, hy_f_wout=hy_f_wout, hy_d=hy_d,
             hy_w_out=hy_w_out, ffn_w_up=ffn_w_up, ffn_conv_w=ffn_conv_w, ffn_conv_b=ffn_conv_b,
             ffn_w_down=ffn_w_down)
    q = _prepare(p)
    return (_trunk(x_prompt, q), _trunk(x_sample, q))
```

```python
import functools
import math

import numpy as np
import jax
import jax.numpy as jnp
from jax import lax
from jax.experimental import pallas as pl
from jax.experimental.pallas import tpu as pltpu

F32 = jnp.float32
BF16 = jnp.bfloat16

D_MODEL = 1024
HEAD_DIM = 64
N_HEADS_A = 8
N_KV_A = 2
GQA = 4
ATTN_W = 512
WINDOW = 128
BLOCK = 128
N_BUCKETS = 32
MAX_DISTANCE = 128
SGU_W = 512
N_GROUPS_B = 8
CH_B = 64
FILTER_EMB = 33
FILTER_BANDS = 16
FILTER_WIDTH = 64
DECAY_TARGET = 1e-2
MIN_DECAY = math.log(DECAY_TARGET) / 1.5
MAX_DECAY = math.log(DECAY_TARGET) / 0.3
FFN_HIDDEN = 2816
EPS = 1e-6
NEG = -1e30

LANES = 128
SUBLANES = 8
FFN_CHUNK = 256
HY_CHUNK = 256
HALO = 8
ROW_BLOCK = 64
FFT_N2 = LANES
FFT_ROWS = 4096
VMEM_LIMIT = 56 * 1024 * 1024


def _gelu(x):
    t = jnp.tanh(x * (0.7978845608028654 + (0.7978845608028654 * 0.044715) * (x * x)))
    hx = 0.5 * x
    return hx + hx * t


def _rms(x, gain):
    return x * lax.rsqrt(jnp.mean(x * x, axis=-1, keepdims=True) + EPS) * gain


def _const_spec(shape):
    nd = len(shape)
    return pl.BlockSpec(shape, lambda *_: (0,) * nd, pipeline_mode=pl.Buffered(1))


def _params(n_axes):
    return pltpu.CompilerParams(dimension_semantics=("arbitrary",) * n_axes,
                                vmem_limit_bytes=VMEM_LIMIT)


def _park_slabs(s_ref, base, h):
    for s in range(h.shape[1] // LANES):
        s_ref[base + s] = h[:, s * LANES:(s + 1) * LANES]


def _conv3_slab(s_ref, slab, w, b, r0, rows):
    lo = HALO + r0
    return (s_ref[slab, lo - 1:lo - 1 + rows, :] * w[0:1]
            + s_ref[slab, lo:lo + rows, :] * w[1:2]
            + s_ref[slab, lo + 1:lo + 1 + rows, :] * w[2:3]
            + b)


def _fill_normed_ext(xe_ref, x, xp, xn, gain, rows, first, last):
    pm = jnp.where(first, 0.0, 1.0)
    nm = jnp.where(last, 0.0, 1.0)
    xe_ref[0:HALO, :] = (_rms(xp, gain) * pm).astype(BF16)
    xe_ref[HALO:HALO + rows, :] = _rms(x, gain).astype(BF16)
    xe_ref[HALO + rows:HALO + rows + HALO, :] = (_rms(xn, gain) * nm).astype(BF16)


def _halo_specs(T, L, D):
    hb = T // HALO
    last = L // HALO - 1
    return [
        pl.BlockSpec((None, T, D), lambda b, i: (b, i, 0)),
        pl.BlockSpec((None, HALO, D), lambda b, i: (b, jnp.maximum(i * hb - 1, 0), 0)),
        pl.BlockSpec((None, HALO, D), lambda b, i: (b, jnp.minimum((i + 1) * hb, last), 0)),
    ]


def _ffn_body(x, xp, xn, g_ref, wup_ref, cw_ref, cb_ref, wdn_ref, o_ref, xe_ref, a_ref, s_ref,
              T, nchunk):
    i = pl.program_id(1)
    _fill_normed_ext(xe_ref, x, xp, xn, g_ref[...], T, i == 0, i == pl.num_programs(1) - 1)
    xe = xe_ref[...]
    ns = FFN_CHUNK // LANES
    for j in range(nchunk):
        base = (j % 2) * 2 * ns
        for part in range(2):
            lo = part * FFN_HIDDEN + j * FFN_CHUNK
            _park_slabs(s_ref, base + part * ns,
                        jnp.dot(xe, wup_ref[:, lo:lo + FFN_CHUNK], preferred_element_type=F32))
        for s in range(ns):
            col = j * FFN_CHUNK + s * LANES
            wg, bg = cw_ref[:, col:col + LANES], cb_ref[:, col:col + LANES]
            wu = cw_ref[:, FFN_HIDDEN + col:FFN_HIDDEN + col + LANES]
            bu = cb_ref[:, FFN_HIDDEN + col:FFN_HIDDEN + col + LANES]
            for r0 in range(0, T, ROW_BLOCK):
                g = _conv3_slab(s_ref, base + s, wg, bg, r0, ROW_BLOCK)
                u = _conv3_slab(s_ref, base + ns + s, wu, bu, r0, ROW_BLOCK)
                a_ref[r0:r0 + ROW_BLOCK, col:col + LANES] = (_gelu(g) * u).astype(BF16)
    o_ref[...] = x + jnp.dot(a_ref[...], wdn_ref[...], preferred_element_type=F32)


def _ffn_kernel(x_ref, xp_ref, xn_ref, g_ref, wup_ref, cw_ref, cb_ref, wdn_ref, o_ref,
                xe_ref, a_ref, s_ref, *, T, nchunk):
    _ffn_body(x_ref[...], xp_ref[...], xn_ref[...], g_ref, wup_ref, cw_ref, cb_ref, wdn_ref,
              o_ref, xe_ref, a_ref, s_ref, T, nchunk)


def _ffn_scratch(T, D):
    return [pltpu.VMEM((T + 2 * HALO, D), BF16), pltpu.VMEM((T, FFN_HIDDEN), BF16),
            pltpu.VMEM((2 * 2 * FFN_CHUNK // LANES, T + 2 * HALO, LANES), F32)]


def _hy_ffn_kernel(x_ref, xp_ref, xn_ref, x0_ref, x0p_ref, x0n_ref, yt_ref, ytp_ref, ytn_ref,
                   wo_ref, g_ref, wup_ref, cw_ref, cb_ref, wdn_ref, o_ref,
                   xe_ref, a_ref, s_ref, m_ref, *, T, nchunk):
    m_ref[0:HALO, :] = (x0p_ref[...] * ytp_ref[...].T[BLOCK - HALO:, :]).astype(BF16)
    for n in range(T // BLOCK):
        lo = HALO + n * BLOCK
        m_ref[lo:lo + BLOCK, :] = (x0_ref[n * BLOCK:(n + 1) * BLOCK, :] * yt_ref[n].T).astype(BF16)
    m_ref[HALO + T:, :] = (x0n_ref[...] * ytn_ref[...].T[:HALO, :]).astype(BF16)
    mix = jnp.dot(m_ref[...], wo_ref[...], preferred_element_type=F32)
    _ffn_body(x_ref[...] + mix[HALO:HALO + T], xp_ref[...] + mix[:HALO], xn_ref[...] + mix[HALO + T:],
              g_ref, wup_ref, cw_ref, cb_ref, wdn_ref, o_ref, xe_ref, a_ref, s_ref, T, nchunk)


def _hy_ffn(x, x0, yt, wo, gain, wup_r, cw_r, cb_r, wdn, T):
    B, L, D = x.shape
    nchunk = FFN_HIDDEN // FFN_CHUNK
    nb = T // BLOCK
    last = L // BLOCK - 1
    kern = functools.partial(_hy_ffn_kernel, T=T, nchunk=nchunk)
    return pl.pallas_call(
        kern,
        grid=(B, L // T),
        in_specs=_halo_specs(T, L, D) + _halo_specs(T, L, D) + [
            pl.BlockSpec((None, nb, D, LANES), lambda b, i: (b, i, 0, 0)),
            pl.BlockSpec((None, None, D, LANES), lambda b, i: (b, jnp.maximum(i * nb - 1, 0), 0, 0)),
            pl.BlockSpec((None, None, D, LANES), lambda b, i: (b, jnp.minimum((i + 1) * nb, last), 0, 0)),
            _const_spec(wo.shape), _const_spec((1, D)), _const_spec(wup_r.shape),
            _const_spec(cw_r.shape), _const_spec(cb_r.shape), _const_spec(wdn.shape)],
        out_specs=pl.BlockSpec((None, T, D), lambda b, i: (b, i, 0)),
        out_shape=jax.ShapeDtypeStruct(x.shape, F32),
        scratch_shapes=_ffn_scratch(T, D) + [pltpu.VMEM((T + 2 * HALO, D), BF16)],
        compiler_params=_params(2),
        name="hyena_out_conv_ffn",
    )(x, x, x, x0, x0, x0, yt, yt, yt, wo, gain, wup_r, cw_r, cb_r, wdn)


def _ffn(x, gain, wup_r, cw_r, cb_r, wdn, T):
    B, L, D = x.shape
    nchunk = FFN_HIDDEN // FFN_CHUNK
    kern = functools.partial(_ffn_kernel, T=T, nchunk=nchunk)
    return pl.pallas_call(
        kern,
        grid=(B, L // T),
        in_specs=_halo_specs(T, L, D) + [
            _const_spec((1, D)), _const_spec(wup_r.shape), _const_spec(cw_r.shape),
            _const_spec(cb_r.shape), _const_spec(wdn.shape)],
        out_specs=pl.BlockSpec((None, T, D), lambda b, i: (b, i, 0)),
        out_shape=jax.ShapeDtypeStruct(x.shape, F32),
        scratch_shapes=_ffn_scratch(T, D),
        compiler_params=_params(2),
        name="conv_ffn",
    )(x, x, x, gain, wup_r, cw_r, cb_r, wdn)


def _hy_in_kernel(x_ref, xp_ref, xn_ref, g_ref, win_ref, cw_ref, cb_ref, x0_ref, zt_ref,
                  xe_ref, s_ref, *, T, nchunk):
    i = pl.program_id(1)
    _fill_normed_ext(xe_ref, x_ref[...], xp_ref[...], xn_ref[...], g_ref[...], T,
                     i == 0, i == pl.num_programs(1) - 1)
    xe = xe_ref[...]
    ns = HY_CHUNK // LANES
    for j in range(nchunk):
        base = (j % 2) * 3 * ns
        for k in range(3):
            lo = k * D_MODEL + j * HY_CHUNK
            _park_slabs(s_ref, base + k * ns,
                        jnp.dot(xe, win_ref[:, lo:lo + HY_CHUNK], preferred_element_type=F32))
        for s in range(ns):
            col = j * HY_CHUNK + s * LANES
            w = [cw_ref[:, k * D_MODEL + col:k * D_MODEL + col + LANES] for k in range(3)]
            b = [cb_ref[:, k * D_MODEL + col:k * D_MODEL + col + LANES] for k in range(3)]
            for n in range(T // BLOCK):
                r0 = n * BLOCK
                x0_ref[r0:r0 + BLOCK, col:col + LANES] = _conv3_slab(s_ref, base + s, w[0], b[0], r0, BLOCK)
                z = (_conv3_slab(s_ref, base + ns + s, w[1], b[1], r0, BLOCK)
                     * _conv3_slab(s_ref, base + 2 * ns + s, w[2], b[2], r0, BLOCK))
                zt_ref[n, col:col + LANES, :] = z.T


def _hy_in(x, gain, win_r, cw_r, cb_r, T):
    B, L, D = x.shape
    nchunk = D // HY_CHUNK
    kern = functools.partial(_hy_in_kernel, T=T, nchunk=nchunk)
    return pl.pallas_call(
        kern,
        grid=(B, L // T),
        in_specs=_halo_specs(T, L, D) + [
            _const_spec((1, D)), _const_spec(win_r.shape), _const_spec(cw_r.shape),
            _const_spec(cb_r.shape)],
        out_specs=[pl.BlockSpec((None, T, D), lambda b, i: (b, i, 0)),
                   pl.BlockSpec((None, T // BLOCK, D, LANES), lambda b, i: (b, i, 0, 0))],
        out_shape=[jax.ShapeDtypeStruct(x.shape, F32),
                   jax.ShapeDtypeStruct((B, L // BLOCK, D, LANES), F32)],
        scratch_shapes=[pltpu.VMEM((T + 2 * HALO, D), BF16),
                        pltpu.VMEM((2 * 3 * HY_CHUNK // LANES, T + 2 * HALO, LANES), F32)],
        compiler_params=_params(2),
        name="hyena_in",
    )(x, x, x, gain, win_r, cw_r, cb_r)


def _dot_f32(a, b):
    return jnp.dot(a, b, preferred_element_type=F32, precision=lax.Precision.HIGHEST)


def _filter_kernel(frl_ref, ph_ref, w1_ref, b1_ref, f1_ref, w2_ref, b2_ref, f2_ref, w3_ref, b3_ref,
                   f3_ref, wout_ref, delta_ref, hf_ref, hb_ref, *, L, R):
    hr = R // 2
    ja = pl.program_id(0) * R + lax.broadcasted_iota(jnp.int32, (hr, 1), 0)
    jb = ja + hr
    lane = lax.broadcasted_iota(jnp.int32, (hr, LANES), 1)
    j = jnp.where(lane < FILTER_WIDTH, ja, jb).astype(F32)
    t = j / float(L - 1)
    w = (2.0 * math.pi) * j / float(L)
    feats = jnp.where(lane % FILTER_WIDTH == 0, t, jnp.cos(w * frl_ref[...] + ph_ref[...]))
    h = jnp.sin(f1_ref[...] * (_dot_f32(feats, w1_ref[...]) + b1_ref[...]))
    h = jnp.sin(f2_ref[...] * (_dot_f32(h, w2_ref[...]) + b2_ref[...]))
    h = jnp.sin(f3_ref[...] * (_dot_f32(h, w3_ref[...]) + b3_ref[...]))
    h = _dot_f32(h, wout_ref[...])
    for half, jidx in enumerate((ja, jb)):
        decay = jnp.exp(-(jidx.astype(F32) / float(L - 1)) * delta_ref[...])
        base = half * 2 * D_MODEL
        hf = h[:, base:base + D_MODEL] * decay
        hb = jnp.where(jidx == 0, 0.0, h[:, base + D_MODEL:base + 2 * D_MODEL] * decay)
        for n in range(hr // BLOCK):
            blk = half * (hr // BLOCK) + n
            hf_ref[blk] = hf[n * BLOCK:(n + 1) * BLOCK, :].T
            hb_ref[blk] = hb[n * BLOCK:(n + 1) * BLOCK, :].T


def _filter(L, frl, ph, w1, b1, f1, w2, b2, f2, w3, b3, f3, wout, delta):
    R = 512
    kern = functools.partial(_filter_kernel, L=L, R=R)
    consts = [frl, ph, w1, b1, f1, w2, b2, f2, w3, b3, f3, wout, delta]
    out_spec = pl.BlockSpec((R // BLOCK, D_MODEL, LANES), lambda i: (i, 0, 0))
    shape = jax.ShapeDtypeStruct((L // BLOCK, D_MODEL, LANES), F32)
    return pl.pallas_call(
        kern,
        grid=(L // R,),
        in_specs=[_const_spec(a.shape) for a in consts],
        out_specs=[out_spec, out_spec],
        out_shape=[shape, shape],
        compiler_params=_params(1),
        name="hyena_filter",
    )(*consts)


def _dft_consts(L):
    N = 2 * L
    N2 = FFT_N2
    N1 = N // N2
    h = N1 // 2
    a1 = -2.0 * np.pi * np.outer(np.arange(N1), np.arange(N1)) / N1
    f1r, f1i = np.cos(a1), np.sin(a1)
    a2 = -2.0 * np.pi * np.outer(np.arange(N2), np.arange(N2)) / N2
    f2r, f2i = np.cos(a2), np.sin(a2)
    at = -2.0 * np.pi * np.outer(np.arange(N1), np.arange(N2)) / N
    m1 = np.block([[f1r[:, :h], -f1i[:, :h]], [f1i[:, :h], f1r[:, :h]]])
    m1f = np.concatenate([f1r[:, :h], f1i[:, :h]], axis=0)
    m3 = np.block([[f1r[:h], f1i[:h]], [-f1i[:h], f1r[:h]]])
    c = lambda a: jnp.asarray(a, dtype=BF16)
    f = lambda a: jnp.asarray(a, dtype=F32)
    return dict(N1=N1, m1=c(m1), m1f=c(m1f), m3=c(m3),
                r2f=c(np.block([[f2r, f2i], [-f2i, f2r]])),
                r2i=c(np.block([[f2r, -f2i], [f2i, f2r]])),
                twr=f(np.cos(at)), twi=f(np.sin(at)))


def _cmul(ar, ai, br, bi):
    return ar * br - ai * bi, ar * bi + ai * br


def _slab8(ref, idx, c0, n):
    v = ref[idx + (slice(None), slice(c0, c0 + SUBLANES), slice(None))]
    return v.reshape(n * SUBLANES, LANES)


def _chan_rows(s_ref, base, c, n):
    return s_ref[pl.ds(base + c, n, stride=SUBLANES), :]


def _fft_kernel(x_ref, kf_ref, m1_ref, m3_ref, twr_ref, twi_ref, r2f_ref, r2i_ref, o_ref,
                *scratch, N1, nc):
    h = N1 // 2
    h8 = h * SUBLANES
    nsub = nc // SUBLANES
    twr, twi = twr_ref[...], twi_ref[...]
    cols = []
    for s8 in range(nsub):
        sin_ref = scratch[s8]
        sin_ref[0:h8] = _slab8(x_ref, (0,), s8 * SUBLANES, h)
        sin_ref[h8:2 * h8] = _slab8(x_ref, (1,), s8 * SUBLANES, h)
        cols += [jnp.concatenate([_chan_rows(sin_ref, 0, c, h), _chan_rows(sin_ref, h8, c, h)], axis=0)
                 for c in range(SUBLANES)]
    a = jnp.dot(m1_ref[...], jnp.concatenate(cols, axis=1).astype(BF16),
                preferred_element_type=F32)
    lhs = []
    for c in range(nc):
        tr, ti = _cmul(a[:N1, c * LANES:(c + 1) * LANES], a[N1:, c * LANES:(c + 1) * LANES], twr, twi)
        lhs.append(jnp.concatenate([tr, ti], axis=1))
    lhs = jnp.concatenate(lhs, axis=0).astype(BF16)
    X = jnp.dot(lhs, r2f_ref[...], preferred_element_type=F32)
    yr, yi = _cmul(X[:, :LANES], X[:, LANES:],
                   kf_ref[0].reshape(nc * N1, LANES), kf_ref[1].reshape(nc * N1, LANES))
    Bm = jnp.dot(jnp.concatenate([yr, yi], axis=1).astype(BF16), r2i_ref[...],
                 preferred_element_type=F32)
    re_cols, im_cols = [], []
    for c in range(nc):
        br, bi = Bm[c * N1:(c + 1) * N1, :LANES], Bm[c * N1:(c + 1) * N1, LANES:]
        re_cols.append(br * twr + bi * twi)
        im_cols.append(bi * twr - br * twi)
    rhs = jnp.concatenate([jnp.concatenate(re_cols, axis=1),
                           jnp.concatenate(im_cols, axis=1)], axis=0).astype(BF16)
    y = jnp.dot(m3_ref[...], rhs, preferred_element_type=F32)
    for s8 in range(nsub):
        c0 = s8 * SUBLANES
        sout_ref = scratch[nsub + s8]
        for c in range(SUBLANES):
            lo = (c0 + c) * LANES
            sout_ref[pl.ds(c, h, stride=SUBLANES), :] = y[:h, lo:lo + LANES]
            sout_ref[pl.ds(h8 + c, h, stride=SUBLANES), :] = y[h:, lo:lo + LANES]
        o_ref[0, :, c0:c0 + SUBLANES, :] = sout_ref[0:h8].reshape(h, SUBLANES, LANES)
        o_ref[1, :, c0:c0 + SUBLANES, :] = sout_ref[h8:2 * h8].reshape(h, SUBLANES, LANES)


def _fftconv(zt, kf, c):
    B, h, C, _ = zt.shape
    N1 = c["N1"]
    nc = min(FFT_ROWS // N1, C)
    kern = functools.partial(_fft_kernel, N1=N1, nc=nc)
    blk = pl.BlockSpec((2, h, nc, LANES), lambda cb, p: (p, 0, cb, 0))
    consts = [c["m1"], c["m3"], c["twr"], c["twi"], c["r2f"], c["r2i"]]
    return pl.pallas_call(
        kern,
        grid=(C // nc, B // 2),
        in_specs=[blk, pl.BlockSpec((2, nc, N1, LANES), lambda cb, p: (0, cb, 0, 0))]
        + [_const_spec(a.shape) for a in consts],
        out_specs=blk,
        out_shape=jax.ShapeDtypeStruct(zt.shape, F32),
        scratch_shapes=[pltpu.VMEM((N1 * SUBLANES, LANES), F32)] * (2 * nc // SUBLANES),
        compiler_params=_params(2),
        name="fft_conv",
    )(zt, kf, *consts)


def _fspec_kernel(hf_ref, hb_ref, d_ref, m1f_ref, twr_ref, twi_ref, r2f_ref, o_ref,
                  *scratch, N1, nc, scale):
    h = N1 // 2
    h8 = h * SUBLANES
    twr, twi = twr_ref[...], twi_ref[...]
    cols = []
    for s8 in range(nc // SUBLANES):
        s_ref = scratch[s8]
        s_ref[0:h8] = _slab8(hf_ref, (), s8 * SUBLANES, h)
        s_ref[h8:2 * h8] = _slab8(hb_ref, (), s8 * SUBLANES, h)
        for c in range(SUBLANES):
            cols += [_chan_rows(s_ref, 0, c, h), _chan_rows(s_ref, h8, c, h)]
    a = jnp.dot(m1f_ref[...], jnp.concatenate(cols, axis=1).astype(BF16),
                preferred_element_type=F32)
    lhs = []
    for k in range(2 * nc):
        tr, ti = _cmul(a[:N1, k * LANES:(k + 1) * LANES], a[N1:, k * LANES:(k + 1) * LANES], twr, twi)
        lhs.append(jnp.concatenate([tr, ti], axis=1))
    lhs = jnp.concatenate(lhs, axis=0).astype(BF16)
    X = jnp.dot(lhs, r2f_ref[...], preferred_element_type=F32)
    for c in range(nc):
        xf = X[(2 * c) * N1:(2 * c + 1) * N1]
        xb = X[(2 * c + 1) * N1:(2 * c + 2) * N1]
        d = d_ref[c:c + 1, :]
        o_ref[0, c] = (xf[:, :LANES] + xb[:, :LANES] + d) * scale
        o_ref[1, c] = (xf[:, LANES:] - xb[:, LANES:]) * scale


def _filter_spectrum(hf, hb, d, c):
    h, C, _ = hf.shape
    N1 = c["N1"]
    nc = min(FFT_ROWS // N1, C)
    kern = functools.partial(_fspec_kernel, N1=N1, nc=nc, scale=1.0 / (N1 * FFT_N2))
    blk = pl.BlockSpec((h, nc, LANES), lambda cb: (0, cb, 0))
    consts = [c["m1f"], c["twr"], c["twi"], c["r2f"]]
    return pl.pallas_call(
        kern,
        grid=(C // nc,),
        in_specs=[blk, blk, pl.BlockSpec((nc, 1), lambda cb: (cb, 0))]
        + [_const_spec(a.shape) for a in consts],
        out_specs=pl.BlockSpec((2, nc, N1, LANES), lambda cb: (0, cb, 0, 0)),
        out_shape=jax.ShapeDtypeStruct((2, C, N1, LANES), F32),
        scratch_shapes=[pltpu.VMEM((N1 * SUBLANES, LANES), F32)] * (nc // SUBLANES),
        compiler_params=_params(1),
        name="filter_spectrum",
    )(hf, hb, d, *consts)


def _t5_bucket(rel):
    half = N_BUCKETS // 2
    max_exact = half // 2
    ret = jnp.where(rel > 0, half, 0)
    n = jnp.abs(rel)
    nf = jnp.maximum(n, 1).astype(jnp.float32)
    large = max_exact + (jnp.log(nf / max_exact) / math.log(MAX_DISTANCE / max_exact)
                         * (half - max_exact)).astype(jnp.int32)
    large = jnp.minimum(large, half - 1)
    return ret + jnp.where(n < max_exact, n, large)


LOG2E = 1.4426950408889634


def _bias_kernel(rb_ref, bucket_ref, rel_ref, o_ref):
    bucket = bucket_ref[...]
    rel = rel_ref[...]
    key = lax.broadcasted_iota(jnp.int32, bucket.shape, 0)
    for h in range(N_HEADS_A):
        acc = jnp.zeros(bucket.shape, F32)
        for b in range(N_BUCKETS):
            acc = jnp.where(bucket == b, rb_ref[b, h], acc)
        acc = acc * LOG2E
        for v in range(4):
            ok = jnp.abs(rel) <= WINDOW
            if v & 1:
                ok = ok & (key >= BLOCK)
            if v & 2:
                ok = ok & (key < 2 * BLOCK)
            o_ref[v, h] = jnp.where(ok, acc, NEG)


def _bias_table(rel_bias):
    rel = (jnp.arange(3 * BLOCK)[:, None] - BLOCK) - jnp.arange(BLOCK)[None, :]
    rel = rel.astype(jnp.int32)
    bucket = _t5_bucket(rel).astype(jnp.int32)
    return pl.pallas_call(
        _bias_kernel,
        in_specs=[pl.BlockSpec(memory_space=pltpu.SMEM),
                  pl.BlockSpec(memory_space=pltpu.VMEM), pl.BlockSpec(memory_space=pltpu.VMEM)],
        out_specs=pl.BlockSpec(memory_space=pltpu.VMEM),
        out_shape=jax.ShapeDtypeStruct((4, N_HEADS_A, 3 * BLOCK, BLOCK), F32),
        name="rel_bias_table",
    )(rel_bias.astype(F32), bucket, rel)


def _group_ms(v, gmat):
    return jnp.dot((v * v).astype(BF16), gmat, preferred_element_type=F32)


def _ab_kernel(sink_ref, x_ref, xp_ref, xn_ref, g_ref, win_ref, wout_ref, gm_ref,
               kg_ref, bias_ref, vg_ref, ws_ref, bs_ref, o_ref, *, T):
    i = pl.program_id(1)
    nb = T // BLOCK
    nblocks = pl.num_programs(1) * nb
    gain = g_ref[...]
    x = x_ref[...]
    proj = jnp.dot(_rms(x, gain).astype(BF16), win_ref[...], preferred_element_type=F32)
    xh = jnp.concatenate([_rms(xp_ref[...], gain), _rms(xn_ref[...], gain)], axis=0)
    kvh = jnp.dot(xh.astype(BF16), win_ref[:, ATTN_W:ATTN_W + 2 * BLOCK],
                  preferred_element_type=F32)
    gm = gm_ref[...]
    gm_k = gm[:BLOCK, :BLOCK]

    k_all = jnp.concatenate([kvh[:BLOCK, :BLOCK], proj[:, ATTN_W:ATTN_W + BLOCK],
                             kvh[BLOCK:, :BLOCK]], axis=0)
    v_all = jnp.concatenate([kvh[:BLOCK, BLOCK:], proj[:, ATTN_W + BLOCK:ATTN_W + 2 * BLOCK],
                             kvh[BLOCK:, BLOCK:]], axis=0)
    kn = k_all * lax.rsqrt(_group_ms(k_all, gm_k) + EPS) * kg_ref[...]

    su = _gelu(proj[:, ATTN_W + 2 * BLOCK:ATTN_W + 2 * BLOCK + SGU_W])
    sv = _gelu(proj[:, ATTN_W + 2 * BLOCK + SGU_W:])
    svn = (sv * lax.rsqrt(_group_ms(sv, gm) + EPS) * vg_ref[...]).astype(BF16)
    low = lax.broadcasted_iota(jnp.int32, (1, BLOCK * nb), 1) % BLOCK < CH_B
    slabs = []
    for j in range(SGU_W // BLOCK):
        rhs = jnp.concatenate([svn[n * BLOCK:(n + 1) * BLOCK, j * BLOCK:(j + 1) * BLOCK]
                               for n in range(nb)], axis=1)
        a = jnp.dot(ws_ref[2 * j], rhs, preferred_element_type=F32)
        b = jnp.dot(ws_ref[2 * j + 1], rhs, preferred_element_type=F32)
        slabs.append(jnp.where(low, a, b))
    mixed = jnp.concatenate(
        [jnp.concatenate([slabs[j][:, n * BLOCK:(n + 1) * BLOCK] for j in range(SGU_W // BLOCK)],
                         axis=1) + bs_ref[...] for n in range(nb)], axis=0)
    sgu_out = jnp.dot((su * mixed).astype(BF16), wout_ref[ATTN_W:, :], preferred_element_type=F32)

    q_t = proj[:, :ATTN_W].T
    qt = []
    for h in range(N_HEADS_A):
        qh = q_t[h * HEAD_DIM:(h + 1) * HEAD_DIM]
        qt.append((qh * lax.rsqrt(jnp.mean(qh * qh, axis=0, keepdims=True) + EPS)).astype(BF16))
    qt = jnp.concatenate(qt, axis=0)
    vt = v_all.T.astype(BF16)
    knb = kn.astype(BF16)
    zeros_q = jnp.zeros((HEAD_DIM, GQA * BLOCK), BF16)
    ones_rows = jnp.ones((2 * SUBLANES, 3 * BLOCK), BF16)
    units = [(n, hk) for n in range(nb) for hk in range(N_KV_A)]
    sks = [jnp.concatenate([jnp.full((1, BLOCK), sink_ref[hk * GQA + g] * LOG2E, F32)
                            for g in range(GQA)], axis=1) for hk in range(N_KV_A)]
    scores = []
    for n, hk in units:
        blk = i * nb + n
        variant = jnp.where(blk == 0, 1, 0) + jnp.where(blk == nblocks - 1, 2, 0)
        qh = jnp.concatenate(
            [qt[(hk * GQA + g) * HEAD_DIM:(hk * GQA + g + 1) * HEAD_DIM, n * BLOCK:(n + 1) * BLOCK]
             for g in range(GQA)], axis=1)
        qz = jnp.concatenate([qh, zeros_q] if hk == 0 else [zeros_q, qh], axis=0)
        s = jnp.dot(knb[n * BLOCK:n * BLOCK + 3 * BLOCK, :], qz,
                    preferred_element_type=F32)
        scores.append(s + jnp.concatenate([bias_ref[variant, hk * GQA + g] for g in range(GQA)], axis=1))
    probs = []
    for (n, hk), s in zip(units, scores):
        m = jnp.maximum(jnp.max(s, axis=0, keepdims=True), sks[hk])
        probs.append((jnp.exp2(s - m).astype(BF16), jnp.exp2(sks[hk] - m)))
    outs = []
    for (n, hk), (p, psink) in zip(units, probs):
        lhs = jnp.concatenate([vt[hk * HEAD_DIM:(hk + 1) * HEAD_DIM, n * BLOCK:n * BLOCK + 3 * BLOCK],
                               ones_rows], axis=0)
        pv = jnp.dot(lhs, p, preferred_element_type=F32)
        outs.append(pv[:HEAD_DIM] / (pv[HEAD_DIM:HEAD_DIM + 1] + psink))
    out_cols = []
    for n in range(nb):
        out_cols.append(jnp.concatenate(
            [outs[n * N_KV_A + hk][:, g * BLOCK:(g + 1) * BLOCK]
             for hk in range(N_KV_A) for g in range(GQA)], axis=0))
    attn = jnp.concatenate(out_cols, axis=1).T.astype(BF16)
    o_ref[...] = x + sgu_out + jnp.dot(attn, wout_ref[:ATTN_W, :], preferred_element_type=F32)


def _ab_layer(x, sink, gain, win, wout, gm, kg, bias_tab, vg, ws, bs, T):
    B, L, D = x.shape
    hb = T // BLOCK
    last = L // BLOCK - 1
    kern = functools.partial(_ab_kernel, T=T)
    consts = [gain, win, wout, gm, kg, bias_tab, vg, ws, bs]
    return pl.pallas_call(
        kern,
        grid=(B, L // T),
        in_specs=[pl.BlockSpec(memory_space=pltpu.SMEM),
                  pl.BlockSpec((None, T, D), lambda b, i: (b, i, 0)),
                  pl.BlockSpec((None, BLOCK, D), lambda b, i: (b, jnp.maximum(i * hb - 1, 0), 0)),
                  pl.BlockSpec((None, BLOCK, D), lambda b, i: (b, jnp.minimum((i + 1) * hb, last), 0)),
                  ] + [_const_spec(a.shape) for a in consts],
        out_specs=pl.BlockSpec((None, T, D), lambda b, i: (b, i, 0)),
        out_shape=jax.ShapeDtypeStruct(x.shape, F32),
        compiler_params=_params(2),
        name="attn_sgu_mixer",
    )(sink, x, x, x, *consts)


def _prepare(p):
    depth = p["mix_norm"].shape[0]
    q = dict(depth=depth)
    q["mix_norm"] = [p["mix_norm"][l][None, :] for l in range(depth)]
    q["ffn_norm"] = [p["ffn_norm"][l][None, :] for l in range(depth)]
    q["ffn_wup"] = [p["ffn_w_up"][l].astype(BF16) for l in range(depth)]
    q["ffn_cw"] = [p["ffn_conv_w"][l] for l in range(depth)]
    q["ffn_cb"] = [p["ffn_conv_b"][l][None, :] for l in range(depth)]
    q["ffn_wdn"] = [p["ffn_w_down"][l].astype(BF16) for l in range(depth)]

    n_even = p["ab_w_in"].shape[0]
    q["ab_win"] = [p["ab_w_in"][i].astype(BF16) for i in range(n_even)]
    q["ab_wout"] = [p["ab_w_out"][i].astype(BF16) for i in range(n_even)]
    q["kg"] = [jnp.tile(p["k_norm"][i] * p["q_norm"][i] * (HEAD_DIM ** -0.5 * LOG2E), N_KV_A)[None, :]
               for i in range(n_even)]
    q["sink"] = [p["attn_sink"][i].astype(F32) for i in range(n_even)]
    q["vg"] = [p["sgu_v_norm"][i].reshape(1, SGU_W) for i in range(n_even)]
    q["ws"] = [p["sgu_w"][i].astype(BF16) for i in range(n_even)]
    q["bs"] = [jnp.repeat(p["sgu_b"][i].T, CH_B, axis=1) for i in range(n_even)]
    grp = np.arange(ATTN_W) // HEAD_DIM
    q["gm"] = jnp.asarray((grp[:, None] == grp[None, :]) / float(HEAD_DIM), dtype=BF16)

    n_odd = p["hy_w_in"].shape[0]
    q["hy_win"] = [p["hy_w_in"][i].astype(BF16) for i in range(n_odd)]
    q["hy_cw"] = [p["hy_conv_w"][i] for i in range(n_odd)]
    q["hy_cb"] = [p["hy_conv_b"][i][None, :] for i in range(n_odd)]
    q["hy_d"] = [p["hy_d"][i][:, None].astype(F32) for i in range(n_odd)]
    q["hy_wout"] = [p["hy_w_out"][i].astype(BF16) for i in range(n_odd)]
    slot = np.arange(LANES) % FILTER_WIDTH
    fr = np.linspace(1e-4, FILTER_BANDS - 1, FILTER_BANDS).astype(np.float32)
    frl = np.where((slot >= 1) & (slot <= 2 * FILTER_BANDS), fr[(slot - 1) % FILTER_BANDS], 0.0)
    ph = np.where((slot > FILTER_BANDS) & (slot <= 2 * FILTER_BANDS), 0.5 * np.pi, 0.0)
    q["frl"] = jnp.asarray(frl[None, :], dtype=F32)
    q["ph"] = jnp.asarray(ph[None, :], dtype=F32)
    q["delta"] = jnp.abs(jnp.linspace(MIN_DECAY, MAX_DECAY, D_MODEL, dtype=F32))[None, :]

    def twice(w):
        z = jnp.zeros_like(w)
        return jnp.concatenate([jnp.concatenate([w, z], axis=1), jnp.concatenate([z, w], axis=1)], axis=0)

    filt = []
    for i in range(n_odd):
        w1p = jnp.zeros((FILTER_WIDTH, FILTER_WIDTH), F32).at[:FILTER_EMB].set(p["hy_f_w1"][i])
        row = lambda a: jnp.tile(a[None, :].astype(F32), (1, 2))
        filt.append((twice(w1p), row(p["hy_f_b1"][i]), row(p["hy_f_freq1"][i]),
                     twice(p["hy_f_w2"][i]), row(p["hy_f_b2"][i]), row(p["hy_f_freq2"][i]),
                     twice(p["hy_f_w3"][i]), row(p["hy_f_b3"][i]), row(p["hy_f_freq3"][i]),
                     twice(p["hy_f_wout"][i])))
    q["filt"] = filt
    q["bias_tab"] = _bias_table(p["rel_bias"])
    return q


def _hyena_conv(x, q, l, consts, T):
    i = l // 2
    L = x.shape[1]
    hf, hb = _filter(L, q["frl"], q["ph"], *q["filt"][i], q["delta"])
    kf = _filter_spectrum(hf, hb, q["hy_d"][i], consts)
    x0, zt = _hy_in(x, q["mix_norm"][l], q["hy_win"][i], q["hy_cw"][i], q["hy_cb"][i], T)
    return x0, _fftconv(zt, kf, consts)


def _trunk(x, q, T=512):
    B, L, D = x.shape
    consts = _dft_consts(L)
    for l in range(q["depth"]):
        i = l // 2
        ffn = (q["ffn_norm"][l], q["ffn_wup"][l], q["ffn_cw"][l], q["ffn_cb"][l], q["ffn_wdn"][l])
        if l % 2 == 0:
            x = _ab_layer(x, q["sink"][i], q["mix_norm"][l], q["ab_win"][i], q["ab_wout"][i],
                          q["gm"], q["kg"][i], q["bias_tab"], q["vg"][i], q["ws"][i],
                          q["bs"][i], T)
            x = _ffn(x, *ffn, T)
        else:
            x0, yt = _hyena_conv(x, q, l, consts, T)
            x = _hy_ffn(x, x0, yt, q["hy_wout"][i], *ffn, T)
    return x


def kernel(x_prompt, x_sample, rel_bias, mix_norm, ffn_norm, ab_w_in, q_norm, k_norm, attn_sink, sgu_v_norm, sgu_w, sgu_b, ab_w_out, hy_w_in, hy_conv_w, hy_conv_b, hy_f_w1, hy_f_b1, hy_f_freq1, hy_f_w2, hy_f_b2, hy_f_freq2, hy_f_w3, hy_f_b3, hy_f_freq3, hy_f_wout, hy_d, hy_w_out, ffn_w_up, ffn_conv_w, ffn_conv_b, ffn_w_down):
    p = dict(rel_bias=rel_bias, mix_norm=mix_norm, ffn_norm=ffn_norm, ab_w_in=ab_w_in,
             q_norm=q_norm, k_norm=k_norm, attn_sink=attn_sink, sgu_v_norm=sgu_v_norm,
             sgu_w=sgu_w, sgu_b=sgu_b, ab_w_out=ab_w_out, hy_w_in=hy_w_in, hy_conv_w=hy_conv_w,
             hy_conv_b=hy_conv_b, hy_f_w1=hy_f_w1, hy_f_b1=hy_f_b1, hy_f_freq1=hy_f_freq1,
             hy_f_w2=hy_f_w2, hy_f_b2=hy_f_b2, hy_f_freq2=hy_f_freq2, hy_f_w3=hy_f_w3,
             hy_f_b3=hy_f_b3, hy_f_freq3=hy_f_freq3, hy_f_wout=hy_f_wout, hy_d=hy_d,
             hy_w_out=hy_w_out, ffn_w_up=ffn_w_up, ffn_conv_w=ffn_conv_w, ffn_conv_b=ffn_conv_b,
             ffn_w_down=ffn_w_down)
    q = _prepare(p)
    return (_trunk(x_prompt, q), _trunk(x_sample, q))
```

```python
import functools
import math

import numpy as np
import jax
import jax.numpy as jnp
from jax import lax
from jax.experimental import pallas as pl
from jax.experimental.pallas import tpu as pltpu

F32 = jnp.float32
BF16 = jnp.bfloat16

D_MODEL = 1024
HEAD_DIM = 64
N_HEADS_A = 8
N_KV_A = 2
GQA = 4
ATTN_W = 512
WINDOW = 128
BLOCK = 128
N_BUCKETS = 32
MAX_DISTANCE = 128
SGU_W = 512
N_GROUPS_B = 8
CH_B = 64
FILTER_EMB = 33
FILTER_BANDS = 16
FILTER_WIDTH = 64
DECAY_TARGET = 1e-2
MIN_DECAY = math.log(DECAY_TARGET) / 1.5
MAX_DECAY = math.log(DECAY_TARGET) / 0.3
FFN_HIDDEN = 2816
EPS = 1e-6
NEG = -1e30

LANES = 128
SUBLANES = 8
TOKENS_FFN = 512
TOKENS_MIXER = 1024
FFN_CHUNK = 256
HY_CHUNK = 256
HALO = 8
ROW_BLOCK = 64
FFN_SLAB_SETS = 2
FFT_N2 = LANES
FFT_ROWS = 4096
VMEM_LIMIT = 56 * 1024 * 1024


def _gelu(x):
    t = jnp.tanh(x * (0.7978845608028654 + (0.7978845608028654 * 0.044715) * (x * x)))
    hx = 0.5 * x
    return hx + hx * t


def _rms(x, gain):
    return x * lax.rsqrt(jnp.mean(x * x, axis=-1, keepdims=True) + EPS) * gain


def _const_spec(shape):
    nd = len(shape)
    return pl.BlockSpec(shape, lambda *_: (0,) * nd, pipeline_mode=pl.Buffered(1))


def _params(n_axes):
    return pltpu.CompilerParams(dimension_semantics=("arbitrary",) * n_axes,
                                vmem_limit_bytes=VMEM_LIMIT)


def _park_slabs(s_ref, base, h):
    for s in range(h.shape[1] // LANES):
        s_ref[base + s] = h[:, s * LANES:(s + 1) * LANES]


def _conv3_slab(s_ref, slab, w, b, r0, rows):
    lo = HALO + r0
    return (s_ref[slab, lo - 1:lo - 1 + rows, :] * w[0:1]
            + s_ref[slab, lo:lo + rows, :] * w[1:2]
            + s_ref[slab, lo + 1:lo + 1 + rows, :] * w[2:3]
            + b)


def _fill_normed_ext(xe_ref, x, xp, xn, gain, rows, first, last):
    pm = jnp.where(first, 0.0, 1.0)
    nm = jnp.where(last, 0.0, 1.0)
    xe_ref[0:HALO, :] = (_rms(xp, gain) * pm).astype(BF16)
    xe_ref[HALO:HALO + rows, :] = _rms(x, gain).astype(BF16)
    xe_ref[HALO + rows:HALO + rows + HALO, :] = (_rms(xn, gain) * nm).astype(BF16)


def _halo_specs(T, L, D):
    hb = T // HALO
    last = L // HALO - 1
    return [
        pl.BlockSpec((None, T, D), lambda b, i: (b, i, 0)),
        pl.BlockSpec((None, HALO, D), lambda b, i: (b, jnp.maximum(i * hb - 1, 0), 0)),
        pl.BlockSpec((None, HALO, D), lambda b, i: (b, jnp.minimum((i + 1) * hb, last), 0)),
    ]


def _ffn_body(x, xp, xn, g_ref, wup_ref, cw_ref, cb_ref, wdn_ref, o_ref, xe_ref, a_ref, s_ref,
              T, nchunk):
    i = pl.program_id(1)
    _fill_normed_ext(xe_ref, x, xp, xn, g_ref[...], T, i == 0, i == pl.num_programs(1) - 1)
    xe = xe_ref[...]
    ns = FFN_CHUNK // LANES
    for j in range(nchunk):
        base = (j % (s_ref.shape[0] // (2 * ns))) * 2 * ns
        for part in range(2):
            lo = part * FFN_HIDDEN + j * FFN_CHUNK
            _park_slabs(s_ref, base + part * ns,
                        jnp.dot(xe, wup_ref[:, lo:lo + FFN_CHUNK], preferred_element_type=F32))
        for s in range(ns):
            col = j * FFN_CHUNK + s * LANES
            wg, bg = cw_ref[:, col:col + LANES], cb_ref[:, col:col + LANES]
            wu = cw_ref[:, FFN_HIDDEN + col:FFN_HIDDEN + col + LANES]
            bu = cb_ref[:, FFN_HIDDEN + col:FFN_HIDDEN + col + LANES]
            for r0 in range(0, T, ROW_BLOCK):
                g = _conv3_slab(s_ref, base + s, wg, bg, r0, ROW_BLOCK)
                u = _conv3_slab(s_ref, base + ns + s, wu, bu, r0, ROW_BLOCK)
                a_ref[r0:r0 + ROW_BLOCK, col:col + LANES] = (_gelu(g) * u).astype(BF16)
    o_ref[...] = x + jnp.dot(a_ref[...], wdn_ref[...], preferred_element_type=F32)


def _ffn_kernel(x_ref, xp_ref, xn_ref, g_ref, wup_ref, cw_ref, cb_ref, wdn_ref, o_ref,
                xe_ref, a_ref, s_ref, *, T, nchunk):
    _ffn_body(x_ref[...], xp_ref[...], xn_ref[...], g_ref, wup_ref, cw_ref, cb_ref, wdn_ref,
              o_ref, xe_ref, a_ref, s_ref, T, nchunk)


def _ffn_scratch(T, D):
    return [pltpu.VMEM((T + 2 * HALO, D), BF16), pltpu.VMEM((T, FFN_HIDDEN), BF16),
            pltpu.VMEM((FFN_SLAB_SETS * 2 * FFN_CHUNK // LANES, T + 2 * HALO, LANES), F32)]


def _hy_ffn_kernel(x_ref, xp_ref, xn_ref, x0_ref, x0p_ref, x0n_ref, yt_ref, ytp_ref, ytn_ref,
                   wo_ref, g_ref, wup_ref, cw_ref, cb_ref, wdn_ref, o_ref,
                   xe_ref, a_ref, s_ref, m_ref, *, T, nchunk):
    m_ref[0:HALO, :] = (x0p_ref[...] * ytp_ref[...].T[BLOCK - HALO:, :]).astype(BF16)
    for n in range(T // BLOCK):
        lo = HALO + n * BLOCK
        m_ref[lo:lo + BLOCK, :] = (x0_ref[n * BLOCK:(n + 1) * BLOCK, :] * yt_ref[n].T).astype(BF16)
    m_ref[HALO + T:, :] = (x0n_ref[...] * ytn_ref[...].T[:HALO, :]).astype(BF16)
    mix = jnp.dot(m_ref[...], wo_ref[...], preferred_element_type=F32)
    _ffn_body(x_ref[...] + mix[HALO:HALO + T], xp_ref[...] + mix[:HALO], xn_ref[...] + mix[HALO + T:],
              g_ref, wup_ref, cw_ref, cb_ref, wdn_ref, o_ref, xe_ref, a_ref, s_ref, T, nchunk)


def _hy_ffn(x, x0, yt, wo, gain, wup_r, cw_r, cb_r, wdn, T):
    B, L, D = x.shape
    nchunk = FFN_HIDDEN // FFN_CHUNK
    nb = T // BLOCK
    last = L // BLOCK - 1
    kern = functools.partial(_hy_ffn_kernel, T=T, nchunk=nchunk)
    return pl.pallas_call(
        kern,
        grid=(B, L // T),
        in_specs=_halo_specs(T, L, D) + _halo_specs(T, L, D) + [
            pl.BlockSpec((None, nb, D, LANES), lambda b, i: (b, i, 0, 0)),
            pl.BlockSpec((None, None, D, LANES), lambda b, i: (b, jnp.maximum(i * nb - 1, 0), 0, 0)),
            pl.BlockSpec((None, None, D, LANES), lambda b, i: (b, jnp.minimum((i + 1) * nb, last), 0, 0)),
            _const_spec(wo.shape), _const_spec((1, D)), _const_spec(wup_r.shape),
            _const_spec(cw_r.shape), _const_spec(cb_r.shape), _const_spec(wdn.shape)],
        out_specs=pl.BlockSpec((None, T, D), lambda b, i: (b, i, 0)),
        out_shape=jax.ShapeDtypeStruct(x.shape, F32),
        scratch_shapes=_ffn_scratch(T, D) + [pltpu.VMEM((T + 2 * HALO, D), BF16)],
        compiler_params=_params(2),
        name="hyena_out_conv_ffn",
    )(x, x, x, x0, x0, x0, yt, yt, yt, wo, gain, wup_r, cw_r, cb_r, wdn)


def _ffn(x, gain, wup_r, cw_r, cb_r, wdn, T):
    B, L, D = x.shape
    nchunk = FFN_HIDDEN // FFN_CHUNK
    kern = functools.partial(_ffn_kernel, T=T, nchunk=nchunk)
    return pl.pallas_call(
        kern,
        grid=(B, L // T),
        in_specs=_halo_specs(T, L, D) + [
            _const_spec((1, D)), _const_spec(wup_r.shape), _const_spec(cw_r.shape),
            _const_spec(cb_r.shape), _const_spec(wdn.shape)],
        out_specs=pl.BlockSpec((None, T, D), lambda b, i: (b, i, 0)),
        out_shape=jax.ShapeDtypeStruct(x.shape, F32),
        scratch_shapes=_ffn_scratch(T, D),
        compiler_params=_params(2),
        name="conv_ffn",
    )(x, x, x, gain, wup_r, cw_r, cb_r, wdn)


def _hy_in_kernel(x_ref, xp_ref, xn_ref, g_ref, win_ref, cw_ref, cb_ref, x0_ref, zt_ref,
                  xe_ref, s_ref, *, T, nchunk):
    i = pl.program_id(1)
    _fill_normed_ext(xe_ref, x_ref[...], xp_ref[...], xn_ref[...], g_ref[...], T,
                     i == 0, i == pl.num_programs(1) - 1)
    xe = xe_ref[...]
    ns = HY_CHUNK // LANES
    for j in range(nchunk):
        base = (j % 2) * 3 * ns
        for k in range(3):
            lo = k * D_MODEL + j * HY_CHUNK
            _park_slabs(s_ref, base + k * ns,
                        jnp.dot(xe, win_ref[:, lo:lo + HY_CHUNK], preferred_element_type=F32))
        for s in range(ns):
            col = j * HY_CHUNK + s * LANES
            w = [cw_ref[:, k * D_MODEL + col:k * D_MODEL + col + LANES] for k in range(3)]
            b = [cb_ref[:, k * D_MODEL + col:k * D_MODEL + col + LANES] for k in range(3)]
            for n in range(T // BLOCK):
                r0 = n * BLOCK
                x0_ref[r0:r0 + BLOCK, col:col + LANES] = _conv3_slab(s_ref, base + s, w[0], b[0], r0, BLOCK)
                z = (_conv3_slab(s_ref, base + ns + s, w[1], b[1], r0, BLOCK)
                     * _conv3_slab(s_ref, base + 2 * ns + s, w[2], b[2], r0, BLOCK))
                zt_ref[n, col:col + LANES, :] = z.T


def _hy_in(x, gain, win_r, cw_r, cb_r, T):
    B, L, D = x.shape
    nchunk = D // HY_CHUNK
    kern = functools.partial(_hy_in_kernel, T=T, nchunk=nchunk)
    return pl.pallas_call(
        kern,
        grid=(B, L // T),
        in_specs=_halo_specs(T, L, D) + [
            _const_spec((1, D)), _const_spec(win_r.shape), _const_spec(cw_r.shape),
            _const_spec(cb_r.shape)],
        out_specs=[pl.BlockSpec((None, T, D), lambda b, i: (b, i, 0)),
                   pl.BlockSpec((None, T // BLOCK, D, LANES), lambda b, i: (b, i, 0, 0))],
        out_shape=[jax.ShapeDtypeStruct(x.shape, F32),
                   jax.ShapeDtypeStruct((B, L // BLOCK, D, LANES), F32)],
        scratch_shapes=[pltpu.VMEM((T + 2 * HALO, D), BF16),
                        pltpu.VMEM((2 * 3 * HY_CHUNK // LANES, T + 2 * HALO, LANES), F32)],
        compiler_params=_params(2),
        name="hyena_in",
    )(x, x, x, gain, win_r, cw_r, cb_r)


def _dot_f32(a, b):
    return jnp.dot(a, b, preferred_element_type=F32, precision=lax.Precision.HIGHEST)


def _filter_kernel(frl_ref, ph_ref, w1_ref, b1_ref, f1_ref, w2_ref, b2_ref, f2_ref, w3_ref, b3_ref,
                   f3_ref, wout_ref, delta_ref, hf_ref, hb_ref, *, L, R):
    hr = R // 2
    ja = pl.program_id(0) * R + lax.broadcasted_iota(jnp.int32, (hr, 1), 0)
    jb = ja + hr
    lane = lax.broadcasted_iota(jnp.int32, (hr, LANES), 1)
    j = jnp.where(lane < FILTER_WIDTH, ja, jb).astype(F32)
    t = j / float(L - 1)
    w = (2.0 * math.pi) * j / float(L)
    feats = jnp.where(lane % FILTER_WIDTH == 0, t, jnp.cos(w * frl_ref[...] + ph_ref[...]))
    h = jnp.sin(f1_ref[...] * (_dot_f32(feats, w1_ref[...]) + b1_ref[...]))
    h = jnp.sin(f2_ref[...] * (_dot_f32(h, w2_ref[...]) + b2_ref[...]))
    h = jnp.sin(f3_ref[...] * (_dot_f32(h, w3_ref[...]) + b3_ref[...]))
    h = _dot_f32(h, wout_ref[...])
    for half, jidx in enumerate((ja, jb)):
        decay = jnp.exp(-(jidx.astype(F32) / float(L - 1)) * delta_ref[...])
        base = half * 2 * D_MODEL
        hf = h[:, base:base + D_MODEL] * decay
        hb = jnp.where(jidx == 0, 0.0, h[:, base + D_MODEL:base + 2 * D_MODEL] * decay)
        for n in range(hr // BLOCK):
            blk = half * (hr // BLOCK) + n
            hf_ref[blk] = hf[n * BLOCK:(n + 1) * BLOCK, :].T
            hb_ref[blk] = hb[n * BLOCK:(n + 1) * BLOCK, :].T


def _filter(L, frl, ph, w1, b1, f1, w2, b2, f2, w3, b3, f3, wout, delta):
    R = 512
    kern = functools.partial(_filter_kernel, L=L, R=R)
    consts = [frl, ph, w1, b1, f1, w2, b2, f2, w3, b3, f3, wout, delta]
    out_spec = pl.BlockSpec((R // BLOCK, D_MODEL, LANES), lambda i: (i, 0, 0))
    shape = jax.ShapeDtypeStruct((L // BLOCK, D_MODEL, LANES), F32)
    return pl.pallas_call(
        kern,
        grid=(L // R,),
        in_specs=[_const_spec(a.shape) for a in consts],
        out_specs=[out_spec, out_spec],
        out_shape=[shape, shape],
        compiler_params=_params(1),
        name="hyena_filter",
    )(*consts)


def _dft_consts(L):
    N = 2 * L
    N2 = FFT_N2
    N1 = N // N2
    h = N1 // 2
    a1 = -2.0 * np.pi * np.outer(np.arange(N1), np.arange(N1)) / N1
    f1r, f1i = np.cos(a1), np.sin(a1)
    a2 = -2.0 * np.pi * np.outer(np.arange(N2), np.arange(N2)) / N2
    f2r, f2i = np.cos(a2), np.sin(a2)
    at = -2.0 * np.pi * np.outer(np.arange(N1), np.arange(N2)) / N
    m1 = np.block([[f1r[:, :h], -f1i[:, :h]], [f1i[:, :h], f1r[:, :h]]])
    m1f = np.concatenate([f1r[:, :h], f1i[:, :h]], axis=0)
    m3 = np.block([[f1r[:h], f1i[:h]], [-f1i[:h], f1r[:h]]])
    c = lambda a: jnp.asarray(a, dtype=BF16)
    f = lambda a: jnp.asarray(a, dtype=F32)
    return dict(N1=N1, m1=c(m1), m1f=c(m1f), m3=c(m3),
                r2f=c(np.block([[f2r, f2i], [-f2i, f2r]])),
                r2i=c(np.block([[f2r, -f2i], [f2i, f2r]])),
                twr=f(np.cos(at)), twi=f(np.sin(at)))


def _cmul(ar, ai, br, bi):
    return ar * br - ai * bi, ar * bi + ai * br


def _slab8(ref, idx, c0, n):
    v = ref[idx + (slice(None), slice(c0, c0 + SUBLANES), slice(None))]
    return v.reshape(n * SUBLANES, LANES)


def _chan_rows(s_ref, base, c, n):
    return s_ref[pl.ds(base + c, n, stride=SUBLANES), :]


def _fft_kernel(x_ref, kf_ref, m1_ref, m3_ref, twr_ref, twi_ref, r2f_ref, r2i_ref, o_ref,
                *scratch, N1, nc):
    h = N1 // 2
    h8 = h * SUBLANES
    nsub = nc // SUBLANES
    twr, twi = twr_ref[...], twi_ref[...]
    cols = []
    for s8 in range(nsub):
        sin_ref = scratch[s8]
        sin_ref[0:h8] = _slab8(x_ref, (0,), s8 * SUBLANES, h)
        sin_ref[h8:2 * h8] = _slab8(x_ref, (1,), s8 * SUBLANES, h)
        cols += [jnp.concatenate([_chan_rows(sin_ref, 0, c, h), _chan_rows(sin_ref, h8, c, h)], axis=0)
                 for c in range(SUBLANES)]
    a = jnp.dot(m1_ref[...], jnp.concatenate(cols, axis=1).astype(BF16),
                preferred_element_type=F32)
    lhs = []
    for c in range(nc):
        tr, ti = _cmul(a[:N1, c * LANES:(c + 1) * LANES], a[N1:, c * LANES:(c + 1) * LANES], twr, twi)
        lhs.append(jnp.concatenate([tr, ti], axis=1))
    lhs = jnp.concatenate(lhs, axis=0).astype(BF16)
    X = jnp.dot(lhs, r2f_ref[...], preferred_element_type=F32)
    yr, yi = _cmul(X[:, :LANES], X[:, LANES:],
                   kf_ref[0].reshape(nc * N1, LANES), kf_ref[1].reshape(nc * N1, LANES))
    Bm = jnp.dot(jnp.concatenate([yr, yi], axis=1).astype(BF16), r2i_ref[...],
                 preferred_element_type=F32)
    re_cols, im_cols = [], []
    for c in range(nc):
        br, bi = Bm[c * N1:(c + 1) * N1, :LANES], Bm[c * N1:(c + 1) * N1, LANES:]
        re_cols.append(br * twr + bi * twi)
        im_cols.append(bi * twr - br * twi)
    rhs = jnp.concatenate([jnp.concatenate(re_cols, axis=1),
                           jnp.concatenate(im_cols, axis=1)], axis=0).astype(BF16)
    y = jnp.dot(m3_ref[...], rhs, preferred_element_type=F32)
    for s8 in range(nsub):
        c0 = s8 * SUBLANES
        sout_ref = scratch[nsub + s8]
        for c in range(SUBLANES):
            lo = (c0 + c) * LANES
            sout_ref[pl.ds(c, h, stride=SUBLANES), :] = y[:h, lo:lo + LANES]
            sout_ref[pl.ds(h8 + c, h, stride=SUBLANES), :] = y[h:, lo:lo + LANES]
        o_ref[0, :, c0:c0 + SUBLANES, :] = sout_ref[0:h8].reshape(h, SUBLANES, LANES)
        o_ref[1, :, c0:c0 + SUBLANES, :] = sout_ref[h8:2 * h8].reshape(h, SUBLANES, LANES)


def _fftconv(zt, kf, c):
    B, h, C, _ = zt.shape
    N1 = c["N1"]
    nc = min(FFT_ROWS // N1, C)
    kern = functools.partial(_fft_kernel, N1=N1, nc=nc)
    blk = pl.BlockSpec((2, h, nc, LANES), lambda cb, p: (p, 0, cb, 0))
    consts = [c["m1"], c["m3"], c["twr"], c["twi"], c["r2f"], c["r2i"]]
    return pl.pallas_call(
        kern,
        grid=(C // nc, B // 2),
        in_specs=[blk, pl.BlockSpec((2, nc, N1, LANES), lambda cb, p: (0, cb, 0, 0))]
        + [_const_spec(a.shape) for a in consts],
        out_specs=blk,
        out_shape=jax.ShapeDtypeStruct(zt.shape, F32),
        scratch_shapes=[pltpu.VMEM((N1 * SUBLANES, LANES), F32)] * (2 * nc // SUBLANES),
        compiler_params=_params(2),
        name="fft_conv",
    )(zt, kf, *consts)


def _fspec_kernel(hf_ref, hb_ref, d_ref, m1f_ref, twr_ref, twi_ref, r2f_ref, o_ref,
                  *scratch, N1, nc, scale):
    h = N1 // 2
    h8 = h * SUBLANES
    twr, twi = twr_ref[...], twi_ref[...]
    cols = []
    for s8 in range(nc // SUBLANES):
        s_ref = scratch[s8]
        s_ref[0:h8] = _slab8(hf_ref, (), s8 * SUBLANES, h)
        s_ref[h8:2 * h8] = _slab8(hb_ref, (), s8 * SUBLANES, h)
        for c in range(SUBLANES):
            cols += [_chan_rows(s_ref, 0, c, h), _chan_rows(s_ref, h8, c, h)]
    a = jnp.dot(m1f_ref[...], jnp.concatenate(cols, axis=1).astype(BF16),
                preferred_element_type=F32)
    lhs = []
    for k in range(2 * nc):
        tr, ti = _cmul(a[:N1, k * LANES:(k + 1) * LANES], a[N1:, k * LANES:(k + 1) * LANES], twr, twi)
        lhs.append(jnp.concatenate([tr, ti], axis=1))
    lhs = jnp.concatenate(lhs, axis=0).astype(BF16)
    X = jnp.dot(lhs, r2f_ref[...], preferred_element_type=F32)
    for c in range(nc):
        xf = X[(2 * c) * N1:(2 * c + 1) * N1]
        xb = X[(2 * c + 1) * N1:(2 * c + 2) * N1]
        d = d_ref[c:c + 1, :]
        o_ref[0, c] = (xf[:, :LANES] + xb[:, :LANES] + d) * scale
        o_ref[1, c] = (xf[:, LANES:] - xb[:, LANES:]) * scale


def _filter_spectrum(hf, hb, d, c):
    h, C, _ = hf.shape
    N1 = c["N1"]
    nc = min(FFT_ROWS // N1, C)
    kern = functools.partial(_fspec_kernel, N1=N1, nc=nc, scale=1.0 / (N1 * FFT_N2))
    blk = pl.BlockSpec((h, nc, LANES), lambda cb: (0, cb, 0))
    consts = [c["m1f"], c["twr"], c["twi"], c["r2f"]]
    return pl.pallas_call(
        kern,
        grid=(C // nc,),
        in_specs=[blk, blk, pl.BlockSpec((nc, 1), lambda cb: (cb, 0))]
        + [_const_spec(a.shape) for a in consts],
        out_specs=pl.BlockSpec((2, nc, N1, LANES), lambda cb: (0, cb, 0, 0)),
        out_shape=jax.ShapeDtypeStruct((2, C, N1, LANES), F32),
        scratch_shapes=[pltpu.VMEM((N1 * SUBLANES, LANES), F32)] * (nc // SUBLANES),
        compiler_params=_params(1),
        name="filter_spectrum",
    )(hf, hb, d, *consts)


def _t5_bucket(rel):
    half = N_BUCKETS // 2
    max_exact = half // 2
    ret = jnp.where(rel > 0, half, 0)
    n = jnp.abs(rel)
    nf = jnp.maximum(n, 1).astype(jnp.float32)
    large = max_exact + (jnp.log(nf / max_exact) / math.log(MAX_DISTANCE / max_exact)
                         * (half - max_exact)).astype(jnp.int32)
    large = jnp.minimum(large, half - 1)
    return ret + jnp.where(n < max_exact, n, large)


LOG2E = 1.4426950408889634


def _bias_kernel(rb_ref, bucket_ref, rel_ref, o_ref):
    bucket = bucket_ref[...]
    rel = rel_ref[...]
    key = lax.broadcasted_iota(jnp.int32, bucket.shape, 0)
    for h in range(N_HEADS_A):
        acc = jnp.zeros(bucket.shape, F32)
        for b in range(N_BUCKETS):
            acc = jnp.where(bucket == b, rb_ref[b, h], acc)
        acc = acc * LOG2E
        for v in range(4):
            ok = jnp.abs(rel) <= WINDOW
            if v & 1:
                ok = ok & (key >= BLOCK)
            if v & 2:
                ok = ok & (key < 2 * BLOCK)
            o_ref[v, h] = jnp.where(ok, acc, NEG)


def _bias_table(rel_bias):
    rel = (jnp.arange(3 * BLOCK)[:, None] - BLOCK) - jnp.arange(BLOCK)[None, :]
    rel = rel.astype(jnp.int32)
    bucket = _t5_bucket(rel).astype(jnp.int32)
    return pl.pallas_call(
        _bias_kernel,
        in_specs=[pl.BlockSpec(memory_space=pltpu.SMEM),
                  pl.BlockSpec(memory_space=pltpu.VMEM), pl.BlockSpec(memory_space=pltpu.VMEM)],
        out_specs=pl.BlockSpec(memory_space=pltpu.VMEM),
        out_shape=jax.ShapeDtypeStruct((4, N_HEADS_A, 3 * BLOCK, BLOCK), F32),
        name="rel_bias_table",
    )(rel_bias.astype(F32), bucket, rel)


def _group_ms(v, gmat):
    return jnp.dot((v * v).astype(BF16), gmat, preferred_element_type=F32)


def _ab_kernel(sink_ref, x_ref, xp_ref, xn_ref, g_ref, win_ref, wout_ref, gm_ref,
               kg_ref, bias_ref, vg_ref, ws_ref, bs_ref, o_ref, *, T):
    i = pl.program_id(1)
    nb = T // BLOCK
    nblocks = pl.num_programs(1) * nb
    gain = g_ref[...]
    x = x_ref[...]
    proj = jnp.dot(_rms(x, gain).astype(BF16), win_ref[...], preferred_element_type=F32)
    xh = jnp.concatenate([_rms(xp_ref[...], gain), _rms(xn_ref[...], gain)], axis=0)
    kvh = jnp.dot(xh.astype(BF16), win_ref[:, ATTN_W:ATTN_W + 2 * BLOCK],
                  preferred_element_type=F32)
    gm = gm_ref[...]
    gm_k = gm[:BLOCK, :BLOCK]

    k_all = jnp.concatenate([kvh[:BLOCK, :BLOCK], proj[:, ATTN_W:ATTN_W + BLOCK],
                             kvh[BLOCK:, :BLOCK]], axis=0)
    v_all = jnp.concatenate([kvh[:BLOCK, BLOCK:], proj[:, ATTN_W + BLOCK:ATTN_W + 2 * BLOCK],
                             kvh[BLOCK:, BLOCK:]], axis=0)
    kn = k_all * lax.rsqrt(_group_ms(k_all, gm_k) + EPS) * kg_ref[...]

    su = _gelu(proj[:, ATTN_W + 2 * BLOCK:ATTN_W + 2 * BLOCK + SGU_W])
    sv = _gelu(proj[:, ATTN_W + 2 * BLOCK + SGU_W:])
    svn = (sv * lax.rsqrt(_group_ms(sv, gm) + EPS) * vg_ref[...]).astype(BF16)
    low = lax.broadcasted_iota(jnp.int32, (1, BLOCK * nb), 1) % BLOCK < CH_B
    slabs = []
    for j in range(SGU_W // BLOCK):
        rhs = jnp.concatenate([svn[n * BLOCK:(n + 1) * BLOCK, j * BLOCK:(j + 1) * BLOCK]
                               for n in range(nb)], axis=1)
        a = jnp.dot(ws_ref[2 * j], rhs, preferred_element_type=F32)
        b = jnp.dot(ws_ref[2 * j + 1], rhs, preferred_element_type=F32)
        slabs.append(jnp.where(low, a, b))
    mixed = jnp.concatenate(
        [jnp.concatenate([slabs[j][:, n * BLOCK:(n + 1) * BLOCK] for j in range(SGU_W // BLOCK)],
                         axis=1) + bs_ref[...] for n in range(nb)], axis=0)
    sgu_out = jnp.dot((su * mixed).astype(BF16), wout_ref[ATTN_W:, :], preferred_element_type=F32)

    q_t = proj[:, :ATTN_W].T
    qt = []
    for h in range(N_HEADS_A):
        qh = q_t[h * HEAD_DIM:(h + 1) * HEAD_DIM]
        qt.append((qh * lax.rsqrt(jnp.mean(qh * qh, axis=0, keepdims=True) + EPS)).astype(BF16))
    qt = jnp.concatenate(qt, axis=0)
    vt = v_all.T.astype(BF16)
    knb = kn.astype(BF16)
    zeros_q = jnp.zeros((HEAD_DIM, GQA * BLOCK), BF16)
    ones_rows = jnp.ones((2 * SUBLANES, 3 * BLOCK), BF16)
    units = [(n, hk) for n in range(nb) for hk in range(N_KV_A)]
    sks = [jnp.concatenate([jnp.full((1, BLOCK), sink_ref[hk * GQA + g] * LOG2E, F32)
                            for g in range(GQA)], axis=1) for hk in range(N_KV_A)]
    scores = []
    for n, hk in units:
        blk = i * nb + n
        variant = jnp.where(blk == 0, 1, 0) + jnp.where(blk == nblocks - 1, 2, 0)
        qh = jnp.concatenate(
            [qt[(hk * GQA + g) * HEAD_DIM:(hk * GQA + g + 1) * HEAD_DIM, n * BLOCK:(n + 1) * BLOCK]
             for g in range(GQA)], axis=1)
        qz = jnp.concatenate([qh, zeros_q] if hk == 0 else [zeros_q, qh], axis=0)
        s = jnp.dot(knb[n * BLOCK:n * BLOCK + 3 * BLOCK, :], qz,
                    preferred_element_type=F32)
        scores.append(s + jnp.concatenate([bias_ref[variant, hk * GQA + g] for g in range(GQA)], axis=1))
    probs = []
    for (n, hk), s in zip(units, scores):
        m = jnp.maximum(jnp.max(s, axis=0, keepdims=True), sks[hk])
        probs.append((jnp.exp2(s - m).astype(BF16), jnp.exp2(sks[hk] - m)))
    outs = []
    for (n, hk), (p, psink) in zip(units, probs):
        lhs = jnp.concatenate([vt[hk * HEAD_DIM:(hk + 1) * HEAD_DIM, n * BLOCK:n * BLOCK + 3 * BLOCK],
                               ones_rows], axis=0)
        pv = jnp.dot(lhs, p, preferred_element_type=F32)
        outs.append(pv[:HEAD_DIM] / (pv[HEAD_DIM:HEAD_DIM + 1] + psink))
    out_cols = []
    for n in range(nb):
        out_cols.append(jnp.concatenate(
            [outs[n * N_KV_A + hk][:, g * BLOCK:(g + 1) * BLOCK]
             for hk in range(N_KV_A) for g in range(GQA)], axis=0))
    attn = jnp.concatenate(out_cols, axis=1).T.astype(BF16)
    o_ref[...] = x + sgu_out + jnp.dot(attn, wout_ref[:ATTN_W, :], preferred_element_type=F32)


def _ab_layer(x, sink, gain, win, wout, gm, kg, bias_tab, vg, ws, bs, T):
    B, L, D = x.shape
    hb = T // BLOCK
    last = L // BLOCK - 1
    kern = functools.partial(_ab_kernel, T=T)
    consts = [gain, win, wout, gm, kg, bias_tab, vg, ws, bs]
    return pl.pallas_call(
        kern,
        grid=(B, L // T),
        in_specs=[pl.BlockSpec(memory_space=pltpu.SMEM),
                  pl.BlockSpec((None, T, D), lambda b, i: (b, i, 0)),
                  pl.BlockSpec((None, BLOCK, D), lambda b, i: (b, jnp.maximum(i * hb - 1, 0), 0)),
                  pl.BlockSpec((None, BLOCK, D), lambda b, i: (b, jnp.minimum((i + 1) * hb, last), 0)),
                  ] + [_const_spec(a.shape) for a in consts],
        out_specs=pl.BlockSpec((None, T, D), lambda b, i: (b, i, 0)),
        out_shape=jax.ShapeDtypeStruct(x.shape, F32),
        compiler_params=_params(2),
        name="attn_sgu_mixer",
    )(sink, x, x, x, *consts)


def _prepare(p):
    depth = p["mix_norm"].shape[0]
    q = dict(depth=depth)
    q["mix_norm"] = [p["mix_norm"][l][None, :] for l in range(depth)]
    q["ffn_norm"] = [p["ffn_norm"][l][None, :] for l in range(depth)]
    q["ffn_wup"] = [p["ffn_w_up"][l].astype(BF16) for l in range(depth)]
    q["ffn_cw"] = [p["ffn_conv_w"][l] for l in range(depth)]
    q["ffn_cb"] = [p["ffn_conv_b"][l][None, :] for l in range(depth)]
    q["ffn_wdn"] = [p["ffn_w_down"][l].astype(BF16) for l in range(depth)]

    n_even = p["ab_w_in"].shape[0]
    q["ab_win"] = [p["ab_w_in"][i].astype(BF16) for i in range(n_even)]
    q["ab_wout"] = [p["ab_w_out"][i].astype(BF16) for i in range(n_even)]
    q["kg"] = [jnp.tile(p["k_norm"][i] * p["q_norm"][i] * (HEAD_DIM ** -0.5 * LOG2E), N_KV_A)[None, :]
               for i in range(n_even)]
    q["sink"] = [p["attn_sink"][i].astype(F32) for i in range(n_even)]
    q["vg"] = [p["sgu_v_norm"][i].reshape(1, SGU_W) for i in range(n_even)]
    q["ws"] = [p["sgu_w"][i].astype(BF16) for i in range(n_even)]
    q["bs"] = [jnp.repeat(p["sgu_b"][i].T, CH_B, axis=1) for i in range(n_even)]
    grp = np.arange(ATTN_W) // HEAD_DIM
    q["gm"] = jnp.asarray((grp[:, None] == grp[None, :]) / float(HEAD_DIM), dtype=BF16)

    n_odd = p["hy_w_in"].shape[0]
    q["hy_win"] = [p["hy_w_in"][i].astype(BF16) for i in range(n_odd)]
    q["hy_cw"] = [p["hy_conv_w"][i] for i in range(n_odd)]
    q["hy_cb"] = [p["hy_conv_b"][i][None, :] for i in range(n_odd)]
    q["hy_d"] = [p["hy_d"][i][:, None].astype(F32) for i in range(n_odd)]
    q["hy_wout"] = [p["hy_w_out"][i].astype(BF16) for i in range(n_odd)]
    slot = np.arange(LANES) % FILTER_WIDTH
    fr = np.linspace(1e-4, FILTER_BANDS - 1, FILTER_BANDS).astype(np.float32)
    frl = np.where((slot >= 1) & (slot <= 2 * FILTER_BANDS), fr[(slot - 1) % FILTER_BANDS], 0.0)
    ph = np.where((slot > FILTER_BANDS) & (slot <= 2 * FILTER_BANDS), 0.5 * np.pi, 0.0)
    q["frl"] = jnp.asarray(frl[None, :], dtype=F32)
    q["ph"] = jnp.asarray(ph[None, :], dtype=F32)
    q["delta"] = jnp.abs(jnp.linspace(MIN_DECAY, MAX_DECAY, D_MODEL, dtype=F32))[None, :]

    def twice(w):
        z = jnp.zeros_like(w)
        return jnp.concatenate([jnp.concatenate([w, z], axis=1), jnp.concatenate([z, w], axis=1)], axis=0)

    filt = []
    for i in range(n_odd):
        w1p = jnp.zeros((FILTER_WIDTH, FILTER_WIDTH), F32).at[:FILTER_EMB].set(p["hy_f_w1"][i])
        row = lambda a: jnp.tile(a[None, :].astype(F32), (1, 2))
        filt.append((twice(w1p), row(p["hy_f_b1"][i]), row(p["hy_f_freq1"][i]),
                     twice(p["hy_f_w2"][i]), row(p["hy_f_b2"][i]), row(p["hy_f_freq2"][i]),
                     twice(p["hy_f_w3"][i]), row(p["hy_f_b3"][i]), row(p["hy_f_freq3"][i]),
                     twice(p["hy_f_wout"][i])))
    q["filt"] = filt
    q["bias_tab"] = _bias_table(p["rel_bias"])
    return q


def _hyena_conv(x, q, l, consts, T):
    i = l // 2
    L = x.shape[1]
    hf, hb = _filter(L, q["frl"], q["ph"], *q["filt"][i], q["delta"])
    kf = _filter_spectrum(hf, hb, q["hy_d"][i], consts)
    x0, zt = _hy_in(x, q["mix_norm"][l], q["hy_win"][i], q["hy_cw"][i], q["hy_cb"][i], T)
    return x0, _fftconv(zt, kf, consts)


def _trunk(x, q):
    B, L, D = x.shape
    consts = _dft_consts(L)
    t_mix, t_ffn = min(TOKENS_MIXER, L), min(TOKENS_FFN, L)
    for l in range(q["depth"]):
        i = l // 2
        ffn = (q["ffn_norm"][l], q["ffn_wup"][l], q["ffn_cw"][l], q["ffn_cb"][l], q["ffn_wdn"][l])
        if l % 2 == 0:
            x = _ab_layer(x, q["sink"][i], q["mix_norm"][l], q["ab_win"][i], q["ab_wout"][i],
                          q["gm"], q["kg"][i], q["bias_tab"], q["vg"][i], q["ws"][i],
                          q["bs"][i], t_mix)
            x = _ffn(x, *ffn, t_ffn)
        else:
            x0, yt = _hyena_conv(x, q, l, consts, t_mix)
            x = _hy_ffn(x, x0, yt, q["hy_wout"][i], *ffn, t_ffn)
    return x


def kernel(x_prompt, x_sample, rel_bias, mix_norm, ffn_norm, ab_w_in, q_norm, k_norm, attn_sink, sgu_v_norm, sgu_w, sgu_b, ab_w_out, hy_w_in, hy_conv_w, hy_conv_b, hy_f_w1, hy_f_b1, hy_f_freq1, hy_f_w2, hy_f_b2, hy_f_freq2, hy_f_w3, hy_f_b3, hy_f_freq3, hy_f_wout, hy_d, hy_w_out, ffn_w_up, ffn_conv_w, ffn_conv_b, ffn_w_down):
    p = dict(rel_bias=rel_bias, mix_norm=mix_norm, ffn_norm=ffn_norm, ab_w_in=ab_w_in,
             q_norm=q_norm, k_norm=k_norm, attn_sink=attn_sink, sgu_v_norm=sgu_v_norm,
             sgu_w=sgu_w, sgu_b=sgu_b, ab_w_out=ab_w_out, hy_w_in=hy_w_in, hy_conv_w=hy_conv_w,
             hy_conv_b=hy_conv_b, hy_f_w1=hy_f_w1, hy_f_b1=hy_f_b1, hy_f_freq1=hy_f_freq1,
             hy_f_w2=hy_f_w2, hy_f_b2=hy_f_b2, hy_f_freq2=hy_f_freq2, hy_f_w3=hy_f_w3,
             hy_f_b3=hy_f_b3, hy_f_freq3=hy_f_freq3, hy_f_wout=hy_f_wout, hy_d=hy_d,
             hy_w_out=hy_w_out, ffn_w_up=ffn_w_up, ffn_conv_w=ffn_conv_w, ffn_conv_b=ffn_conv_b,
             ffn_w_down=ffn_w_down)
    q = _prepare(p)
    return (_trunk(x_prompt, q), _trunk(x_sample, q))
```

```python
import functools
import math

import numpy as np
import jax
import jax.numpy as jnp
from jax import lax
from jax.experimental import pallas as pl
from jax.experimental.pallas import tpu as pltpu

F32 = jnp.float32
BF16 = jnp.bfloat16

D_MODEL = 1024
HEAD_DIM = 64
N_HEADS_A = 8
N_KV_A = 2
GQA = 4
ATTN_W = 512
WINDOW = 128
BLOCK = 128
N_BUCKETS = 32
MAX_DISTANCE = 128
SGU_W = 512
N_GROUPS_B = 8
CH_B = 64
FILTER_EMB = 33
FILTER_BANDS = 16
FILTER_WIDTH = 64
DECAY_TARGET = 1e-2
MIN_DECAY = math.log(DECAY_TARGET) / 1.5
MAX_DECAY = math.log(DECAY_TARGET) / 0.3
FFN_HIDDEN = 2816
EPS = 1e-6
NEG = -1e30

LANES = 128
SUBLANES = 8
TOKENS_FFN = 512
TOKENS_MIXER = 1024
FFN_CHUNK = 256
HY_CHUNK = 256
HALO = 8
ROW_BLOCK = 64
FFN_SLAB_SETS = 2
FFT_N2 = LANES
FFT_ROWS = 4096
VMEM_LIMIT = 56 * 1024 * 1024


def _gelu(x):
    t = jnp.tanh(x * (0.7978845608028654 + (0.7978845608028654 * 0.044715) * (x * x)))
    hx = 0.5 * x
    return hx + hx * t


def _rms(x, gain):
    return x * lax.rsqrt(jnp.mean(x * x, axis=-1, keepdims=True) + EPS) * gain


def _const_spec(shape):
    nd = len(shape)
    return pl.BlockSpec(shape, lambda *_: (0,) * nd, pipeline_mode=pl.Buffered(1))


def _params(n_axes):
    return pltpu.CompilerParams(dimension_semantics=("arbitrary",) * n_axes,
                                vmem_limit_bytes=VMEM_LIMIT)


def _park_slabs(s_ref, base, h):
    for s in range(h.shape[1] // LANES):
        s_ref[base + s] = h[:, s * LANES:(s + 1) * LANES]


def _conv3_slab(s_ref, slab, w, b, r0, rows):
    lo = HALO + r0
    return (s_ref[slab, lo - 1:lo - 1 + rows, :] * w[0:1]
            + s_ref[slab, lo:lo + rows, :] * w[1:2]
            + s_ref[slab, lo + 1:lo + 1 + rows, :] * w[2:3]
            + b)


def _fill_normed_ext(xe_ref, x, xp, xn, gain, rows, first, last):
    pm = jnp.where(first, 0.0, 1.0)
    nm = jnp.where(last, 0.0, 1.0)
    xe_ref[0:HALO, :] = (_rms(xp, gain) * pm).astype(BF16)
    xe_ref[HALO:HALO + rows, :] = _rms(x, gain).astype(BF16)
    xe_ref[HALO + rows:HALO + rows + HALO, :] = (_rms(xn, gain) * nm).astype(BF16)


def _halo_specs(T, L, D):
    hb = T // HALO
    last = L // HALO - 1
    return [
        pl.BlockSpec((None, T, D), lambda b, i: (b, i, 0)),
        pl.BlockSpec((None, HALO, D), lambda b, i: (b, jnp.maximum(i * hb - 1, 0), 0)),
        pl.BlockSpec((None, HALO, D), lambda b, i: (b, jnp.minimum((i + 1) * hb, last), 0)),
    ]


def _ffn_body(x, xp, xn, g_ref, wup_ref, cw_ref, cb_ref, wdn_ref, o_ref, xe_ref, a_ref, s_ref,
              T, nchunk):
    i = pl.program_id(1)
    _fill_normed_ext(xe_ref, x, xp, xn, g_ref[...], T, i == 0, i == pl.num_programs(1) - 1)
    xe = xe_ref[...]
    ns = FFN_CHUNK // LANES
    for j in range(nchunk):
        base = (j % (s_ref.shape[0] // (2 * ns))) * 2 * ns
        for part in range(2):
            lo = part * FFN_HIDDEN + j * FFN_CHUNK
            _park_slabs(s_ref, base + part * ns,
                        jnp.dot(xe, wup_ref[:, lo:lo + FFN_CHUNK], preferred_element_type=F32))
        for s in range(ns):
            col = j * FFN_CHUNK + s * LANES
            wg, bg = cw_ref[:, col:col + LANES], cb_ref[:, col:col + LANES]
            wu = cw_ref[:, FFN_HIDDEN + col:FFN_HIDDEN + col + LANES]
            bu = cb_ref[:, FFN_HIDDEN + col:FFN_HIDDEN + col + LANES]
            for r0 in range(0, T, ROW_BLOCK):
                g = _conv3_slab(s_ref, base + s, wg, bg, r0, ROW_BLOCK)
                u = _conv3_slab(s_ref, base + ns + s, wu, bu, r0, ROW_BLOCK)
                a_ref[r0:r0 + ROW_BLOCK, col:col + LANES] = (_gelu(g) * u).astype(BF16)
    o_ref[...] = x + jnp.dot(a_ref[...], wdn_ref[...], preferred_element_type=F32)


def _ffn_kernel(x_ref, xp_ref, xn_ref, g_ref, wup_ref, cw_ref, cb_ref, wdn_ref, o_ref,
                xe_ref, a_ref, s_ref, *, T, nchunk):
    _ffn_body(x_ref[...], xp_ref[...], xn_ref[...], g_ref, wup_ref, cw_ref, cb_ref, wdn_ref,
              o_ref, xe_ref, a_ref, s_ref, T, nchunk)


def _ffn_scratch(T, D):
    return [pltpu.VMEM((T + 2 * HALO, D), BF16), pltpu.VMEM((T, FFN_HIDDEN), BF16),
            pltpu.VMEM((FFN_SLAB_SETS * 2 * FFN_CHUNK // LANES, T + 2 * HALO, LANES), F32)]


def _hy_ffn_kernel(x_ref, xp_ref, xn_ref, x0_ref, x0p_ref, x0n_ref, yt_ref, ytp_ref, ytn_ref,
                   wo_ref, g_ref, wup_ref, cw_ref, cb_ref, wdn_ref, o_ref,
                   xe_ref, a_ref, s_ref, m_ref, *, T, nchunk):
    m_ref[0:HALO, :] = (x0p_ref[...] * ytp_ref[...].T[BLOCK - HALO:, :]).astype(BF16)
    for n in range(T // BLOCK):
        lo = HALO + n * BLOCK
        m_ref[lo:lo + BLOCK, :] = (x0_ref[n * BLOCK:(n + 1) * BLOCK, :] * yt_ref[n].T).astype(BF16)
    m_ref[HALO + T:, :] = (x0n_ref[...] * ytn_ref[...].T[:HALO, :]).astype(BF16)
    mix = jnp.dot(m_ref[...], wo_ref[...], preferred_element_type=F32)
    _ffn_body(x_ref[...] + mix[HALO:HALO + T], xp_ref[...] + mix[:HALO], xn_ref[...] + mix[HALO + T:],
              g_ref, wup_ref, cw_ref, cb_ref, wdn_ref, o_ref, xe_ref, a_ref, s_ref, T, nchunk)


def _hy_ffn(x, x0, yt, wo, gain, wup_r, cw_r, cb_r, wdn, T):
    B, L, D = x.shape
    nchunk = FFN_HIDDEN // FFN_CHUNK
    nb = T // BLOCK
    last = L // BLOCK - 1
    kern = functools.partial(_hy_ffn_kernel, T=T, nchunk=nchunk)
    return pl.pallas_call(
        kern,
        grid=(B, L // T),
        in_specs=_halo_specs(T, L, D) + _halo_specs(T, L, D) + [
            pl.BlockSpec((None, nb, D, LANES), lambda b, i: (b, i, 0, 0)),
            pl.BlockSpec((None, None, D, LANES), lambda b, i: (b, jnp.maximum(i * nb - 1, 0), 0, 0)),
            pl.BlockSpec((None, None, D, LANES), lambda b, i: (b, jnp.minimum((i + 1) * nb, last), 0, 0)),
            _const_spec(wo.shape), _const_spec((1, D)), _const_spec(wup_r.shape),
            _const_spec(cw_r.shape), _const_spec(cb_r.shape), _const_spec(wdn.shape)],
        out_specs=pl.BlockSpec((None, T, D), lambda b, i: (b, i, 0)),
        out_shape=jax.ShapeDtypeStruct(x.shape, F32),
        scratch_shapes=_ffn_scratch(T, D) + [pltpu.VMEM((T + 2 * HALO, D), BF16)],
        compiler_params=_params(2),
        name="hyena_out_conv_ffn",
    )(x, x, x, x0, x0, x0, yt, yt, yt, wo, gain, wup_r, cw_r, cb_r, wdn)


def _ffn(x, gain, wup_r, cw_r, cb_r, wdn, T):
    B, L, D = x.shape
    nchunk = FFN_HIDDEN // FFN_CHUNK
    kern = functools.partial(_ffn_kernel, T=T, nchunk=nchunk)
    return pl.pallas_call(
        kern,
        grid=(B, L // T),
        in_specs=_halo_specs(T, L, D) + [
            _const_spec((1, D)), _const_spec(wup_r.shape), _const_spec(cw_r.shape),
            _const_spec(cb_r.shape), _const_spec(wdn.shape)],
        out_specs=pl.BlockSpec((None, T, D), lambda b, i: (b, i, 0)),
        out_shape=jax.ShapeDtypeStruct(x.shape, F32),
        scratch_shapes=_ffn_scratch(T, D),
        compiler_params=_params(2),
        name="conv_ffn",
    )(x, x, x, gain, wup_r, cw_r, cb_r, wdn)


def _hy_in_kernel(x_ref, xp_ref, xn_ref, g_ref, win_ref, cw_ref, cb_ref, x0_ref, zt_ref,
                  xe_ref, s_ref, *, T, nchunk):
    i = pl.program_id(1)
    _fill_normed_ext(xe_ref, x_ref[...], xp_ref[...], xn_ref[...], g_ref[...], T,
                     i == 0, i == pl.num_programs(1) - 1)
    xe = xe_ref[...]
    ns = HY_CHUNK // LANES
    for j in range(nchunk):
        base = (j % 2) * 3 * ns
        for k in range(3):
            lo = k * D_MODEL + j * HY_CHUNK
            _park_slabs(s_ref, base + k * ns,
                        jnp.dot(xe, win_ref[:, lo:lo + HY_CHUNK], preferred_element_type=F32))
        for s in range(ns):
            col = j * HY_CHUNK + s * LANES
            w = [cw_ref[:, k * D_MODEL + col:k * D_MODEL + col + LANES] for k in range(3)]
            b = [cb_ref[:, k * D_MODEL + col:k * D_MODEL + col + LANES] for k in range(3)]
            for n in range(T // BLOCK):
                r0 = n * BLOCK
                x0_ref[r0:r0 + BLOCK, col:col + LANES] = _conv3_slab(s_ref, base + s, w[0], b[0], r0, BLOCK)
                z = (_conv3_slab(s_ref, base + ns + s, w[1], b[1], r0, BLOCK)
                     * _conv3_slab(s_ref, base + 2 * ns + s, w[2], b[2], r0, BLOCK))
                zt_ref[n, col:col + LANES, :] = z.T


def _hy_in(x, gain, win_r, cw_r, cb_r, T):
    B, L, D = x.shape
    nchunk = D // HY_CHUNK
    kern = functools.partial(_hy_in_kernel, T=T, nchunk=nchunk)
    return pl.pallas_call(
        kern,
        grid=(B, L // T),
        in_specs=_halo_specs(T, L, D) + [
            _const_spec((1, D)), _const_spec(win_r.shape), _const_spec(cw_r.shape),
            _const_spec(cb_r.shape)],
        out_specs=[pl.BlockSpec((None, T, D), lambda b, i: (b, i, 0)),
                   pl.BlockSpec((None, T // BLOCK, D, LANES), lambda b, i: (b, i, 0, 0))],
        out_shape=[jax.ShapeDtypeStruct(x.shape, F32),
                   jax.ShapeDtypeStruct((B, L // BLOCK, D, LANES), F32)],
        scratch_shapes=[pltpu.VMEM((T + 2 * HALO, D), BF16),
                        pltpu.VMEM((2 * 3 * HY_CHUNK // LANES, T + 2 * HALO, LANES), F32)],
        compiler_params=_params(2),
        name="hyena_in",
    )(x, x, x, gain, win_r, cw_r, cb_r)


def _dot_f32(a, w_ref):
    a_hi = a.astype(BF16)
    a_lo = (a - a_hi.astype(F32)).astype(BF16)
    w_hi, w_lo = w_ref[0], w_ref[1]
    return (jnp.dot(a_hi, w_hi, preferred_element_type=F32)
            + jnp.dot(a_lo, w_hi, preferred_element_type=F32)
            + jnp.dot(a_hi, w_lo, preferred_element_type=F32))


def _filter_kernel(frl_ref, ph_ref, w1_ref, b1_ref, f1_ref, w2_ref, b2_ref, f2_ref, w3_ref, b3_ref,
                   f3_ref, wout_ref, delta_ref, hf_ref, hb_ref, *, L, R):
    hr = R // 2
    ja = pl.program_id(0) * R + lax.broadcasted_iota(jnp.int32, (hr, 1), 0)
    jb = ja + hr
    lane = lax.broadcasted_iota(jnp.int32, (hr, LANES), 1)
    j = jnp.where(lane < FILTER_WIDTH, ja, jb).astype(F32)
    t = j / float(L - 1)
    w = (2.0 * math.pi) * j / float(L)
    feats = jnp.where(lane % FILTER_WIDTH == 0, t, jnp.cos(w * frl_ref[...] + ph_ref[...]))
    h = jnp.sin(f1_ref[...] * (_dot_f32(feats, w1_ref) + b1_ref[...]))
    h = jnp.sin(f2_ref[...] * (_dot_f32(h, w2_ref) + b2_ref[...]))
    h = jnp.sin(f3_ref[...] * (_dot_f32(h, w3_ref) + b3_ref[...]))
    h = _dot_f32(h, wout_ref)
    for half, jidx in enumerate((ja, jb)):
        decay = jnp.exp(-(jidx.astype(F32) / float(L - 1)) * delta_ref[...])
        base = half * 2 * D_MODEL
        hf = h[:, base:base + D_MODEL] * decay
        hb = jnp.where(jidx == 0, 0.0, h[:, base + D_MODEL:base + 2 * D_MODEL] * decay)
        for n in range(hr // BLOCK):
            blk = half * (hr // BLOCK) + n
            hf_ref[blk] = hf[n * BLOCK:(n + 1) * BLOCK, :].T
            hb_ref[blk] = hb[n * BLOCK:(n + 1) * BLOCK, :].T


def _filter(L, frl, ph, w1, b1, f1, w2, b2, f2, w3, b3, f3, wout, delta):
    R = 512
    kern = functools.partial(_filter_kernel, L=L, R=R)
    consts = [frl, ph, w1, b1, f1, w2, b2, f2, w3, b3, f3, wout, delta]
    out_spec = pl.BlockSpec((R // BLOCK, D_MODEL, LANES), lambda i: (i, 0, 0))
    shape = jax.ShapeDtypeStruct((L // BLOCK, D_MODEL, LANES), F32)
    return pl.pallas_call(
        kern,
        grid=(L // R,),
        in_specs=[_const_spec(a.shape) for a in consts],
        out_specs=[out_spec, out_spec],
        out_shape=[shape, shape],
        compiler_params=_params(1),
        name="hyena_filter",
    )(*consts)


def _dft_consts(L):
    N = 2 * L
    N2 = FFT_N2
    N1 = N // N2
    h = N1 // 2
    a1 = -2.0 * np.pi * np.outer(np.arange(N1), np.arange(N1)) / N1
    f1r, f1i = np.cos(a1), np.sin(a1)
    a2 = -2.0 * np.pi * np.outer(np.arange(N2), np.arange(N2)) / N2
    f2r, f2i = np.cos(a2), np.sin(a2)
    at = -2.0 * np.pi * np.outer(np.arange(N1), np.arange(N2)) / N
    m1 = np.block([[f1r[:, :h], -f1i[:, :h]], [f1i[:, :h], f1r[:, :h]]])
    m1f = np.concatenate([f1r[:, :h], f1i[:, :h]], axis=0)
    m3 = np.block([[f1r[:h], f1i[:h]], [-f1i[:h], f1r[:h]]])
    c = lambda a: jnp.asarray(a, dtype=BF16)
    f = lambda a: jnp.asarray(a, dtype=F32)
    return dict(N1=N1, m1=c(m1), m1f=c(m1f), m3=c(m3),
                r2f=c(np.block([[f2r, f2i], [-f2i, f2r]])),
                r2i=c(np.block([[f2r, -f2i], [f2i, f2r]])),
                twr=f(np.cos(at)), twi=f(np.sin(at)))


def _cmul(ar, ai, br, bi):
    return ar * br - ai * bi, ar * bi + ai * br


def _slab8(ref, idx, c0, n):
    v = ref[idx + (slice(None), slice(c0, c0 + SUBLANES), slice(None))]
    return v.reshape(n * SUBLANES, LANES)


def _chan_rows(s_ref, base, c, n):
    return s_ref[pl.ds(base + c, n, stride=SUBLANES), :]


def _fft_kernel(x_ref, kf_ref, m1_ref, m3_ref, twr_ref, twi_ref, r2f_ref, r2i_ref, o_ref,
                *scratch, N1, nc):
    h = N1 // 2
    h8 = h * SUBLANES
    nsub = nc // SUBLANES
    twr, twi = twr_ref[...], twi_ref[...]
    cols = []
    for s8 in range(nsub):
        sin_ref = scratch[s8]
        sin_ref[0:h8] = _slab8(x_ref, (0,), s8 * SUBLANES, h)
        sin_ref[h8:2 * h8] = _slab8(x_ref, (1,), s8 * SUBLANES, h)
        cols += [jnp.concatenate([_chan_rows(sin_ref, 0, c, h), _chan_rows(sin_ref, h8, c, h)], axis=0)
                 for c in range(SUBLANES)]
    a = jnp.dot(m1_ref[...], jnp.concatenate(cols, axis=1).astype(BF16),
                preferred_element_type=F32)
    lhs = []
    for c in range(nc):
        tr, ti = _cmul(a[:N1, c * LANES:(c + 1) * LANES], a[N1:, c * LANES:(c + 1) * LANES], twr, twi)
        lhs.append(jnp.concatenate([tr, ti], axis=1))
    lhs = jnp.concatenate(lhs, axis=0).astype(BF16)
    X = jnp.dot(lhs, r2f_ref[...], preferred_element_type=F32)
    yr, yi = _cmul(X[:, :LANES], X[:, LANES:],
                   kf_ref[0].reshape(nc * N1, LANES), kf_ref[1].reshape(nc * N1, LANES))
    Bm = jnp.dot(jnp.concatenate([yr, yi], axis=1).astype(BF16), r2i_ref[...],
                 preferred_element_type=F32)
    re_cols, im_cols = [], []
    for c in range(nc):
        br, bi = Bm[c * N1:(c + 1) * N1, :LANES], Bm[c * N1:(c + 1) * N1, LANES:]
        re_cols.append(br * twr + bi * twi)
        im_cols.append(bi * twr - br * twi)
    rhs = jnp.concatenate([jnp.concatenate(re_cols, axis=1),
                           jnp.concatenate(im_cols, axis=1)], axis=0).astype(BF16)
    y = jnp.dot(m3_ref[...], rhs, preferred_element_type=F32)
    for s8 in range(nsub):
        c0 = s8 * SUBLANES
        sout_ref = scratch[nsub + s8]
        for c in range(SUBLANES):
            lo = (c0 + c) * LANES
            sout_ref[pl.ds(c, h, stride=SUBLANES), :] = y[:h, lo:lo + LANES]
            sout_ref[pl.ds(h8 + c, h, stride=SUBLANES), :] = y[h:, lo:lo + LANES]
        o_ref[0, :, c0:c0 + SUBLANES, :] = sout_ref[0:h8].reshape(h, SUBLANES, LANES)
        o_ref[1, :, c0:c0 + SUBLANES, :] = sout_ref[h8:2 * h8].reshape(h, SUBLANES, LANES)


def _fftconv(zt, kf, c):
    B, h, C, _ = zt.shape
    N1 = c["N1"]
    nc = min(FFT_ROWS // N1, C)
    kern = functools.partial(_fft_kernel, N1=N1, nc=nc)
    blk = pl.BlockSpec((2, h, nc, LANES), lambda cb, p: (p, 0, cb, 0))
    consts = [c["m1"], c["m3"], c["twr"], c["twi"], c["r2f"], c["r2i"]]
    return pl.pallas_call(
        kern,
        grid=(C // nc, B // 2),
        in_specs=[blk, pl.BlockSpec((2, nc, N1, LANES), lambda cb, p: (0, cb, 0, 0))]
        + [_const_spec(a.shape) for a in consts],
        out_specs=blk,
        out_shape=jax.ShapeDtypeStruct(zt.shape, F32),
        scratch_shapes=[pltpu.VMEM((N1 * SUBLANES, LANES), F32)] * (2 * nc // SUBLANES),
        compiler_params=_params(2),
        name="fft_conv",
    )(zt, kf, *consts)


def _fspec_kernel(hf_ref, hb_ref, d_ref, m1f_ref, twr_ref, twi_ref, r2f_ref, o_ref,
                  *scratch, N1, nc, scale):
    h = N1 // 2
    h8 = h * SUBLANES
    twr, twi = twr_ref[...], twi_ref[...]
    cols = []
    for s8 in range(nc // SUBLANES):
        s_ref = scratch[s8]
        s_ref[0:h8] = _slab8(hf_ref, (), s8 * SUBLANES, h)
        s_ref[h8:2 * h8] = _slab8(hb_ref, (), s8 * SUBLANES, h)
        for c in range(SUBLANES):
            cols += [_chan_rows(s_ref, 0, c, h), _chan_rows(s_ref, h8, c, h)]
    a = jnp.dot(m1f_ref[...], jnp.concatenate(cols, axis=1).astype(BF16),
                preferred_element_type=F32)
    lhs = []
    for k in range(2 * nc):
        tr, ti = _cmul(a[:N1, k * LANES:(k + 1) * LANES], a[N1:, k * LANES:(k + 1) * LANES], twr, twi)
        lhs.append(jnp.concatenate([tr, ti], axis=1))
    lhs = jnp.concatenate(lhs, axis=0).astype(BF16)
    X = jnp.dot(lhs, r2f_ref[...], preferred_element_type=F32)
    for c in range(nc):
        xf = X[(2 * c) * N1:(2 * c + 1) * N1]
        xb = X[(2 * c + 1) * N1:(2 * c + 2) * N1]
        d = d_ref[c:c + 1, :]
        o_ref[0, c] = (xf[:, :LANES] + xb[:, :LANES] + d) * scale
        o_ref[1, c] = (xf[:, LANES:] - xb[:, LANES:]) * scale


def _filter_spectrum(hf, hb, d, c):
    h, C, _ = hf.shape
    N1 = c["N1"]
    nc = min(FFT_ROWS // N1, C)
    kern = functools.partial(_fspec_kernel, N1=N1, nc=nc, scale=1.0 / (N1 * FFT_N2))
    blk = pl.BlockSpec((h, nc, LANES), lambda cb: (0, cb, 0))
    consts = [c["m1f"], c["twr"], c["twi"], c["r2f"]]
    return pl.pallas_call(
        kern,
        grid=(C // nc,),
        in_specs=[blk, blk, pl.BlockSpec((nc, 1), lambda cb: (cb, 0))]
        + [_const_spec(a.shape) for a in consts],
        out_specs=pl.BlockSpec((2, nc, N1, LANES), lambda cb: (0, cb, 0, 0)),
        out_shape=jax.ShapeDtypeStruct((2, C, N1, LANES), F32),
        scratch_shapes=[pltpu.VMEM((N1 * SUBLANES, LANES), F32)] * (nc // SUBLANES),
        compiler_params=_params(1),
        name="filter_spectrum",
    )(hf, hb, d, *consts)


def _t5_bucket(rel):
    half = N_BUCKETS // 2
    max_exact = half // 2
    ret = jnp.where(rel > 0, half, 0)
    n = jnp.abs(rel)
    nf = jnp.maximum(n, 1).astype(jnp.float32)
    large = max_exact + (jnp.log(nf / max_exact) / math.log(MAX_DISTANCE / max_exact)
                         * (half - max_exact)).astype(jnp.int32)
    large = jnp.minimum(large, half - 1)
    return ret + jnp.where(n < max_exact, n, large)


LOG2E = 1.4426950408889634


def _bias_kernel(rb_ref, bucket_ref, rel_ref, o_ref):
    bucket = bucket_ref[...]
    rel = rel_ref[...]
    key = lax.broadcasted_iota(jnp.int32, bucket.shape, 0)
    for h in range(N_HEADS_A):
        acc = jnp.zeros(bucket.shape, F32)
        for b in range(N_BUCKETS):
            acc = jnp.where(bucket == b, rb_ref[b, h], acc)
        acc = acc * LOG2E
        for v in range(4):
            ok = jnp.abs(rel) <= WINDOW
            if v & 1:
                ok = ok & (key >= BLOCK)
            if v & 2:
                ok = ok & (key < 2 * BLOCK)
            o_ref[v, h] = jnp.where(ok, acc, NEG)


def _bias_table(rel_bias):
    rel = (jnp.arange(3 * BLOCK)[:, None] - BLOCK) - jnp.arange(BLOCK)[None, :]
    rel = rel.astype(jnp.int32)
    bucket = _t5_bucket(rel).astype(jnp.int32)
    return pl.pallas_call(
        _bias_kernel,
        in_specs=[pl.BlockSpec(memory_space=pltpu.SMEM),
                  pl.BlockSpec(memory_space=pltpu.VMEM), pl.BlockSpec(memory_space=pltpu.VMEM)],
        out_specs=pl.BlockSpec(memory_space=pltpu.VMEM),
        out_shape=jax.ShapeDtypeStruct((4, N_HEADS_A, 3 * BLOCK, BLOCK), F32),
        name="rel_bias_table",
    )(rel_bias.astype(F32), bucket, rel)


def _group_ms(v, gmat):
    return jnp.dot((v * v).astype(BF16), gmat, preferred_element_type=F32)


def _ab_kernel(sink_ref, x_ref, xp_ref, xn_ref, g_ref, win_ref, wout_ref, gm_ref,
               kg_ref, bias_ref, vg_ref, ws_ref, bs_ref, o_ref, *, T):
    i = pl.program_id(1)
    nb = T // BLOCK
    nblocks = pl.num_programs(1) * nb
    gain = g_ref[...]
    x = x_ref[...]
    proj = jnp.dot(_rms(x, gain).astype(BF16), win_ref[...], preferred_element_type=F32)
    xh = jnp.concatenate([_rms(xp_ref[...], gain), _rms(xn_ref[...], gain)], axis=0)
    kvh = jnp.dot(xh.astype(BF16), win_ref[:, ATTN_W:ATTN_W + 2 * BLOCK],
                  preferred_element_type=F32)
    gm = gm_ref[...]
    gm_k = gm[:BLOCK, :BLOCK]

    k_all = jnp.concatenate([kvh[:BLOCK, :BLOCK], proj[:, ATTN_W:ATTN_W + BLOCK],
                             kvh[BLOCK:, :BLOCK]], axis=0)
    v_all = jnp.concatenate([kvh[:BLOCK, BLOCK:], proj[:, ATTN_W + BLOCK:ATTN_W + 2 * BLOCK],
                             kvh[BLOCK:, BLOCK:]], axis=0)
    kn = k_all * lax.rsqrt(_group_ms(k_all, gm_k) + EPS) * kg_ref[...]

    su = _gelu(proj[:, ATTN_W + 2 * BLOCK:ATTN_W + 2 * BLOCK + SGU_W])
    sv = _gelu(proj[:, ATTN_W + 2 * BLOCK + SGU_W:])
    svn = (sv * lax.rsqrt(_group_ms(sv, gm) + EPS) * vg_ref[...]).astype(BF16)
    low = lax.broadcasted_iota(jnp.int32, (1, BLOCK * nb), 1) % BLOCK < CH_B
    slabs = []
    for j in range(SGU_W // BLOCK):
        rhs = jnp.concatenate([svn[n * BLOCK:(n + 1) * BLOCK, j * BLOCK:(j + 1) * BLOCK]
                               for n in range(nb)], axis=1)
        a = jnp.dot(ws_ref[2 * j], rhs, preferred_element_type=F32)
        b = jnp.dot(ws_ref[2 * j + 1], rhs, preferred_element_type=F32)
        slabs.append(jnp.where(low, a, b))
    mixed = jnp.concatenate(
        [jnp.concatenate([slabs[j][:, n * BLOCK:(n + 1) * BLOCK] for j in range(SGU_W // BLOCK)],
                         axis=1) + bs_ref[...] for n in range(nb)], axis=0)
    sgu_out = jnp.dot((su * mixed).astype(BF16), wout_ref[ATTN_W:, :], preferred_element_type=F32)

    q_t = proj[:, :ATTN_W].T
    qt = []
    for h in range(N_HEADS_A):
        qh = q_t[h * HEAD_DIM:(h + 1) * HEAD_DIM]
        qt.append((qh * lax.rsqrt(jnp.mean(qh * qh, axis=0, keepdims=True) + EPS)).astype(BF16))
    qt = jnp.concatenate(qt, axis=0)
    vt = v_all.T.astype(BF16)
    knb = kn.astype(BF16)
    zeros_q = jnp.zeros((HEAD_DIM, GQA * BLOCK), BF16)
    ones_rows = jnp.ones((2 * SUBLANES, 3 * BLOCK), BF16)
    units = [(n, hk) for n in range(nb) for hk in range(N_KV_A)]
    sks = [jnp.concatenate([jnp.full((1, BLOCK), sink_ref[hk * GQA + g] * LOG2E, F32)
                            for g in range(GQA)], axis=1) for hk in range(N_KV_A)]
    scores = []
    for n, hk in units:
        blk = i * nb + n
        variant = jnp.where(blk == 0, 1, 0) + jnp.where(blk == nblocks - 1, 2, 0)
        qh = jnp.concatenate(
            [qt[(hk * GQA + g) * HEAD_DIM:(hk * GQA + g + 1) * HEAD_DIM, n * BLOCK:(n + 1) * BLOCK]
             for g in range(GQA)], axis=1)
        qz = jnp.concatenate([qh, zeros_q] if hk == 0 else [zeros_q, qh], axis=0)
        s = jnp.dot(knb[n * BLOCK:n * BLOCK + 3 * BLOCK, :], qz,
                    preferred_element_type=F32)
        scores.append(s + jnp.concatenate([bias_ref[variant, hk * GQA + g] for g in range(GQA)], axis=1))
    probs = []
    for (n, hk), s in zip(units, scores):
        m = jnp.maximum(jnp.max(s, axis=0, keepdims=True), sks[hk])
        probs.append((jnp.exp2(s - m).astype(BF16), jnp.exp2(sks[hk] - m)))
    outs = []
    for (n, hk), (p, psink) in zip(units, probs):
        lhs = jnp.concatenate([vt[hk * HEAD_DIM:(hk + 1) * HEAD_DIM, n * BLOCK:n * BLOCK + 3 * BLOCK],
                               ones_rows], axis=0)
        pv = jnp.dot(lhs, p, preferred_element_type=F32)
        outs.append(pv[:HEAD_DIM] / (pv[HEAD_DIM:HEAD_DIM + 1] + psink))
    out_cols = []
    for n in range(nb):
        out_cols.append(jnp.concatenate(
            [outs[n * N_KV_A + hk][:, g * BLOCK:(g + 1) * BLOCK]
             for hk in range(N_KV_A) for g in range(GQA)], axis=0))
    attn = jnp.concatenate(out_cols, axis=1).T.astype(BF16)
    o_ref[...] = x + sgu_out + jnp.dot(attn, wout_ref[:ATTN_W, :], preferred_element_type=F32)


def _ab_layer(x, sink, gain, win, wout, gm, kg, bias_tab, vg, ws, bs, T):
    B, L, D = x.shape
    hb = T // BLOCK
    last = L // BLOCK - 1
    kern = functools.partial(_ab_kernel, T=T)
    consts = [gain, win, wout, gm, kg, bias_tab, vg, ws, bs]
    return pl.pallas_call(
        kern,
        grid=(B, L // T),
        in_specs=[pl.BlockSpec(memory_space=pltpu.SMEM),
                  pl.BlockSpec((None, T, D), lambda b, i: (b, i, 0)),
                  pl.BlockSpec((None, BLOCK, D), lambda b, i: (b, jnp.maximum(i * hb - 1, 0), 0)),
                  pl.BlockSpec((None, BLOCK, D), lambda b, i: (b, jnp.minimum((i + 1) * hb, last), 0)),
                  ] + [_const_spec(a.shape) for a in consts],
        out_specs=pl.BlockSpec((None, T, D), lambda b, i: (b, i, 0)),
        out_shape=jax.ShapeDtypeStruct(x.shape, F32),
        compiler_params=_params(2),
        name="attn_sgu_mixer",
    )(sink, x, x, x, *consts)


def _prepare(p):
    depth = p["mix_norm"].shape[0]
    q = dict(depth=depth)
    q["mix_norm"] = [p["mix_norm"][l][None, :] for l in range(depth)]
    q["ffn_norm"] = [p["ffn_norm"][l][None, :] for l in range(depth)]
    q["ffn_wup"] = [p["ffn_w_up"][l].astype(BF16) for l in range(depth)]
    q["ffn_cw"] = [p["ffn_conv_w"][l] for l in range(depth)]
    q["ffn_cb"] = [p["ffn_conv_b"][l][None, :] for l in range(depth)]
    q["ffn_wdn"] = [p["ffn_w_down"][l].astype(BF16) for l in range(depth)]

    n_even = p["ab_w_in"].shape[0]
    q["ab_win"] = [p["ab_w_in"][i].astype(BF16) for i in range(n_even)]
    q["ab_wout"] = [p["ab_w_out"][i].astype(BF16) for i in range(n_even)]
    q["kg"] = [jnp.tile(p["k_norm"][i] * p["q_norm"][i] * (HEAD_DIM ** -0.5 * LOG2E), N_KV_A)[None, :]
               for i in range(n_even)]
    q["sink"] = [p["attn_sink"][i].astype(F32) for i in range(n_even)]
    q["vg"] = [p["sgu_v_norm"][i].reshape(1, SGU_W) for i in range(n_even)]
    q["ws"] = [p["sgu_w"][i].astype(BF16) for i in range(n_even)]
    q["bs"] = [jnp.repeat(p["sgu_b"][i].T, CH_B, axis=1) for i in range(n_even)]
    grp = np.arange(ATTN_W) // HEAD_DIM
    q["gm"] = jnp.asarray((grp[:, None] == grp[None, :]) / float(HEAD_DIM), dtype=BF16)

    n_odd = p["hy_w_in"].shape[0]
    q["hy_win"] = [p["hy_w_in"][i].astype(BF16) for i in range(n_odd)]
    q["hy_cw"] = [p["hy_conv_w"][i] for i in range(n_odd)]
    q["hy_cb"] = [p["hy_conv_b"][i][None, :] for i in range(n_odd)]
    q["hy_d"] = [p["hy_d"][i][:, None].astype(F32) for i in range(n_odd)]
    q["hy_wout"] = [p["hy_w_out"][i].astype(BF16) for i in range(n_odd)]
    slot = np.arange(LANES) % FILTER_WIDTH
    fr = np.linspace(1e-4, FILTER_BANDS - 1, FILTER_BANDS).astype(np.float32)
    frl = np.where((slot >= 1) & (slot <= 2 * FILTER_BANDS), fr[(slot - 1) % FILTER_BANDS], 0.0)
    ph = np.where((slot > FILTER_BANDS) & (slot <= 2 * FILTER_BANDS), 0.5 * np.pi, 0.0)
    q["frl"] = jnp.asarray(frl[None, :], dtype=F32)
    q["ph"] = jnp.asarray(ph[None, :], dtype=F32)
    q["delta"] = jnp.abs(jnp.linspace(MIN_DECAY, MAX_DECAY, D_MODEL, dtype=F32))[None, :]

    def twice(w):
        z = jnp.zeros_like(w)
        w = jnp.concatenate([jnp.concatenate([w, z], axis=1), jnp.concatenate([z, w], axis=1)], axis=0)
        hi = w.astype(BF16)
        return jnp.stack([hi, (w - hi.astype(F32)).astype(BF16)])

    filt = []
    for i in range(n_odd):
        w1p = jnp.zeros((FILTER_WIDTH, FILTER_WIDTH), F32).at[:FILTER_EMB].set(p["hy_f_w1"][i])
        row = lambda a: jnp.tile(a[None, :].astype(F32), (1, 2))
        filt.append((twice(w1p), row(p["hy_f_b1"][i]), row(p["hy_f_freq1"][i]),
                     twice(p["hy_f_w2"][i]), row(p["hy_f_b2"][i]), row(p["hy_f_freq2"][i]),
                     twice(p["hy_f_w3"][i]), row(p["hy_f_b3"][i]), row(p["hy_f_freq3"][i]),
                     twice(p["hy_f_wout"][i])))
    q["filt"] = filt
    q["bias_tab"] = _bias_table(p["rel_bias"])
    return q


def _hyena_conv(x, q, l, consts, T):
    i = l // 2
    L = x.shape[1]
    hf, hb = _filter(L, q["frl"], q["ph"], *q["filt"][i], q["delta"])
    kf = _filter_spectrum(hf, hb, q["hy_d"][i], consts)
    x0, zt = _hy_in(x, q["mix_norm"][l], q["hy_win"][i], q["hy_cw"][i], q["hy_cb"][i], T)
    return x0, _fftconv(zt, kf, consts)


def _trunk(x, q):
    B, L, D = x.shape
    consts = _dft_consts(L)
    t_mix, t_ffn = min(TOKENS_MIXER, L), min(TOKENS_FFN, L)
    for l in range(q["depth"]):
        i = l // 2
        ffn = (q["ffn_norm"][l], q["ffn_wup"][l], q["ffn_cw"][l], q["ffn_cb"][l], q["ffn_wdn"][l])
        if l % 2 == 0:
            x = _ab_layer(x, q["sink"][i], q["mix_norm"][l], q["ab_win"][i], q["ab_wout"][i],
                          q["gm"], q["kg"][i], q["bias_tab"], q["vg"][i], q["ws"][i],
                          q["bs"][i], t_mix)
            x = _ffn(x, *ffn, t_ffn)
        else:
            x0, yt = _hyena_conv(x, q, l, consts, t_mix)
            x = _hy_ffn(x, x0, yt, q["hy_wout"][i], *ffn, t_ffn)
    return x


def kernel(x_prompt, x_sample, rel_bias, mix_norm, ffn_norm, ab_w_in, q_norm, k_norm, attn_sink, sgu_v_norm, sgu_w, sgu_b, ab_w_out, hy_w_in, hy_conv_w, hy_conv_b, hy_f_w1, hy_f_b1, hy_f_freq1, hy_f_w2, hy_f_b2, hy_f_freq2, hy_f_w3, hy_f_b3, hy_f_freq3, hy_f_wout, hy_d, hy_w_out, ffn_w_up, ffn_conv_w, ffn_conv_b, ffn_w_down):
    p = dict(rel_bias=rel_bias, mix_norm=mix_norm, ffn_norm=ffn_norm, ab_w_in=ab_w_in,
             q_norm=q_norm, k_norm=k_norm, attn_sink=attn_sink, sgu_v_norm=sgu_v_norm,
             sgu_w=sgu_w, sgu_b=sgu_b, ab_w_out=ab_w_out, hy_w_in=hy_w_in, hy_conv_w=hy_conv_w,
             hy_conv_b=hy_conv_b, hy_f_w1=hy_f_w1, hy_f_b1=hy_f_b1, hy_f_freq1=hy_f_freq1,
             hy_f_w2=hy_f_w2, hy_f_b2=hy_f_b2, hy_f_freq2=hy_f_freq2, hy_f_w3=hy_f_w3,
             hy_f_b3=hy_f_b3, hy_f_freq3=hy_f_freq3, hy_f_wout=hy_f_wout, hy_d=hy_d,
             hy_w_out=hy_w_out, ffn_w_up=ffn_w_up, ffn_conv_w=ffn_conv_w, ffn_conv_b=ffn_conv_b,
             ffn_w_down=ffn_w_down)
    q = _prepare(p)
    return (_trunk(x_prompt, q), _trunk(x_sample, q))
```

```python
import functools
import math

import numpy as np
import jax
import jax.numpy as jnp
from jax import lax
from jax.experimental import pallas as pl
from jax.experimental.pallas import tpu as pltpu

F32 = jnp.float32
BF16 = jnp.bfloat16

D_MODEL = 1024
HEAD_DIM = 64
N_HEADS_A = 8
N_KV_A = 2
GQA = 4
ATTN_W = 512
WINDOW = 128
BLOCK = 128
N_BUCKETS = 32
MAX_DISTANCE = 128
SGU_W = 512
N_GROUPS_B = 8
CH_B = 64
FILTER_EMB = 33
FILTER_BANDS = 16
FILTER_WIDTH = 64
DECAY_TARGET = 1e-2
MIN_DECAY = math.log(DECAY_TARGET) / 1.5
MAX_DECAY = math.log(DECAY_TARGET) / 0.3
FFN_HIDDEN = 2816
EPS = 1e-6
NEG = -1e30

LANES = 128
SUBLANES = 8
TOKENS_FFN = 512
TOKENS_MIXER = 1024
FFN_CHUNK = 256
HY_CHUNK = 256
HALO = 8
ROW_BLOCK = 64
FFN_SLAB_SETS = FFN_HIDDEN // FFN_CHUNK
FFT_N2 = LANES
FFT_ROWS = 4096
VMEM_LIMIT = 56 * 1024 * 1024


def _gelu(x):
    t = jnp.tanh(x * (0.7978845608028654 + (0.7978845608028654 * 0.044715) * (x * x)))
    hx = 0.5 * x
    return hx + hx * t


def _rms(x, gain):
    return x * lax.rsqrt(jnp.mean(x * x, axis=-1, keepdims=True) + EPS) * gain


def _const_spec(shape):
    nd = len(shape)
    return pl.BlockSpec(shape, lambda *_: (0,) * nd, pipeline_mode=pl.Buffered(1))


def _params(n_axes):
    return pltpu.CompilerParams(dimension_semantics=("arbitrary",) * n_axes,
                                vmem_limit_bytes=VMEM_LIMIT)


def _park_slabs(s_ref, base, h):
    for s in range(h.shape[1] // LANES):
        s_ref[base + s] = h[:, s * LANES:(s + 1) * LANES]


def _conv3_slab(s_ref, slab, w, b, r0, rows):
    lo = HALO + r0
    return (s_ref[slab, lo - 1:lo - 1 + rows, :] * w[0:1]
            + s_ref[slab, lo:lo + rows, :] * w[1:2]
            + s_ref[slab, lo + 1:lo + 1 + rows, :] * w[2:3]
            + b)


def _fill_normed_ext(xe_ref, x, xp, xn, gain, rows, first, last):
    pm = jnp.where(first, 0.0, 1.0)
    nm = jnp.where(last, 0.0, 1.0)
    xe_ref[0:HALO, :] = (_rms(xp, gain) * pm).astype(BF16)
    xe_ref[HALO:HALO + rows, :] = _rms(x, gain).astype(BF16)
    xe_ref[HALO + rows:HALO + rows + HALO, :] = (_rms(xn, gain) * nm).astype(BF16)


def _halo_specs(T, L, D):
    hb = T // HALO
    last = L // HALO - 1
    return [
        pl.BlockSpec((None, T, D), lambda b, i: (b, i, 0)),
        pl.BlockSpec((None, HALO, D), lambda b, i: (b, jnp.maximum(i * hb - 1, 0), 0)),
        pl.BlockSpec((None, HALO, D), lambda b, i: (b, jnp.minimum((i + 1) * hb, last), 0)),
    ]


def _ffn_body(x, xp, xn, g_ref, wup_ref, cw_ref, cb_ref, wdn_ref, o_ref, xe_ref, a_ref, s_ref,
              T, nchunk):
    i = pl.program_id(1)
    _fill_normed_ext(xe_ref, x, xp, xn, g_ref[...], T, i == 0, i == pl.num_programs(1) - 1)
    xe = xe_ref[...]
    ns = FFN_CHUNK // LANES
    for j in range(nchunk):
        base = (j % FFN_SLAB_SETS) * 2 * ns
        for part in range(2):
            lo = part * FFN_HIDDEN + j * FFN_CHUNK
            _park_slabs(s_ref, base + part * ns,
                        jnp.dot(xe, wup_ref[:, lo:lo + FFN_CHUNK], preferred_element_type=F32))
        for s in range(ns):
            col = j * FFN_CHUNK + s * LANES
            wg, bg = cw_ref[:, col:col + LANES], cb_ref[:, col:col + LANES]
            wu = cw_ref[:, FFN_HIDDEN + col:FFN_HIDDEN + col + LANES]
            bu = cb_ref[:, FFN_HIDDEN + col:FFN_HIDDEN + col + LANES]
            for r0 in range(0, T, ROW_BLOCK):
                g = _conv3_slab(s_ref, base + s, wg, bg, r0, ROW_BLOCK)
                u = _conv3_slab(s_ref, base + ns + s, wu, bu, r0, ROW_BLOCK)
                a_ref[r0:r0 + ROW_BLOCK, col:col + LANES] = (_gelu(g) * u).astype(BF16)
    o_ref[...] = x + jnp.dot(a_ref[...], wdn_ref[...], preferred_element_type=F32)


def _ffn_kernel(x_ref, xp_ref, xn_ref, g_ref, wup_ref, cw_ref, cb_ref, wdn_ref, o_ref,
                xe_ref, a_ref, s_ref, *, T, nchunk):
    _ffn_body(x_ref[...], xp_ref[...], xn_ref[...], g_ref, wup_ref, cw_ref, cb_ref, wdn_ref,
              o_ref, xe_ref, a_ref, s_ref, T, nchunk)


def _ffn_scratch(T, D):
    return [pltpu.VMEM((T + 2 * HALO, D), BF16), pltpu.VMEM((T, FFN_HIDDEN), BF16),
            pltpu.VMEM((FFN_SLAB_SETS * 2 * FFN_CHUNK // LANES, T + 2 * HALO, LANES), F32)]


def _hy_ffn_kernel(x_ref, xp_ref, xn_ref, x0_ref, x0p_ref, x0n_ref, yt_ref, ytp_ref, ytn_ref,
                   wo_ref, g_ref, wup_ref, cw_ref, cb_ref, wdn_ref, o_ref,
                   xe_ref, a_ref, s_ref, m_ref, *, T, nchunk):
    m_ref[0:HALO, :] = (x0p_ref[...] * ytp_ref[...].T[BLOCK - HALO:, :]).astype(BF16)
    for n in range(T // BLOCK):
        lo = HALO + n * BLOCK
        m_ref[lo:lo + BLOCK, :] = (x0_ref[n * BLOCK:(n + 1) * BLOCK, :] * yt_ref[n].T).astype(BF16)
    m_ref[HALO + T:, :] = (x0n_ref[...] * ytn_ref[...].T[:HALO, :]).astype(BF16)
    mix = jnp.dot(m_ref[...], wo_ref[...], preferred_element_type=F32)
    _ffn_body(x_ref[...] + mix[HALO:HALO + T], xp_ref[...] + mix[:HALO], xn_ref[...] + mix[HALO + T:],
              g_ref, wup_ref, cw_ref, cb_ref, wdn_ref, o_ref, xe_ref, a_ref, s_ref, T, nchunk)


def _hy_ffn(x, x0, yt, wo, gain, wup_r, cw_r, cb_r, wdn, T):
    B, L, D = x.shape
    nchunk = FFN_HIDDEN // FFN_CHUNK
    nb = T // BLOCK
    last = L // BLOCK - 1
    kern = functools.partial(_hy_ffn_kernel, T=T, nchunk=nchunk)
    return pl.pallas_call(
        kern,
        grid=(B, L // T),
        in_specs=_halo_specs(T, L, D) + _halo_specs(T, L, D) + [
            pl.BlockSpec((None, nb, D, LANES), lambda b, i: (b, i, 0, 0)),
            pl.BlockSpec((None, None, D, LANES), lambda b, i: (b, jnp.maximum(i * nb - 1, 0), 0, 0)),
            pl.BlockSpec((None, None, D, LANES), lambda b, i: (b, jnp.minimum((i + 1) * nb, last), 0, 0)),
            _const_spec(wo.shape), _const_spec((1, D)), _const_spec(wup_r.shape),
            _const_spec(cw_r.shape), _const_spec(cb_r.shape), _const_spec(wdn.shape)],
        out_specs=pl.BlockSpec((None, T, D), lambda b, i: (b, i, 0)),
        out_shape=jax.ShapeDtypeStruct(x.shape, F32),
        scratch_shapes=_ffn_scratch(T, D) + [pltpu.VMEM((T + 2 * HALO, D), BF16)],
        compiler_params=_params(2),
        name="hyena_out_conv_ffn",
    )(x, x, x, x0, x0, x0, yt, yt, yt, wo, gain, wup_r, cw_r, cb_r, wdn)


def _ffn(x, gain, wup_r, cw_r, cb_r, wdn, T):
    B, L, D = x.shape
    nchunk = FFN_HIDDEN // FFN_CHUNK
    kern = functools.partial(_ffn_kernel, T=T, nchunk=nchunk)
    return pl.pallas_call(
        kern,
        grid=(B, L // T),
        in_specs=_halo_specs(T, L, D) + [
            _const_spec((1, D)), _const_spec(wup_r.shape), _const_spec(cw_r.shape),
            _const_spec(cb_r.shape), _const_spec(wdn.shape)],
        out_specs=pl.BlockSpec((None, T, D), lambda b, i: (b, i, 0)),
        out_shape=jax.ShapeDtypeStruct(x.shape, F32),
        scratch_shapes=_ffn_scratch(T, D),
        compiler_params=_params(2),
        name="conv_ffn",
    )(x, x, x, gain, wup_r, cw_r, cb_r, wdn)


def _hy_in_kernel(x_ref, xp_ref, xn_ref, g_ref, win_ref, cw_ref, cb_ref, x0_ref, zt_ref,
                  xe_ref, s_ref, *, T, nchunk):
    i = pl.program_id(1)
    _fill_normed_ext(xe_ref, x_ref[...], xp_ref[...], xn_ref[...], g_ref[...], T,
                     i == 0, i == pl.num_programs(1) - 1)
    xe = xe_ref[...]
    ns = HY_CHUNK // LANES
    for j in range(nchunk):
        base = (j % 2) * 3 * ns
        for k in range(3):
            lo = k * D_MODEL + j * HY_CHUNK
            _park_slabs(s_ref, base + k * ns,
                        jnp.dot(xe, win_ref[:, lo:lo + HY_CHUNK], preferred_element_type=F32))
        for s in range(ns):
            col = j * HY_CHUNK + s * LANES
            w = [cw_ref[:, k * D_MODEL + col:k * D_MODEL + col + LANES] for k in range(3)]
            b = [cb_ref[:, k * D_MODEL + col:k * D_MODEL + col + LANES] for k in range(3)]
            for n in range(T // BLOCK):
                r0 = n * BLOCK
                x0_ref[r0:r0 + BLOCK, col:col + LANES] = _conv3_slab(s_ref, base + s, w[0], b[0], r0, BLOCK)
                z = (_conv3_slab(s_ref, base + ns + s, w[1], b[1], r0, BLOCK)
                     * _conv3_slab(s_ref, base + 2 * ns + s, w[2], b[2], r0, BLOCK))
                zt_ref[n, col:col + LANES, :] = z.T


def _hy_in(x, gain, win_r, cw_r, cb_r, T):
    B, L, D = x.shape
    nchunk = D // HY_CHUNK
    kern = functools.partial(_hy_in_kernel, T=T, nchunk=nchunk)
    return pl.pallas_call(
        kern,
        grid=(B, L // T),
        in_specs=_halo_specs(T, L, D) + [
            _const_spec((1, D)), _const_spec(win_r.shape), _const_spec(cw_r.shape),
            _const_spec(cb_r.shape)],
        out_specs=[pl.BlockSpec((None, T, D), lambda b, i: (b, i, 0)),
                   pl.BlockSpec((None, T // BLOCK, D, LANES), lambda b, i: (b, i, 0, 0))],
        out_shape=[jax.ShapeDtypeStruct(x.shape, F32),
                   jax.ShapeDtypeStruct((B, L // BLOCK, D, LANES), F32)],
        scratch_shapes=[pltpu.VMEM((T + 2 * HALO, D), BF16),
                        pltpu.VMEM((2 * 3 * HY_CHUNK // LANES, T + 2 * HALO, LANES), F32)],
        compiler_params=_params(2),
        name="hyena_in",
    )(x, x, x, gain, win_r, cw_r, cb_r)


def _dot_f32(a, w_ref):
    a_hi = a.astype(BF16)
    a_lo = (a - a_hi.astype(F32)).astype(BF16)
    w_hi, w_lo = w_ref[0], w_ref[1]
    return (jnp.dot(a_hi, w_hi, preferred_element_type=F32)
            + jnp.dot(a_lo, w_hi, preferred_element_type=F32)
            + jnp.dot(a_hi, w_lo, preferred_element_type=F32))


def _filter_kernel(frl_ref, ph_ref, w1_ref, b1_ref, f1_ref, w2_ref, b2_ref, f2_ref, w3_ref, b3_ref,
                   f3_ref, wout_ref, delta_ref, hf_ref, hb_ref, *, L, R):
    hr = R // 2
    ja = pl.program_id(0) * R + lax.broadcasted_iota(jnp.int32, (hr, 1), 0)
    jb = ja + hr
    lane = lax.broadcasted_iota(jnp.int32, (hr, LANES), 1)
    j = jnp.where(lane < FILTER_WIDTH, ja, jb).astype(F32)
    t = j / float(L - 1)
    w = (2.0 * math.pi) * j / float(L)
    feats = jnp.where(lane % FILTER_WIDTH == 0, t, jnp.cos(w * frl_ref[...] + ph_ref[...]))
    h = jnp.sin(f1_ref[...] * (_dot_f32(feats, w1_ref) + b1_ref[...]))
    h = jnp.sin(f2_ref[...] * (_dot_f32(h, w2_ref) + b2_ref[...]))
    h = jnp.sin(f3_ref[...] * (_dot_f32(h, w3_ref) + b3_ref[...]))
    h = _dot_f32(h, wout_ref)
    for half, jidx in enumerate((ja, jb)):
        decay = jnp.exp(-(jidx.astype(F32) / float(L - 1)) * delta_ref[...])
        base = half * 2 * D_MODEL
        hf = h[:, base:base + D_MODEL] * decay
        hb = jnp.where(jidx == 0, 0.0, h[:, base + D_MODEL:base + 2 * D_MODEL] * decay)
        for n in range(hr // BLOCK):
            blk = half * (hr // BLOCK) + n
            hf_ref[blk] = hf[n * BLOCK:(n + 1) * BLOCK, :].T
            hb_ref[blk] = hb[n * BLOCK:(n + 1) * BLOCK, :].T


def _filter(L, frl, ph, w1, b1, f1, w2, b2, f2, w3, b3, f3, wout, delta):
    R = 512
    kern = functools.partial(_filter_kernel, L=L, R=R)
    consts = [frl, ph, w1, b1, f1, w2, b2, f2, w3, b3, f3, wout, delta]
    out_spec = pl.BlockSpec((R // BLOCK, D_MODEL, LANES), lambda i: (i, 0, 0))
    shape = jax.ShapeDtypeStruct((L // BLOCK, D_MODEL, LANES), F32)
    return pl.pallas_call(
        kern,
        grid=(L // R,),
        in_specs=[_const_spec(a.shape) for a in consts],
        out_specs=[out_spec, out_spec],
        out_shape=[shape, shape],
        compiler_params=_params(1),
        name="hyena_filter",
    )(*consts)


def _dft_consts(L):
    N = 2 * L
    N2 = FFT_N2
    N1 = N // N2
    h = N1 // 2
    a1 = -2.0 * np.pi * np.outer(np.arange(N1), np.arange(N1)) / N1
    f1r, f1i = np.cos(a1), np.sin(a1)
    a2 = -2.0 * np.pi * np.outer(np.arange(N2), np.arange(N2)) / N2
    f2r, f2i = np.cos(a2), np.sin(a2)
    at = -2.0 * np.pi * np.outer(np.arange(N1), np.arange(N2)) / N
    m1 = np.block([[f1r[:, :h], -f1i[:, :h]], [f1i[:, :h], f1r[:, :h]]])
    m1f = np.concatenate([f1r[:, :h], f1i[:, :h]], axis=0)
    m3 = np.block([[f1r[:h], f1i[:h]], [-f1i[:h], f1r[:h]]])
    c = lambda a: jnp.asarray(a, dtype=BF16)
    f = lambda a: jnp.asarray(a, dtype=F32)
    return dict(N1=N1, m1=c(m1), m1f=c(m1f), m3=c(m3),
                r2f=c(np.block([[f2r, f2i], [-f2i, f2r]])),
                r2i=c(np.block([[f2r, -f2i], [f2i, f2r]])),
                twr=f(np.cos(at)), twi=f(np.sin(at)))


def _cmul(ar, ai, br, bi):
    return ar * br - ai * bi, ar * bi + ai * br


def _slab8(ref, idx, c0, n):
    v = ref[idx + (slice(None), slice(c0, c0 + SUBLANES), slice(None))]
    return v.reshape(n * SUBLANES, LANES)


def _chan_rows(s_ref, base, c, n):
    return s_ref[pl.ds(base + c, n, stride=SUBLANES), :]


def _fft_kernel(x_ref, kf_ref, m1_ref, m3_ref, twr_ref, twi_ref, r2f_ref, r2i_ref, o_ref,
                *scratch, N1, nc):
    h = N1 // 2
    h8 = h * SUBLANES
    nsub = nc // SUBLANES
    twr, twi = twr_ref[...], twi_ref[...]
    cols = []
    for s8 in range(nsub):
        sin_ref = scratch[s8]
        sin_ref[0:h8] = _slab8(x_ref, (0,), s8 * SUBLANES, h)
        sin_ref[h8:2 * h8] = _slab8(x_ref, (1,), s8 * SUBLANES, h)
        cols += [jnp.concatenate([_chan_rows(sin_ref, 0, c, h), _chan_rows(sin_ref, h8, c, h)], axis=0)
                 for c in range(SUBLANES)]
    a = jnp.dot(m1_ref[...], jnp.concatenate(cols, axis=1).astype(BF16),
                preferred_element_type=F32)
    lhs = []
    for c in range(nc):
        tr, ti = _cmul(a[:N1, c * LANES:(c + 1) * LANES], a[N1:, c * LANES:(c + 1) * LANES], twr, twi)
        lhs.append(jnp.concatenate([tr, ti], axis=1))
    lhs = jnp.concatenate(lhs, axis=0).astype(BF16)
    X = jnp.dot(lhs, r2f_ref[...], preferred_element_type=F32)
    yr, yi = _cmul(X[:, :LANES], X[:, LANES:],
                   kf_ref[0].reshape(nc * N1, LANES), kf_ref[1].reshape(nc * N1, LANES))
    Bm = jnp.dot(jnp.concatenate([yr, yi], axis=1).astype(BF16), r2i_ref[...],
                 preferred_element_type=F32)
    re_cols, im_cols = [], []
    for c in range(nc):
        br, bi = Bm[c * N1:(c + 1) * N1, :LANES], Bm[c * N1:(c + 1) * N1, LANES:]
        re_cols.append(br * twr + bi * twi)
        im_cols.append(bi * twr - br * twi)
    rhs = jnp.concatenate([jnp.concatenate(re_cols, axis=1),
                           jnp.concatenate(im_cols, axis=1)], axis=0).astype(BF16)
    y = jnp.dot(m3_ref[...], rhs, preferred_element_type=F32)
    for s8 in range(nsub):
        c0 = s8 * SUBLANES
        sout_ref = scratch[nsub + s8]
        for c in range(SUBLANES):
            lo = (c0 + c) * LANES
            sout_ref[pl.ds(c, h, stride=SUBLANES), :] = y[:h, lo:lo + LANES]
            sout_ref[pl.ds(h8 + c, h, stride=SUBLANES), :] = y[h:, lo:lo + LANES]
        o_ref[0, :, c0:c0 + SUBLANES, :] = sout_ref[0:h8].reshape(h, SUBLANES, LANES)
        o_ref[1, :, c0:c0 + SUBLANES, :] = sout_ref[h8:2 * h8].reshape(h, SUBLANES, LANES)


def _fftconv(zt, kf, c):
    B, h, C, _ = zt.shape
    N1 = c["N1"]
    nc = min(FFT_ROWS // N1, C)
    kern = functools.partial(_fft_kernel, N1=N1, nc=nc)
    blk = pl.BlockSpec((2, h, nc, LANES), lambda cb, p: (p, 0, cb, 0))
    consts = [c["m1"], c["m3"], c["twr"], c["twi"], c["r2f"], c["r2i"]]
    return pl.pallas_call(
        kern,
        grid=(C // nc, B // 2),
        in_specs=[blk, pl.BlockSpec((2, nc, N1, LANES), lambda cb, p: (0, cb, 0, 0))]
        + [_const_spec(a.shape) for a in consts],
        out_specs=blk,
        out_shape=jax.ShapeDtypeStruct(zt.shape, F32),
        scratch_shapes=[pltpu.VMEM((N1 * SUBLANES, LANES), F32)] * (2 * nc // SUBLANES),
        compiler_params=_params(2),
        name="fft_conv",
    )(zt, kf, *consts)


def _fspec_kernel(hf_ref, hb_ref, d_ref, m1f_ref, twr_ref, twi_ref, r2f_ref, o_ref,
                  *scratch, N1, nc, scale):
    h = N1 // 2
    h8 = h * SUBLANES
    twr, twi = twr_ref[...], twi_ref[...]
    cols = []
    for s8 in range(nc // SUBLANES):
        s_ref = scratch[s8]
        s_ref[0:h8] = _slab8(hf_ref, (), s8 * SUBLANES, h)
        s_ref[h8:2 * h8] = _slab8(hb_ref, (), s8 * SUBLANES, h)
        for c in range(SUBLANES):
            cols += [_chan_rows(s_ref, 0, c, h), _chan_rows(s_ref, h8, c, h)]
    a = jnp.dot(m1f_ref[...], jnp.concatenate(cols, axis=1).astype(BF16),
                preferred_element_type=F32)
    lhs = []
    for k in range(2 * nc):
        tr, ti = _cmul(a[:N1, k * LANES:(k + 1) * LANES], a[N1:, k * LANES:(k + 1) * LANES], twr, twi)
        lhs.append(jnp.concatenate([tr, ti], axis=1))
    lhs = jnp.concatenate(lhs, axis=0).astype(BF16)
    X = jnp.dot(lhs, r2f_ref[...], preferred_element_type=F32)
    for c in range(nc):
        xf = X[(2 * c) * N1:(2 * c + 1) * N1]
        xb = X[(2 * c + 1) * N1:(2 * c + 2) * N1]
        d = d_ref[c:c + 1, :]
        o_ref[0, c] = (xf[:, :LANES] + xb[:, :LANES] + d) * scale
        o_ref[1, c] = (xf[:, LANES:] - xb[:, LANES:]) * scale


def _filter_spectrum(hf, hb, d, c):
    h, C, _ = hf.shape
    N1 = c["N1"]
    nc = min(FFT_ROWS // N1, C)
    kern = functools.partial(_fspec_kernel, N1=N1, nc=nc, scale=1.0 / (N1 * FFT_N2))
    blk = pl.BlockSpec((h, nc, LANES), lambda cb: (0, cb, 0))
    consts = [c["m1f"], c["twr"], c["twi"], c["r2f"]]
    return pl.pallas_call(
        kern,
        grid=(C // nc,),
        in_specs=[blk, blk, pl.BlockSpec((nc, 1), lambda cb: (cb, 0))]
        + [_const_spec(a.shape) for a in consts],
        out_specs=pl.BlockSpec((2, nc, N1, LANES), lambda cb: (0, cb, 0, 0)),
        out_shape=jax.ShapeDtypeStruct((2, C, N1, LANES), F32),
        scratch_shapes=[pltpu.VMEM((N1 * SUBLANES, LANES), F32)] * (nc // SUBLANES),
        compiler_params=_params(1),
        name="filter_spectrum",
    )(hf, hb, d, *consts)


def _t5_bucket(rel):
    half = N_BUCKETS // 2
    max_exact = half // 2
    ret = jnp.where(rel > 0, half, 0)
    n = jnp.abs(rel)
    nf = jnp.maximum(n, 1).astype(jnp.float32)
    large = max_exact + (jnp.log(nf / max_exact) / math.log(MAX_DISTANCE / max_exact)
                         * (half - max_exact)).astype(jnp.int32)
    large = jnp.minimum(large, half - 1)
    return ret + jnp.where(n < max_exact, n, large)


LOG2E = 1.4426950408889634


def _bias_kernel(rb_ref, bucket_ref, rel_ref, o_ref):
    bucket = bucket_ref[...]
    rel = rel_ref[...]
    key = lax.broadcasted_iota(jnp.int32, bucket.shape, 0)
    for h in range(N_HEADS_A):
        acc = jnp.zeros(bucket.shape, F32)
        for b in range(N_BUCKETS):
            acc = jnp.where(bucket == b, rb_ref[b, h], acc)
        acc = acc * LOG2E
        for v in range(4):
            ok = jnp.abs(rel) <= WINDOW
            if v & 1:
                ok = ok & (key >= BLOCK)
            if v & 2:
                ok = ok & (key < 2 * BLOCK)
            o_ref[v, h] = jnp.where(ok, acc, NEG)


def _bias_table(rel_bias):
    rel = (jnp.arange(3 * BLOCK)[:, None] - BLOCK) - jnp.arange(BLOCK)[None, :]
    rel = rel.astype(jnp.int32)
    bucket = _t5_bucket(rel).astype(jnp.int32)
    return pl.pallas_call(
        _bias_kernel,
        in_specs=[pl.BlockSpec(memory_space=pltpu.SMEM),
                  pl.BlockSpec(memory_space=pltpu.VMEM), pl.BlockSpec(memory_space=pltpu.VMEM)],
        out_specs=pl.BlockSpec(memory_space=pltpu.VMEM),
        out_shape=jax.ShapeDtypeStruct((4, N_HEADS_A, 3 * BLOCK, BLOCK), F32),
        name="rel_bias_table",
    )(rel_bias.astype(F32), bucket, rel)


def _group_ms(v, gmat):
    return jnp.dot((v * v).astype(BF16), gmat, preferred_element_type=F32)


def _ab_kernel(sink_ref, x_ref, xp_ref, xn_ref, g_ref, win_ref, wout_ref, gm_ref,
               kg_ref, bias_ref, vg_ref, ws_ref, bs_ref, o_ref, *, T):
    i = pl.program_id(1)
    nb = T // BLOCK
    nblocks = pl.num_programs(1) * nb
    gain = g_ref[...]
    x = x_ref[...]
    proj = jnp.dot(_rms(x, gain).astype(BF16), win_ref[...], preferred_element_type=F32)
    xh = jnp.concatenate([_rms(xp_ref[...], gain), _rms(xn_ref[...], gain)], axis=0)
    kvh = jnp.dot(xh.astype(BF16), win_ref[:, ATTN_W:ATTN_W + 2 * BLOCK],
                  preferred_element_type=F32)
    gm = gm_ref[...]
    gm_k = gm[:BLOCK, :BLOCK]

    k_all = jnp.concatenate([kvh[:BLOCK, :BLOCK], proj[:, ATTN_W:ATTN_W + BLOCK],
                             kvh[BLOCK:, :BLOCK]], axis=0)
    v_all = jnp.concatenate([kvh[:BLOCK, BLOCK:], proj[:, ATTN_W + BLOCK:ATTN_W + 2 * BLOCK],
                             kvh[BLOCK:, BLOCK:]], axis=0)
    kn = k_all * lax.rsqrt(_group_ms(k_all, gm_k) + EPS) * kg_ref[...]

    su = _gelu(proj[:, ATTN_W + 2 * BLOCK:ATTN_W + 2 * BLOCK + SGU_W])
    sv = _gelu(proj[:, ATTN_W + 2 * BLOCK + SGU_W:])
    svn = (sv * lax.rsqrt(_group_ms(sv, gm) + EPS) * vg_ref[...]).astype(BF16)
    low = lax.broadcasted_iota(jnp.int32, (1, BLOCK * nb), 1) % BLOCK < CH_B
    slabs = []
    for j in range(SGU_W // BLOCK):
        rhs = jnp.concatenate([svn[n * BLOCK:(n + 1) * BLOCK, j * BLOCK:(j + 1) * BLOCK]
                               for n in range(nb)], axis=1)
        a = jnp.dot(ws_ref[2 * j], rhs, preferred_element_type=F32)
        b = jnp.dot(ws_ref[2 * j + 1], rhs, preferred_element_type=F32)
        slabs.append(jnp.where(low, a, b))
    mixed = jnp.concatenate(
        [jnp.concatenate([slabs[j][:, n * BLOCK:(n + 1) * BLOCK] for j in range(SGU_W // BLOCK)],
                         axis=1) + bs_ref[...] for n in range(nb)], axis=0)
    sgu_out = jnp.dot((su * mixed).astype(BF16), wout_ref[ATTN_W:, :], preferred_element_type=F32)

    q_t = proj[:, :ATTN_W].T
    qt = []
    for h in range(N_HEADS_A):
        qh = q_t[h * HEAD_DIM:(h + 1) * HEAD_DIM]
        qt.append((qh * lax.rsqrt(jnp.mean(qh * qh, axis=0, keepdims=True) + EPS)).astype(BF16))
    qt = jnp.concatenate(qt, axis=0)
    vt = v_all.T.astype(BF16)
    knb = kn.astype(BF16)
    zeros_q = jnp.zeros((HEAD_DIM, GQA * BLOCK), BF16)
    ones_rows = jnp.ones((2 * SUBLANES, 3 * BLOCK), BF16)
    units = [(n, hk) for n in range(nb) for hk in range(N_KV_A)]
    sks = [jnp.concatenate([jnp.full((1, BLOCK), sink_ref[hk * GQA + g] * LOG2E, F32)
                            for g in range(GQA)], axis=1) for hk in range(N_KV_A)]
    scores = []
    for n, hk in units:
        blk = i * nb + n
        variant = jnp.where(blk == 0, 1, 0) + jnp.where(blk == nblocks - 1, 2, 0)
        qh = jnp.concatenate(
            [qt[(hk * GQA + g) * HEAD_DIM:(hk * GQA + g + 1) * HEAD_DIM, n * BLOCK:(n + 1) * BLOCK]
             for g in range(GQA)], axis=1)
        qz = jnp.concatenate([qh, zeros_q] if hk == 0 else [zeros_q, qh], axis=0)
        s = jnp.dot(knb[n * BLOCK:n * BLOCK + 3 * BLOCK, :], qz,
                    preferred_element_type=F32)
        scores.append(s + jnp.concatenate([bias_ref[variant, hk * GQA + g] for g in range(GQA)], axis=1))
    probs = []
    for (n, hk), s in zip(units, scores):
        m = jnp.maximum(jnp.max(s, axis=0, keepdims=True), sks[hk])
        probs.append((jnp.exp2(s - m).astype(BF16), jnp.exp2(sks[hk] - m)))
    outs = []
    for (n, hk), (p, psink) in zip(units, probs):
        lhs = jnp.concatenate([vt[hk * HEAD_DIM:(hk + 1) * HEAD_DIM, n * BLOCK:n * BLOCK + 3 * BLOCK],
                               ones_rows], axis=0)
        pv = jnp.dot(lhs, p, preferred_element_type=F32)
        outs.append(pv[:HEAD_DIM] / (pv[HEAD_DIM:HEAD_DIM + 1] + psink))
    out_cols = []
    for n in range(nb):
        out_cols.append(jnp.concatenate(
            [outs[n * N_KV_A + hk][:, g * BLOCK:(g + 1) * BLOCK]
             for hk in range(N_KV_A) for g in range(GQA)], axis=0))
    attn = jnp.concatenate(out_cols, axis=1).T.astype(BF16)
    o_ref[...] = x + sgu_out + jnp.dot(attn, wout_ref[:ATTN_W, :], preferred_element_type=F32)


def _ab_layer(x, sink, gain, win, wout, gm, kg, bias_tab, vg, ws, bs, T):
    B, L, D = x.shape
    hb = T // BLOCK
    last = L // BLOCK - 1
    kern = functools.partial(_ab_kernel, T=T)
    consts = [gain, win, wout, gm, kg, bias_tab, vg, ws, bs]
    return pl.pallas_call(
        kern,
        grid=(B, L // T),
        in_specs=[pl.BlockSpec(memory_space=pltpu.SMEM),
                  pl.BlockSpec((None, T, D), lambda b, i: (b, i, 0)),
                  pl.BlockSpec((None, BLOCK, D), lambda b, i: (b, jnp.maximum(i * hb - 1, 0), 0)),
                  pl.BlockSpec((None, BLOCK, D), lambda b, i: (b, jnp.minimum((i + 1) * hb, last), 0)),
                  ] + [_const_spec(a.shape) for a in consts],
        out_specs=pl.BlockSpec((None, T, D), lambda b, i: (b, i, 0)),
        out_shape=jax.ShapeDtypeStruct(x.shape, F32),
        compiler_params=_params(2),
        name="attn_sgu_mixer",
    )(sink, x, x, x, *consts)


def _prepare(p):
    depth = p["mix_norm"].shape[0]
    q = dict(depth=depth)
    q["mix_norm"] = [p["mix_norm"][l][None, :] for l in range(depth)]
    q["ffn_norm"] = [p["ffn_norm"][l][None, :] for l in range(depth)]
    q["ffn_wup"] = [p["ffn_w_up"][l].astype(BF16) for l in range(depth)]
    q["ffn_cw"] = [p["ffn_conv_w"][l] for l in range(depth)]
    q["ffn_cb"] = [p["ffn_conv_b"][l][None, :] for l in range(depth)]
    q["ffn_wdn"] = [p["ffn_w_down"][l].astype(BF16) for l in range(depth)]

    n_even = p["ab_w_in"].shape[0]
    q["ab_win"] = [p["ab_w_in"][i].astype(BF16) for i in range(n_even)]
    q["ab_wout"] = [p["ab_w_out"][i].astype(BF16) for i in range(n_even)]
    q["kg"] = [jnp.tile(p["k_norm"][i] * p["q_norm"][i] * (HEAD_DIM ** -0.5 * LOG2E), N_KV_A)[None, :]
               for i in range(n_even)]
    q["sink"] = [p["attn_sink"][i].astype(F32) for i in range(n_even)]
    q["vg"] = [p["sgu_v_norm"][i].reshape(1, SGU_W) for i in range(n_even)]
    q["ws"] = [p["sgu_w"][i].astype(BF16) for i in range(n_even)]
    q["bs"] = [jnp.repeat(p["sgu_b"][i].T, CH_B, axis=1) for i in range(n_even)]
    grp = np.arange(ATTN_W) // HEAD_DIM
    q["gm"] = jnp.asarray((grp[:, None] == grp[None, :]) / float(HEAD_DIM), dtype=BF16)

    n_odd = p["hy_w_in"].shape[0]
    q["hy_win"] = [p["hy_w_in"][i].astype(BF16) for i in range(n_odd)]
    q["hy_cw"] = [p["hy_conv_w"][i] for i in range(n_odd)]
    q["hy_cb"] = [p["hy_conv_b"][i][None, :] for i in range(n_odd)]
    q["hy_d"] = [p["hy_d"][i][:, None].astype(F32) for i in range(n_odd)]
    q["hy_wout"] = [p["hy_w_out"][i].astype(BF16) for i in range(n_odd)]
    slot = np.arange(LANES) % FILTER_WIDTH
    fr = np.linspace(1e-4, FILTER_BANDS - 1, FILTER_BANDS).astype(np.float32)
    frl = np.where((slot >= 1) & (slot <= 2 * FILTER_BANDS), fr[(slot - 1) % FILTER_BANDS], 0.0)
    ph = np.where((slot > FILTER_BANDS) & (slot <= 2 * FILTER_BANDS), 0.5 * np.pi, 0.0)
    q["frl"] = jnp.asarray(frl[None, :], dtype=F32)
    q["ph"] = jnp.asarray(ph[None, :], dtype=F32)
    q["delta"] = jnp.abs(jnp.linspace(MIN_DECAY, MAX_DECAY, D_MODEL, dtype=F32))[None, :]

    def twice(w):
        z = jnp.zeros_like(w)
        w = jnp.concatenate([jnp.concatenate([w, z], axis=1), jnp.concatenate([z, w], axis=1)], axis=0)
        hi = w.astype(BF16)
        return jnp.stack([hi, (w - hi.astype(F32)).astype(BF16)])

    filt = []
    for i in range(n_odd):
        w1p = jnp.zeros((FILTER_WIDTH, FILTER_WIDTH), F32).at[:FILTER_EMB].set(p["hy_f_w1"][i])
        row = lambda a: jnp.tile(a[None, :].astype(F32), (1, 2))
        filt.append((twice(w1p), row(p["hy_f_b1"][i]), row(p["hy_f_freq1"][i]),
                     twice(p["hy_f_w2"][i]), row(p["hy_f_b2"][i]), row(p["hy_f_freq2"][i]),
                     twice(p["hy_f_w3"][i]), row(p["hy_f_b3"][i]), row(p["hy_f_freq3"][i]),
                     twice(p["hy_f_wout"][i])))
    q["filt"] = filt
    q["bias_tab"] = _bias_table(p["rel_bias"])
    return q


def _hyena_conv(x, q, l, consts, T):
    i = l // 2
    L = x.shape[1]
    hf, hb = _filter(L, q["frl"], q["ph"], *q["filt"][i], q["delta"])
    kf = _filter_spectrum(hf, hb, q["hy_d"][i], consts)
    x0, zt = _hy_in(x, q["mix_norm"][l], q["hy_win"][i], q["hy_cw"][i], q["hy_cb"][i], T)
    return x0, _fftconv(zt, kf, consts)


def _trunk(x, q):
    B, L, D = x.shape
    consts = _dft_consts(L)
    t_mix, t_ffn = min(TOKENS_MIXER, L), min(TOKENS_FFN, L)
    for l in range(q["depth"]):
        i = l // 2
        ffn = (q["ffn_norm"][l], q["ffn_wup"][l], q["ffn_cw"][l], q["ffn_cb"][l], q["ffn_wdn"][l])
        if l % 2 == 0:
            x = _ab_layer(x, q["sink"][i], q["mix_norm"][l], q["ab_win"][i], q["ab_wout"][i],
                          q["gm"], q["kg"][i], q["bias_tab"], q["vg"][i], q["ws"][i],
                          q["bs"][i], t_mix)
            x = _ffn(x, *ffn, t_ffn)
        else:
            x0, yt = _hyena_conv(x, q, l, consts, t_mix)
            x = _hy_ffn(x, x0, yt, q["hy_wout"][i], *ffn, t_ffn)
    return x


def kernel(x_prompt, x_sample, rel_bias, mix_norm, ffn_norm, ab_w_in, q_norm, k_norm, attn_sink, sgu_v_norm, sgu_w, sgu_b, ab_w_out, hy_w_in, hy_conv_w, hy_conv_b, hy_f_w1, hy_f_b1, hy_f_freq1, hy_f_w2, hy_f_b2, hy_f_freq2, hy_f_w3, hy_f_b3, hy_f_freq3, hy_f_wout, hy_d, hy_w_out, ffn_w_up, ffn_conv_w, ffn_conv_b, ffn_w_down):
    p = dict(rel_bias=rel_bias, mix_norm=mix_norm, ffn_norm=ffn_norm, ab_w_in=ab_w_in,
             q_norm=q_norm, k_norm=k_norm, attn_sink=attn_sink, sgu_v_norm=sgu_v_norm,
             sgu_w=sgu_w, sgu_b=sgu_b, ab_w_out=ab_w_out, hy_w_in=hy_w_in, hy_conv_w=hy_conv_w,
             hy_conv_b=hy_conv_b, hy_f_w1=hy_f_w1, hy_f_b1=hy_f_b1, hy_f_freq1=hy_f_freq1,
             hy_f_w2=hy_f_w2, hy_f_b2=hy_f_b2, hy_f_freq2=hy_f_freq2, hy_f_w3=hy_f_w3,
             hy_f_b3=hy_f_b3, hy_f_freq3=hy_f_freq3, hy_f_wout=hy_f_wout, hy_d=hy_d,
             hy_w_out=hy_w_out, ffn_w_up=ffn_w_up, ffn_conv_w=ffn_conv_w, ffn_conv_b=ffn_conv_b,
             ffn_w_down=ffn_w_down)
    q = _prepare(p)
    return (_trunk(x_prompt, q), _trunk(x_sample, q))
```

```python
import functools
import math

import numpy as np
import jax
import jax.numpy as jnp
from jax import lax
from jax.experimental import pallas as pl
from jax.experimental.pallas import tpu as pltpu

F32 = jnp.float32
BF16 = jnp.bfloat16

D_MODEL = 1024
HEAD_DIM = 64
N_HEADS_A = 8
N_KV_A = 2
GQA = 4
ATTN_W = 512
WINDOW = 128
BLOCK = 128
N_BUCKETS = 32
MAX_DISTANCE = 128
SGU_W = 512
N_GROUPS_B = 8
CH_B = 64
FILTER_EMB = 33
FILTER_BANDS = 16
FILTER_WIDTH = 64
DECAY_TARGET = 1e-2
MIN_DECAY = math.log(DECAY_TARGET) / 1.5
MAX_DECAY = math.log(DECAY_TARGET) / 0.3
FFN_HIDDEN = 2816
EPS = 1e-6
NEG = -1e30

LANES = 128
SUBLANES = 8
TOKENS_FFN = 512
TOKENS_MIXER = 1024
FFN_CHUNK = 256
HY_CHUNK = 256
HALO = 8
ROW_BLOCK = 64
FFN_SLAB_SETS = FFN_HIDDEN // FFN_CHUNK
FFT_N2 = LANES
FFT_ROWS = 4096
VMEM_LIMIT = 56 * 1024 * 1024


def _gelu(x):
    t = jnp.tanh(x * (0.7978845608028654 + (0.7978845608028654 * 0.044715) * (x * x)))
    hx = 0.5 * x
    return hx + hx * t


def _rms(x, gain):
    return x * lax.rsqrt(jnp.mean(x * x, axis=-1, keepdims=True) + EPS) * gain


def _const_spec(shape):
    nd = len(shape)
    return pl.BlockSpec(shape, lambda *_: (0,) * nd, pipeline_mode=pl.Buffered(1))


def _params(n_axes):
    return pltpu.CompilerParams(dimension_semantics=("arbitrary",) * n_axes,
                                vmem_limit_bytes=VMEM_LIMIT)


def _park_slabs(s_ref, base, h):
    for s in range(h.shape[1] // LANES):
        s_ref[base + s] = h[:, s * LANES:(s + 1) * LANES]


def _conv3_slab(s_ref, slab, w, b, r0, rows):
    lo = HALO + r0
    return (s_ref[slab, lo - 1:lo - 1 + rows, :] * w[0:1]
            + s_ref[slab, lo:lo + rows, :] * w[1:2]
            + s_ref[slab, lo + 1:lo + 1 + rows, :] * w[2:3]
            + b)


def _fill_normed_ext(xe_ref, x, xp, xn, gain, rows, first, last):
    pm = jnp.where(first, 0.0, 1.0)
    nm = jnp.where(last, 0.0, 1.0)
    xe_ref[0:HALO, :] = (_rms(xp, gain) * pm).astype(BF16)
    xe_ref[HALO:HALO + rows, :] = _rms(x, gain).astype(BF16)
    xe_ref[HALO + rows:HALO + rows + HALO, :] = (_rms(xn, gain) * nm).astype(BF16)


def _halo_specs(T, L, D):
    hb = T // HALO
    last = L // HALO - 1
    return [
        pl.BlockSpec((None, T, D), lambda b, i: (b, i, 0)),
        pl.BlockSpec((None, HALO, D), lambda b, i: (b, jnp.maximum(i * hb - 1, 0), 0)),
        pl.BlockSpec((None, HALO, D), lambda b, i: (b, jnp.minimum((i + 1) * hb, last), 0)),
    ]


def _ffn_body(x, xp, xn, g_ref, wup_ref, cw_ref, cb_ref, wdn_ref, o_ref, xe_ref, a_ref, s_ref,
              T, nchunk):
    i = pl.program_id(1)
    _fill_normed_ext(xe_ref, x, xp, xn, g_ref[...], T, i == 0, i == pl.num_programs(1) - 1)
    xe = xe_ref[...]
    ns = FFN_CHUNK // LANES
    for j in range(nchunk):
        base = (j % FFN_SLAB_SETS) * 2 * ns
        for part in range(2):
            lo = part * FFN_HIDDEN + j * FFN_CHUNK
            _park_slabs(s_ref, base + part * ns,
                        jnp.dot(xe, wup_ref[:, lo:lo + FFN_CHUNK], preferred_element_type=F32))
        for s in range(ns):
            col = j * FFN_CHUNK + s * LANES
            wg, bg = cw_ref[:, col:col + LANES], cb_ref[:, col:col + LANES]
            wu = cw_ref[:, FFN_HIDDEN + col:FFN_HIDDEN + col + LANES]
            bu = cb_ref[:, FFN_HIDDEN + col:FFN_HIDDEN + col + LANES]
            for r0 in range(0, T, ROW_BLOCK):
                g = _conv3_slab(s_ref, base + s, wg, bg, r0, ROW_BLOCK)
                u = _conv3_slab(s_ref, base + ns + s, wu, bu, r0, ROW_BLOCK)
                a_ref[r0:r0 + ROW_BLOCK, col:col + LANES] = (_gelu(g) * u).astype(BF16)
    o_ref[...] = x + jnp.dot(a_ref[...], wdn_ref[...], preferred_element_type=F32)


def _ffn_kernel(x_ref, xp_ref, xn_ref, g_ref, wup_ref, cw_ref, cb_ref, wdn_ref, o_ref,
                xe_ref, a_ref, s_ref, *, T, nchunk):
    _ffn_body(x_ref[...], xp_ref[...], xn_ref[...], g_ref, wup_ref, cw_ref, cb_ref, wdn_ref,
              o_ref, xe_ref, a_ref, s_ref, T, nchunk)


def _ffn_scratch(T, D):
    return [pltpu.VMEM((T + 2 * HALO, D), BF16), pltpu.VMEM((T, FFN_HIDDEN), BF16),
            pltpu.VMEM((FFN_SLAB_SETS * 2 * FFN_CHUNK // LANES, T + 2 * HALO, LANES), F32)]


def _hy_ffn_kernel(x_ref, xp_ref, xn_ref, x0_ref, x0p_ref, x0n_ref, yt_ref, ytp_ref, ytn_ref,
                   wo_ref, g_ref, wup_ref, cw_ref, cb_ref, wdn_ref, o_ref,
                   xe_ref, a_ref, s_ref, m_ref, *, T, nchunk):
    m_ref[0:HALO, :] = (x0p_ref[...] * ytp_ref[...].T[BLOCK - HALO:, :]).astype(BF16)
    for n in range(T // BLOCK):
        lo = HALO + n * BLOCK
        m_ref[lo:lo + BLOCK, :] = (x0_ref[n * BLOCK:(n + 1) * BLOCK, :] * yt_ref[n].T).astype(BF16)
    m_ref[HALO + T:, :] = (x0n_ref[...] * ytn_ref[...].T[:HALO, :]).astype(BF16)
    mix = jnp.dot(m_ref[...], wo_ref[...], preferred_element_type=F32)
    _ffn_body(x_ref[...] + mix[HALO:HALO + T], xp_ref[...] + mix[:HALO], xn_ref[...] + mix[HALO + T:],
              g_ref, wup_ref, cw_ref, cb_ref, wdn_ref, o_ref, xe_ref, a_ref, s_ref, T, nchunk)


def _hy_ffn(x, x0, yt, wo, gain, wup_r, cw_r, cb_r, wdn, T):
    B, L, D = x.shape
    nchunk = FFN_HIDDEN // FFN_CHUNK
    nb = T // BLOCK
    last = L // BLOCK - 1
    kern = functools.partial(_hy_ffn_kernel, T=T, nchunk=nchunk)
    return pl.pallas_call(
        kern,
        grid=(B, L // T),
        in_specs=_halo_specs(T, L, D) + _halo_specs(T, L, D) + [
            pl.BlockSpec((None, nb, D, LANES), lambda b, i: (b, i, 0, 0)),
            pl.BlockSpec((None, None, D, LANES), lambda b, i: (b, jnp.maximum(i * nb - 1, 0), 0, 0)),
            pl.BlockSpec((None, None, D, LANES), lambda b, i: (b, jnp.minimum((i + 1) * nb, last), 0, 0)),
            _const_spec(wo.shape), _const_spec((1, D)), _const_spec(wup_r.shape),
            _const_spec(cw_r.shape), _const_spec(cb_r.shape), _const_spec(wdn.shape)],
        out_specs=pl.BlockSpec((None, T, D), lambda b, i: (b, i, 0)),
        out_shape=jax.ShapeDtypeStruct(x.shape, F32),
        scratch_shapes=_ffn_scratch(T, D) + [pltpu.VMEM((T + 2 * HALO, D), BF16)],
        compiler_params=_params(2),
        name="hyena_out_conv_ffn",
    )(x, x, x, x0, x0, x0, yt, yt, yt, wo, gain, wup_r, cw_r, cb_r, wdn)


def _ffn(x, gain, wup_r, cw_r, cb_r, wdn, T):
    B, L, D = x.shape
    nchunk = FFN_HIDDEN // FFN_CHUNK
    kern = functools.partial(_ffn_kernel, T=T, nchunk=nchunk)
    return pl.pallas_call(
        kern,
        grid=(B, L // T),
        in_specs=_halo_specs(T, L, D) + [
            _const_spec((1, D)), _const_spec(wup_r.shape), _const_spec(cw_r.shape),
            _const_spec(cb_r.shape), _const_spec(wdn.shape)],
        out_specs=pl.BlockSpec((None, T, D), lambda b, i: (b, i, 0)),
        out_shape=jax.ShapeDtypeStruct(x.shape, F32),
        scratch_shapes=_ffn_scratch(T, D),
        compiler_params=_params(2),
        name="conv_ffn",
    )(x, x, x, gain, wup_r, cw_r, cb_r, wdn)


def _hy_in_kernel(x_ref, xp_ref, xn_ref, g_ref, win_ref, cw_ref, cb_ref, x0_ref, zt_ref,
                  xe_ref, s_ref, *, T, nchunk):
    i = pl.program_id(1)
    _fill_normed_ext(xe_ref, x_ref[...], xp_ref[...], xn_ref[...], g_ref[...], T,
                     i == 0, i == pl.num_programs(1) - 1)
    xe = xe_ref[...]
    ns = HY_CHUNK // LANES
    for j in range(nchunk):
        base = (j % 2) * 3 * ns
        for k in range(3):
            lo = k * D_MODEL + j * HY_CHUNK
            _park_slabs(s_ref, base + k * ns,
                        jnp.dot(xe, win_ref[:, lo:lo + HY_CHUNK], preferred_element_type=F32))
        for s in range(ns):
            col = j * HY_CHUNK + s * LANES
            w = [cw_ref[:, k * D_MODEL + col:k * D_MODEL + col + LANES] for k in range(3)]
            b = [cb_ref[:, k * D_MODEL + col:k * D_MODEL + col + LANES] for k in range(3)]
            for n in range(T // BLOCK):
                r0 = n * BLOCK
                x0_ref[r0:r0 + BLOCK, col:col + LANES] = _conv3_slab(s_ref, base + s, w[0], b[0], r0, BLOCK)
                z = (_conv3_slab(s_ref, base + ns + s, w[1], b[1], r0, BLOCK)
                     * _conv3_slab(s_ref, base + 2 * ns + s, w[2], b[2], r0, BLOCK))
                zt_ref[n, col:col + LANES, :] = z.T


def _hy_in(x, gain, win_r, cw_r, cb_r, T):
    B, L, D = x.shape
    nchunk = D // HY_CHUNK
    kern = functools.partial(_hy_in_kernel, T=T, nchunk=nchunk)
    return pl.pallas_call(
        kern,
        grid=(B, L // T),
        in_specs=_halo_specs(T, L, D) + [
            _const_spec((1, D)), _const_spec(win_r.shape), _const_spec(cw_r.shape),
            _const_spec(cb_r.shape)],
        out_specs=[pl.BlockSpec((None, T, D), lambda b, i: (b, i, 0)),
                   pl.BlockSpec((None, T // BLOCK, D, LANES), lambda b, i: (b, i, 0, 0))],
        out_shape=[jax.ShapeDtypeStruct(x.shape, F32),
                   jax.ShapeDtypeStruct((B, L // BLOCK, D, LANES), F32)],
        scratch_shapes=[pltpu.VMEM((T + 2 * HALO, D), BF16),
                        pltpu.VMEM((2 * 3 * HY_CHUNK // LANES, T + 2 * HALO, LANES), F32)],
        compiler_params=_params(2),
        name="hyena_in",
    )(x, x, x, gain, win_r, cw_r, cb_r)


def _dot_f32(a, w_ref):
    a_hi = a.astype(BF16)
    a_lo = (a - a_hi.astype(F32)).astype(BF16)
    w_hi, w_lo = w_ref[0], w_ref[1]
    return (jnp.dot(a_hi, w_hi, preferred_element_type=F32)
            + jnp.dot(a_lo, w_hi, preferred_element_type=F32)
            + jnp.dot(a_hi, w_lo, preferred_element_type=F32))


def _filter_kernel(frl_ref, ph_ref, w1_ref, b1_ref, f1_ref, w2_ref, b2_ref, f2_ref, w3_ref, b3_ref,
                   f3_ref, wout_ref, delta_ref, hf_ref, hb_ref, *, L, R):
    hr = R // 2
    ja = pl.program_id(0) * R + lax.broadcasted_iota(jnp.int32, (hr, 1), 0)
    jb = ja + hr
    lane = lax.broadcasted_iota(jnp.int32, (hr, LANES), 1)
    j = jnp.where(lane < FILTER_WIDTH, ja, jb).astype(F32)
    t = j / float(L - 1)
    w = (2.0 * math.pi) * j / float(L)
    feats = jnp.where(lane % FILTER_WIDTH == 0, t, jnp.cos(w * frl_ref[...] + ph_ref[...]))
    h = jnp.sin(f1_ref[...] * (_dot_f32(feats, w1_ref) + b1_ref[...]))
    h = jnp.sin(f2_ref[...] * (_dot_f32(h, w2_ref) + b2_ref[...]))
    h = jnp.sin(f3_ref[...] * (_dot_f32(h, w3_ref) + b3_ref[...]))
    h = _dot_f32(h, wout_ref)
    for half, jidx in enumerate((ja, jb)):
        decay = jnp.exp(-(jidx.astype(F32) / float(L - 1)) * delta_ref[...])
        base = half * 2 * D_MODEL
        hf = h[:, base:base + D_MODEL] * decay
        hb = jnp.where(jidx == 0, 0.0, h[:, base + D_MODEL:base + 2 * D_MODEL] * decay)
        for n in range(hr // BLOCK):
            blk = half * (hr // BLOCK) + n
            hf_ref[blk] = hf[n * BLOCK:(n + 1) * BLOCK, :].T
            hb_ref[blk] = hb[n * BLOCK:(n + 1) * BLOCK, :].T


def _filter(L, frl, ph, w1, b1, f1, w2, b2, f2, w3, b3, f3, wout, delta):
    R = 512
    kern = functools.partial(_filter_kernel, L=L, R=R)
    consts = [frl, ph, w1, b1, f1, w2, b2, f2, w3, b3, f3, wout, delta]
    out_spec = pl.BlockSpec((R // BLOCK, D_MODEL, LANES), lambda i: (i, 0, 0))
    shape = jax.ShapeDtypeStruct((L // BLOCK, D_MODEL, LANES), F32)
    return pl.pallas_call(
        kern,
        grid=(L // R,),
        in_specs=[_const_spec(a.shape) for a in consts],
        out_specs=[out_spec, out_spec],
        out_shape=[shape, shape],
        compiler_params=_params(1),
        name="hyena_filter",
    )(*consts)


def _dft_consts(L):
    N = 2 * L
    N2 = FFT_N2
    N1 = N // N2
    h = N1 // 2
    a1 = -2.0 * np.pi * np.outer(np.arange(N1), np.arange(N1)) / N1
    f1r, f1i = np.cos(a1), np.sin(a1)
    a2 = -2.0 * np.pi * np.outer(np.arange(N2), np.arange(N2)) / N2
    f2r, f2i = np.cos(a2), np.sin(a2)
    at = -2.0 * np.pi * np.outer(np.arange(N1), np.arange(N2)) / N
    m1 = np.block([[f1r[:, :h], -f1i[:, :h]], [f1i[:, :h], f1r[:, :h]]])
    m1f = np.concatenate([f1r[:, :h], f1i[:, :h]], axis=0)
    m3 = np.block([[f1r[:h], f1i[:h]], [-f1i[:h], f1r[:h]]])
    c = lambda a: jnp.asarray(a, dtype=BF16)
    f = lambda a: jnp.asarray(a, dtype=F32)
    return dict(N1=N1, m1=c(m1), m1f=c(m1f), m3=c(m3),
                r2f=c(np.block([[f2r, f2i], [-f2i, f2r]])),
                r2i=c(np.block([[f2r, -f2i], [f2i, f2r]])),
                twr=f(np.cos(at)), twi=f(np.sin(at)))


def _cmul(ar, ai, br, bi):
    return ar * br - ai * bi, ar * bi + ai * br


def _slab8(ref, idx, c0, n):
    v = ref[idx + (slice(None), slice(c0, c0 + SUBLANES), slice(None))]
    return v.reshape(n * SUBLANES, LANES)


def _chan_rows(s_ref, base, c, n):
    return s_ref[pl.ds(base + c, n, stride=SUBLANES), :]


def _fft_kernel(x_ref, kf_ref, m1_ref, m3_ref, twr_ref, twi_ref, r2f_ref, r2i_ref, o_ref,
                *scratch, N1, nc):
    h = N1 // 2
    h8 = h * SUBLANES
    nsub = nc // SUBLANES
    twr, twi = twr_ref[...], twi_ref[...]
    cols = []
    for s8 in range(nsub):
        sin_ref = scratch[s8]
        sin_ref[0:h8] = _slab8(x_ref, (0,), s8 * SUBLANES, h)
        sin_ref[h8:2 * h8] = _slab8(x_ref, (1,), s8 * SUBLANES, h)
        cols += [jnp.concatenate([_chan_rows(sin_ref, 0, c, h), _chan_rows(sin_ref, h8, c, h)], axis=0)
                 for c in range(SUBLANES)]
    twr, twi = twr.astype(BF16), twi.astype(BF16)
    a = jnp.dot(m1_ref[...], jnp.concatenate(cols, axis=1).astype(BF16),
                preferred_element_type=F32).astype(BF16)
    lhs = []
    for c in range(nc):
        tr, ti = _cmul(a[:N1, c * LANES:(c + 1) * LANES], a[N1:, c * LANES:(c + 1) * LANES], twr, twi)
        lhs.append(jnp.concatenate([tr, ti], axis=1))
    lhs = jnp.concatenate(lhs, axis=0)
    X = jnp.dot(lhs, r2f_ref[...], preferred_element_type=F32)
    yr, yi = _cmul(X[:, :LANES], X[:, LANES:],
                   kf_ref[0].reshape(nc * N1, LANES), kf_ref[1].reshape(nc * N1, LANES))
    Bm = jnp.dot(jnp.concatenate([yr, yi], axis=1).astype(BF16), r2i_ref[...],
                 preferred_element_type=F32).astype(BF16)
    re_cols, im_cols = [], []
    for c in range(nc):
        br, bi = Bm[c * N1:(c + 1) * N1, :LANES], Bm[c * N1:(c + 1) * N1, LANES:]
        re_cols.append(br * twr + bi * twi)
        im_cols.append(bi * twr - br * twi)
    rhs = jnp.concatenate([jnp.concatenate(re_cols, axis=1),
                           jnp.concatenate(im_cols, axis=1)], axis=0)
    y = jnp.dot(m3_ref[...], rhs, preferred_element_type=F32)
    for s8 in range(nsub):
        c0 = s8 * SUBLANES
        sout_ref = scratch[nsub + s8]
        for c in range(SUBLANES):
            lo = (c0 + c) * LANES
            sout_ref[pl.ds(c, h, stride=SUBLANES), :] = y[:h, lo:lo + LANES]
            sout_ref[pl.ds(h8 + c, h, stride=SUBLANES), :] = y[h:, lo:lo + LANES]
        o_ref[0, :, c0:c0 + SUBLANES, :] = sout_ref[0:h8].reshape(h, SUBLANES, LANES)
        o_ref[1, :, c0:c0 + SUBLANES, :] = sout_ref[h8:2 * h8].reshape(h, SUBLANES, LANES)


def _fftconv(zt, kf, c):
    B, h, C, _ = zt.shape
    N1 = c["N1"]
    nc = min(FFT_ROWS // N1, C)
    kern = functools.partial(_fft_kernel, N1=N1, nc=nc)
    blk = pl.BlockSpec((2, h, nc, LANES), lambda cb, p: (p, 0, cb, 0))
    consts = [c["m1"], c["m3"], c["twr"], c["twi"], c["r2f"], c["r2i"]]
    return pl.pallas_call(
        kern,
        grid=(C // nc, B // 2),
        in_specs=[blk, pl.BlockSpec((2, nc, N1, LANES), lambda cb, p: (0, cb, 0, 0))]
        + [_const_spec(a.shape) for a in consts],
        out_specs=blk,
        out_shape=jax.ShapeDtypeStruct(zt.shape, F32),
        scratch_shapes=[pltpu.VMEM((N1 * SUBLANES, LANES), F32)] * (2 * nc // SUBLANES),
        compiler_params=_params(2),
        name="fft_conv",
    )(zt, kf, *consts)


def _fspec_kernel(hf_ref, hb_ref, d_ref, m1f_ref, twr_ref, twi_ref, r2f_ref, o_ref,
                  *scratch, N1, nc, scale):
    h = N1 // 2
    h8 = h * SUBLANES
    twr, twi = twr_ref[...], twi_ref[...]
    cols = []
    for s8 in range(nc // SUBLANES):
        s_ref = scratch[s8]
        s_ref[0:h8] = _slab8(hf_ref, (), s8 * SUBLANES, h)
        s_ref[h8:2 * h8] = _slab8(hb_ref, (), s8 * SUBLANES, h)
        for c in range(SUBLANES):
            cols += [_chan_rows(s_ref, 0, c, h), _chan_rows(s_ref, h8, c, h)]
    twr, twi = twr.astype(BF16), twi.astype(BF16)
    a = jnp.dot(m1f_ref[...], jnp.concatenate(cols, axis=1).astype(BF16),
                preferred_element_type=F32).astype(BF16)
    lhs = []
    for k in range(2 * nc):
        tr, ti = _cmul(a[:N1, k * LANES:(k + 1) * LANES], a[N1:, k * LANES:(k + 1) * LANES], twr, twi)
        lhs.append(jnp.concatenate([tr, ti], axis=1))
    lhs = jnp.concatenate(lhs, axis=0)
    X = jnp.dot(lhs, r2f_ref[...], preferred_element_type=F32)
    for c in range(nc):
        xf = X[(2 * c) * N1:(2 * c + 1) * N1]
        xb = X[(2 * c + 1) * N1:(2 * c + 2) * N1]
        d = d_ref[c:c + 1, :]
        o_ref[0, c] = (xf[:, :LANES] + xb[:, :LANES] + d) * scale
        o_ref[1, c] = (xf[:, LANES:] - xb[:, LANES:]) * scale


def _filter_spectrum(hf, hb, d, c):
    h, C, _ = hf.shape
    N1 = c["N1"]
    nc = min(FFT_ROWS // N1, C)
    kern = functools.partial(_fspec_kernel, N1=N1, nc=nc, scale=1.0 / (N1 * FFT_N2))
    blk = pl.BlockSpec((h, nc, LANES), lambda cb: (0, cb, 0))
    consts = [c["m1f"], c["twr"], c["twi"], c["r2f"]]
    return pl.pallas_call(
        kern,
        grid=(C // nc,),
        in_specs=[blk, blk, pl.BlockSpec((nc, 1), lambda cb: (cb, 0))]
        + [_const_spec(a.shape) for a in consts],
        out_specs=pl.BlockSpec((2, nc, N1, LANES), lambda cb: (0, cb, 0, 0)),
        out_shape=jax.ShapeDtypeStruct((2, C, N1, LANES), F32),
        scratch_shapes=[pltpu.VMEM((N1 * SUBLANES, LANES), F32)] * (nc // SUBLANES),
        compiler_params=_params(1),
        name="filter_spectrum",
    )(hf, hb, d, *consts)


def _t5_bucket(rel):
    half = N_BUCKETS // 2
    max_exact = half // 2
    ret = jnp.where(rel > 0, half, 0)
    n = jnp.abs(rel)
    nf = jnp.maximum(n, 1).astype(jnp.float32)
    large = max_exact + (jnp.log(nf / max_exact) / math.log(MAX_DISTANCE / max_exact)
                         * (half - max_exact)).astype(jnp.int32)
    large = jnp.minimum(large, half - 1)
    return ret + jnp.where(n < max_exact, n, large)


LOG2E = 1.4426950408889634


def _bias_kernel(rb_ref, bucket_ref, rel_ref, o_ref):
    bucket = bucket_ref[...]
    rel = rel_ref[...]
    key = lax.broadcasted_iota(jnp.int32, bucket.shape, 0)
    for h in range(N_HEADS_A):
        acc = jnp.zeros(bucket.shape, F32)
        for b in range(N_BUCKETS):
            acc = jnp.where(bucket == b, rb_ref[b, h], acc)
        acc = acc * LOG2E
        for v in range(4):
            ok = jnp.abs(rel) <= WINDOW
            if v & 1:
                ok = ok & (key >= BLOCK)
            if v & 2:
                ok = ok & (key < 2 * BLOCK)
            o_ref[v, h] = jnp.where(ok, acc, NEG)


def _bias_table(rel_bias):
    rel = (jnp.arange(3 * BLOCK)[:, None] - BLOCK) - jnp.arange(BLOCK)[None, :]
    rel = rel.astype(jnp.int32)
    bucket = _t5_bucket(rel).astype(jnp.int32)
    return pl.pallas_call(
        _bias_kernel,
        in_specs=[pl.BlockSpec(memory_space=pltpu.SMEM),
                  pl.BlockSpec(memory_space=pltpu.VMEM), pl.BlockSpec(memory_space=pltpu.VMEM)],
        out_specs=pl.BlockSpec(memory_space=pltpu.VMEM),
        out_shape=jax.ShapeDtypeStruct((4, N_HEADS_A, 3 * BLOCK, BLOCK), F32),
        name="rel_bias_table",
    )(rel_bias.astype(F32), bucket, rel)


def _group_ms(v, gmat):
    return jnp.dot((v * v).astype(BF16), gmat, preferred_element_type=F32)


def _ab_kernel(sink_ref, x_ref, xp_ref, xn_ref, g_ref, win_ref, wout_ref, gm_ref,
               kg_ref, bias_ref, vg_ref, ws_ref, bs_ref, o_ref, *, T):
    i = pl.program_id(1)
    nb = T // BLOCK
    nblocks = pl.num_programs(1) * nb
    gain = g_ref[...]
    x = x_ref[...]
    proj = jnp.dot(_rms(x, gain).astype(BF16), win_ref[...], preferred_element_type=F32)
    xh = jnp.concatenate([_rms(xp_ref[...], gain), _rms(xn_ref[...], gain)], axis=0)
    kvh = jnp.dot(xh.astype(BF16), win_ref[:, ATTN_W:ATTN_W + 2 * BLOCK],
                  preferred_element_type=F32)
    gm = gm_ref[...]
    gm_k = gm[:BLOCK, :BLOCK]

    k_all = jnp.concatenate([kvh[:BLOCK, :BLOCK], proj[:, ATTN_W:ATTN_W + BLOCK],
                             kvh[BLOCK:, :BLOCK]], axis=0)
    v_all = jnp.concatenate([kvh[:BLOCK, BLOCK:], proj[:, ATTN_W + BLOCK:ATTN_W + 2 * BLOCK],
                             kvh[BLOCK:, BLOCK:]], axis=0)
    kn = k_all * lax.rsqrt(_group_ms(k_all, gm_k) + EPS) * kg_ref[...]

    su = _gelu(proj[:, ATTN_W + 2 * BLOCK:ATTN_W + 2 * BLOCK + SGU_W])
    sv = _gelu(proj[:, ATTN_W + 2 * BLOCK + SGU_W:])
    svn = (sv * lax.rsqrt(_group_ms(sv, gm) + EPS) * vg_ref[...]).astype(BF16)
    low = lax.broadcasted_iota(jnp.int32, (1, BLOCK * nb), 1) % BLOCK < CH_B
    slabs = []
    for j in range(SGU_W // BLOCK):
        rhs = jnp.concatenate([svn[n * BLOCK:(n + 1) * BLOCK, j * BLOCK:(j + 1) * BLOCK]
                               for n in range(nb)], axis=1)
        a = jnp.dot(ws_ref[2 * j], rhs, preferred_element_type=F32)
        b = jnp.dot(ws_ref[2 * j + 1], rhs, preferred_element_type=F32)
        slabs.append(jnp.where(low, a, b))
    mixed = jnp.concatenate(
        [jnp.concatenate([slabs[j][:, n * BLOCK:(n + 1) * BLOCK] for j in range(SGU_W // BLOCK)],
                         axis=1) + bs_ref[...] for n in range(nb)], axis=0)
    sgu_out = jnp.dot((su * mixed).astype(BF16), wout_ref[ATTN_W:, :], preferred_element_type=F32)

    q_t = proj[:, :ATTN_W].T
    qt = []
    for h in range(N_HEADS_A):
        qh = q_t[h * HEAD_DIM:(h + 1) * HEAD_DIM]
        qt.append((qh * lax.rsqrt(jnp.mean(qh * qh, axis=0, keepdims=True) + EPS)).astype(BF16))
    qt = jnp.concatenate(qt, axis=0)
    vt = v_all.T.astype(BF16)
    knb = kn.astype(BF16)
    zeros_q = jnp.zeros((HEAD_DIM, GQA * BLOCK), BF16)
    ones_rows = jnp.ones((2 * SUBLANES, 3 * BLOCK), BF16)
    units = [(n, hk) for n in range(nb) for hk in range(N_KV_A)]
    sks = [jnp.concatenate([jnp.full((1, BLOCK), sink_ref[hk * GQA + g] * LOG2E, F32)
                            for g in range(GQA)], axis=1) for hk in range(N_KV_A)]
    scores = []
    for n, hk in units:
        blk = i * nb + n
        variant = jnp.where(blk == 0, 1, 0) + jnp.where(blk == nblocks - 1, 2, 0)
        qh = jnp.concatenate(
            [qt[(hk * GQA + g) * HEAD_DIM:(hk * GQA + g + 1) * HEAD_DIM, n * BLOCK:(n + 1) * BLOCK]
             for g in range(GQA)], axis=1)
        qz = jnp.concatenate([qh, zeros_q] if hk == 0 else [zeros_q, qh], axis=0)
        s = jnp.dot(knb[n * BLOCK:n * BLOCK + 3 * BLOCK, :], qz,
                    preferred_element_type=F32)
        scores.append(s + jnp.concatenate([bias_ref[variant, hk * GQA + g] for g in range(GQA)], axis=1))
    probs = []
    for (n, hk), s in zip(units, scores):
        m = jnp.maximum(jnp.max(s, axis=0, keepdims=True), sks[hk])
        probs.append((jnp.exp2(s - m).astype(BF16), jnp.exp2(sks[hk] - m)))
    outs = []
    for (n, hk), (p, psink) in zip(units, probs):
        lhs = jnp.concatenate([vt[hk * HEAD_DIM:(hk + 1) * HEAD_DIM, n * BLOCK:n * BLOCK + 3 * BLOCK],
                               ones_rows], axis=0)
        pv = jnp.dot(lhs, p, preferred_element_type=F32)
        outs.append(pv[:HEAD_DIM] / (pv[HEAD_DIM:HEAD_DIM + 1] + psink))
    out_cols = []
    for n in range(nb):
        out_cols.append(jnp.concatenate(
            [outs[n * N_KV_A + hk][:, g * BLOCK:(g + 1) * BLOCK]
             for hk in range(N_KV_A) for g in range(GQA)], axis=0))
    attn = jnp.concatenate(out_cols, axis=1).T.astype(BF16)
    o_ref[...] = x + sgu_out + jnp.dot(attn, wout_ref[:ATTN_W, :], preferred_element_type=F32)


def _ab_layer(x, sink, gain, win, wout, gm, kg, bias_tab, vg, ws, bs, T):
    B, L, D = x.shape
    hb = T // BLOCK
    last = L // BLOCK - 1
    kern = functools.partial(_ab_kernel, T=T)
    consts = [gain, win, wout, gm, kg, bias_tab, vg, ws, bs]
    return pl.pallas_call(
        kern,
        grid=(B, L // T),
        in_specs=[pl.BlockSpec(memory_space=pltpu.SMEM),
                  pl.BlockSpec((None, T, D), lambda b, i: (b, i, 0)),
                  pl.BlockSpec((None, BLOCK, D), lambda b, i: (b, jnp.maximum(i * hb - 1, 0), 0)),
                  pl.BlockSpec((None, BLOCK, D), lambda b, i: (b, jnp.minimum((i + 1) * hb, last), 0)),
                  ] + [_const_spec(a.shape) for a in consts],
        out_specs=pl.BlockSpec((None, T, D), lambda b, i: (b, i, 0)),
        out_shape=jax.ShapeDtypeStruct(x.shape, F32),
        compiler_params=_params(2),
        name="attn_sgu_mixer",
    )(sink, x, x, x, *consts)


def _prepare(p):
    depth = p["mix_norm"].shape[0]
    q = dict(depth=depth)
    q["mix_norm"] = [p["mix_norm"][l][None, :] for l in range(depth)]
    q["ffn_norm"] = [p["ffn_norm"][l][None, :] for l in range(depth)]
    q["ffn_wup"] = [p["ffn_w_up"][l].astype(BF16) for l in range(depth)]
    q["ffn_cw"] = [p["ffn_conv_w"][l] for l in range(depth)]
    q["ffn_cb"] = [p["ffn_conv_b"][l][None, :] for l in range(depth)]
    q["ffn_wdn"] = [p["ffn_w_down"][l].astype(BF16) for l in range(depth)]

    n_even = p["ab_w_in"].shape[0]
    q["ab_win"] = [p["ab_w_in"][i].astype(BF16) for i in range(n_even)]
    q["ab_wout"] = [p["ab_w_out"][i].astype(BF16) for i in range(n_even)]
    q["kg"] = [jnp.tile(p["k_norm"][i] * p["q_norm"][i] * (HEAD_DIM ** -0.5 * LOG2E), N_KV_A)[None, :]
               for i in range(n_even)]
    q["sink"] = [p["attn_sink"][i].astype(F32) for i in range(n_even)]
    q["vg"] = [p["sgu_v_norm"][i].reshape(1, SGU_W) for i in range(n_even)]
    q["ws"] = [p["sgu_w"][i].astype(BF16) for i in range(n_even)]
    q["bs"] = [jnp.repeat(p["sgu_b"][i].T, CH_B, axis=1) for i in range(n_even)]
    grp = np.arange(ATTN_W) // HEAD_DIM
    q["gm"] = jnp.asarray((grp[:, None] == grp[None, :]) / float(HEAD_DIM), dtype=BF16)

    n_odd = p["hy_w_in"].shape[0]
    q["hy_win"] = [p["hy_w_in"][i].astype(BF16) for i in range(n_odd)]
    q["hy_cw"] = [p["hy_conv_w"][i] for i in range(n_odd)]
    q["hy_cb"] = [p["hy_conv_b"][i][None, :] for i in range(n_odd)]
    q["hy_d"] = [p["hy_d"][i][:, None].astype(F32) for i in range(n_odd)]
    q["hy_wout"] = [p["hy_w_out"][i].astype(BF16) for i in range(n_odd)]
    slot = np.arange(LANES) % FILTER_WIDTH
    fr = np.linspace(1e-4, FILTER_BANDS - 1, FILTER_BANDS).astype(np.float32)
    frl = np.where((slot >= 1) & (slot <= 2 * FILTER_BANDS), fr[(slot - 1) % FILTER_BANDS], 0.0)
    ph = np.where((slot > FILTER_BANDS) & (slot <= 2 * FILTER_BANDS), 0.5 * np.pi, 0.0)
    q["frl"] = jnp.asarray(frl[None, :], dtype=F32)
    q["ph"] = jnp.asarray(ph[None, :], dtype=F32)
    q["delta"] = jnp.abs(jnp.linspace(MIN_DECAY, MAX_DECAY, D_MODEL, dtype=F32))[None, :]

    def twice(w):
        z = jnp.zeros_like(w)
        w = jnp.concatenate([jnp.concatenate([w, z], axis=1), jnp.concatenate([z, w], axis=1)], axis=0)
        hi = w.astype(BF16)
        return jnp.stack([hi, (w - hi.astype(F32)).astype(BF16)])

    filt = []
    for i in range(n_odd):
        w1p = jnp.zeros((FILTER_WIDTH, FILTER_WIDTH), F32).at[:FILTER_EMB].set(p["hy_f_w1"][i])
        row = lambda a: jnp.tile(a[None, :].astype(F32), (1, 2))
        filt.append((twice(w1p), row(p["hy_f_b1"][i]), row(p["hy_f_freq1"][i]),
                     twice(p["hy_f_w2"][i]), row(p["hy_f_b2"][i]), row(p["hy_f_freq2"][i]),
                     twice(p["hy_f_w3"][i]), row(p["hy_f_b3"][i]), row(p["hy_f_freq3"][i]),
                     twice(p["hy_f_wout"][i])))
    q["filt"] = filt
    q["bias_tab"] = _bias_table(p["rel_bias"])
    return q


def _hyena_conv(x, q, l, consts, T):
    i = l // 2
    L = x.shape[1]
    hf, hb = _filter(L, q["frl"], q["ph"], *q["filt"][i], q["delta"])
    kf = _filter_spectrum(hf, hb, q["hy_d"][i], consts)
    x0, zt = _hy_in(x, q["mix_norm"][l], q["hy_win"][i], q["hy_cw"][i], q["hy_cb"][i], T)
    return x0, _fftconv(zt, kf, consts)


def _trunk(x, q):
    B, L, D = x.shape
    consts = _dft_consts(L)
    t_mix, t_ffn = min(TOKENS_MIXER, L), min(TOKENS_FFN, L)
    for l in range(q["depth"]):
        i = l // 2
        ffn = (q["ffn_norm"][l], q["ffn_wup"][l], q["ffn_cw"][l], q["ffn_cb"][l], q["ffn_wdn"][l])
        if l % 2 == 0:
            x = _ab_layer(x, q["sink"][i], q["mix_norm"][l], q["ab_win"][i], q["ab_wout"][i],
                          q["gm"], q["kg"][i], q["bias_tab"], q["vg"][i], q["ws"][i],
                          q["bs"][i], t_mix)
            x = _ffn(x, *ffn, t_ffn)
        else:
            x0, yt = _hyena_conv(x, q, l, consts, t_mix)
            x = _hy_ffn(x, x0, yt, q["hy_wout"][i], *ffn, t_ffn)
    return x


def kernel(x_prompt, x_sample, rel_bias, mix_norm, ffn_norm, ab_w_in, q_norm, k_norm, attn_sink, sgu_v_norm, sgu_w, sgu_b, ab_w_out, hy_w_in, hy_conv_w, hy_conv_b, hy_f_w1, hy_f_b1, hy_f_freq1, hy_f_w2, hy_f_b2, hy_f_freq2, hy_f_w3, hy_f_b3, hy_f_freq3, hy_f_wout, hy_d, hy_w_out, ffn_w_up, ffn_conv_w, ffn_conv_b, ffn_w_down):
    p = dict(rel_bias=rel_bias, mix_norm=mix_norm, ffn_norm=ffn_norm, ab_w_in=ab_w_in,
             q_norm=q_norm, k_norm=k_norm, attn_sink=attn_sink, sgu_v_norm=sgu_v_norm,
             sgu_w=sgu_w, sgu_b=sgu_b, ab_w_out=ab_w_out, hy_w_in=hy_w_in, hy_conv_w=hy_conv_w,
             hy_conv_b=hy_conv_b, hy_f_w1=hy_f_w1, hy_f_b1=hy_f_b1, hy_f_freq1=hy_f_freq1,
             hy_f_w2=hy_f_w2, hy_f_b2=hy_f_b2, hy_f_freq2=hy_f_freq2, hy_f_w3=hy_f_w3,
             hy_f_b3=hy_f_b3, hy_f_freq3=hy_f_freq3, hy_f_wout=hy_f_wout, hy_d=hy_d,
             hy_w_out=hy_w_out, ffn_w_up=ffn_w_up, ffn_conv_w=ffn_conv_w, ffn_conv_b=ffn_conv_b,
             ffn_w_down=ffn_w_down)
    q = _prepare(p)
    return (_trunk(x_prompt, q), _trunk(x_sample, q))
```

```python
import functools
import math

import numpy as np
import jax
import jax.numpy as jnp
from jax import lax
from jax.experimental import pallas as pl
from jax.experimental.pallas import tpu as pltpu

F32 = jnp.float32
BF16 = jnp.bfloat16

D_MODEL = 1024
HEAD_DIM = 64
N_HEADS_A = 8
N_KV_A = 2
GQA = 4
ATTN_W = 512
WINDOW = 128
BLOCK = 128
N_BUCKETS = 32
MAX_DISTANCE = 128
SGU_W = 512
N_GROUPS_B = 8
CH_B = 64
FILTER_EMB = 33
FILTER_BANDS = 16
FILTER_WIDTH = 64
DECAY_TARGET = 1e-2
MIN_DECAY = math.log(DECAY_TARGET) / 1.5
MAX_DECAY = math.log(DECAY_TARGET) / 0.3
FFN_HIDDEN = 2816
EPS = 1e-6
NEG = -1e30

LANES = 128
SUBLANES = 8
TOKENS_FFN = 512
TOKENS_MIXER = 1024
FFN_CHUNK = 256
HY_CHUNK = 256
HALO = 8
ROW_BLOCK = 64
FFN_SLAB_SETS = FFN_HIDDEN // FFN_CHUNK
FFT_N2 = LANES
FFT_ROWS = 4096
VMEM_LIMIT = 56 * 1024 * 1024


def _gelu(x):
    t = jnp.tanh(x * (0.7978845608028654 + (0.7978845608028654 * 0.044715) * (x * x)))
    hx = 0.5 * x
    return hx + hx * t


def _rms(x, gain):
    return x * lax.rsqrt(jnp.mean(x * x, axis=-1, keepdims=True) + EPS) * gain


def _const_spec(shape):
    nd = len(shape)
    return pl.BlockSpec(shape, lambda *_: (0,) * nd, pipeline_mode=pl.Buffered(1))


def _params(n_axes):
    return pltpu.CompilerParams(dimension_semantics=("arbitrary",) * n_axes,
                                vmem_limit_bytes=VMEM_LIMIT)


def _park_slabs(s_ref, base, h):
    for s in range(h.shape[1] // LANES):
        s_ref[base + s] = h[:, s * LANES:(s + 1) * LANES]


def _conv3_slab(s_ref, slab, w, b, r0, rows):
    lo = HALO + r0
    return (s_ref[slab, lo - 1:lo - 1 + rows, :] * w[0:1]
            + s_ref[slab, lo:lo + rows, :] * w[1:2]
            + s_ref[slab, lo + 1:lo + 1 + rows, :] * w[2:3]
            + b)


def _fill_normed_ext(xe_ref, x, xp, xn, gain, rows, first, last):
    pm = jnp.where(first, 0.0, 1.0)
    nm = jnp.where(last, 0.0, 1.0)
    xe_ref[0:HALO, :] = (_rms(xp, gain) * pm).astype(BF16)
    xe_ref[HALO:HALO + rows, :] = _rms(x, gain).astype(BF16)
    xe_ref[HALO + rows:HALO + rows + HALO, :] = (_rms(xn, gain) * nm).astype(BF16)


def _halo_specs(T, L, D):
    hb = T // HALO
    last = L // HALO - 1
    return [
        pl.BlockSpec((None, T, D), lambda b, i: (b, i, 0)),
        pl.BlockSpec((None, HALO, D), lambda b, i: (b, jnp.maximum(i * hb - 1, 0), 0)),
        pl.BlockSpec((None, HALO, D), lambda b, i: (b, jnp.minimum((i + 1) * hb, last), 0)),
    ]


def _ffn_body(x, xp, xn, g_ref, wup_ref, cw_ref, cb_ref, wdn_ref, o_ref, xe_ref, a_ref, s_ref,
              T, nchunk):
    i = pl.program_id(1)
    _fill_normed_ext(xe_ref, x, xp, xn, g_ref[...], T, i == 0, i == pl.num_programs(1) - 1)
    xe = xe_ref[...]
    ns = FFN_CHUNK // LANES
    for j in range(nchunk):
        base = (j % FFN_SLAB_SETS) * 2 * ns
        for part in range(2):
            lo = part * FFN_HIDDEN + j * FFN_CHUNK
            _park_slabs(s_ref, base + part * ns,
                        jnp.dot(xe, wup_ref[:, lo:lo + FFN_CHUNK], preferred_element_type=F32))
        for s in range(ns):
            col = j * FFN_CHUNK + s * LANES
            wg, bg = cw_ref[:, col:col + LANES], cb_ref[:, col:col + LANES]
            wu = cw_ref[:, FFN_HIDDEN + col:FFN_HIDDEN + col + LANES]
            bu = cb_ref[:, FFN_HIDDEN + col:FFN_HIDDEN + col + LANES]
            for r0 in range(0, T, ROW_BLOCK):
                g = _conv3_slab(s_ref, base + s, wg, bg, r0, ROW_BLOCK)
                u = _conv3_slab(s_ref, base + ns + s, wu, bu, r0, ROW_BLOCK)
                a_ref[r0:r0 + ROW_BLOCK, col:col + LANES] = (_gelu(g) * u).astype(BF16)
    o_ref[...] = x + jnp.dot(a_ref[...], wdn_ref[...], preferred_element_type=F32)


def _ffn_kernel(x_ref, xp_ref, xn_ref, g_ref, wup_ref, cw_ref, cb_ref, wdn_ref, o_ref,
                xe_ref, a_ref, s_ref, *, T, nchunk):
    _ffn_body(x_ref[...], xp_ref[...], xn_ref[...], g_ref, wup_ref, cw_ref, cb_ref, wdn_ref,
              o_ref, xe_ref, a_ref, s_ref, T, nchunk)


def _ffn_scratch(T, D):
    return [pltpu.VMEM((T + 2 * HALO, D), BF16), pltpu.VMEM((T, FFN_HIDDEN), BF16),
            pltpu.VMEM((FFN_SLAB_SETS * 2 * FFN_CHUNK // LANES, T + 2 * HALO, LANES), F32)]


def _hy_ffn_kernel(x_ref, xp_ref, xn_ref, x0_ref, x0p_ref, x0n_ref, yt_ref, ytp_ref, ytn_ref,
                   wo_ref, g_ref, wup_ref, cw_ref, cb_ref, wdn_ref, o_ref,
                   xe_ref, a_ref, s_ref, m_ref, *, T, nchunk):
    tok = lambda blk: blk.astype(F32).T
    m_ref[0:HALO, :] = (x0p_ref[...] * tok(ytp_ref[...])[BLOCK - HALO:, :]).astype(BF16)
    for n in range(T // BLOCK):
        lo = HALO + n * BLOCK
        m_ref[lo:lo + BLOCK, :] = (x0_ref[n * BLOCK:(n + 1) * BLOCK, :] * tok(yt_ref[n])).astype(BF16)
    m_ref[HALO + T:, :] = (x0n_ref[...] * tok(ytn_ref[...])[:HALO, :]).astype(BF16)
    mix = jnp.dot(m_ref[...], wo_ref[...], preferred_element_type=F32)
    _ffn_body(x_ref[...] + mix[HALO:HALO + T], xp_ref[...] + mix[:HALO], xn_ref[...] + mix[HALO + T:],
              g_ref, wup_ref, cw_ref, cb_ref, wdn_ref, o_ref, xe_ref, a_ref, s_ref, T, nchunk)


def _hy_ffn(x, x0, yt, wo, gain, wup_r, cw_r, cb_r, wdn, T):
    B, L, D = x.shape
    nchunk = FFN_HIDDEN // FFN_CHUNK
    nb = T // BLOCK
    last = L // BLOCK - 1
    kern = functools.partial(_hy_ffn_kernel, T=T, nchunk=nchunk)
    return pl.pallas_call(
        kern,
        grid=(B, L // T),
        in_specs=_halo_specs(T, L, D) + _halo_specs(T, L, D) + [
            pl.BlockSpec((None, nb, D, LANES), lambda b, i: (b, i, 0, 0)),
            pl.BlockSpec((None, None, D, LANES), lambda b, i: (b, jnp.maximum(i * nb - 1, 0), 0, 0)),
            pl.BlockSpec((None, None, D, LANES), lambda b, i: (b, jnp.minimum((i + 1) * nb, last), 0, 0)),
            _const_spec(wo.shape), _const_spec((1, D)), _const_spec(wup_r.shape),
            _const_spec(cw_r.shape), _const_spec(cb_r.shape), _const_spec(wdn.shape)],
        out_specs=pl.BlockSpec((None, T, D), lambda b, i: (b, i, 0)),
        out_shape=jax.ShapeDtypeStruct(x.shape, F32),
        scratch_shapes=_ffn_scratch(T, D) + [pltpu.VMEM((T + 2 * HALO, D), BF16)],
        compiler_params=_params(2),
        name="hyena_out_conv_ffn",
    )(x, x, x, x0, x0, x0, yt, yt, yt, wo, gain, wup_r, cw_r, cb_r, wdn)


def _ffn(x, gain, wup_r, cw_r, cb_r, wdn, T):
    B, L, D = x.shape
    nchunk = FFN_HIDDEN // FFN_CHUNK
    kern = functools.partial(_ffn_kernel, T=T, nchunk=nchunk)
    return pl.pallas_call(
        kern,
        grid=(B, L // T),
        in_specs=_halo_specs(T, L, D) + [
            _const_spec((1, D)), _const_spec(wup_r.shape), _const_spec(cw_r.shape),
            _const_spec(cb_r.shape), _const_spec(wdn.shape)],
        out_specs=pl.BlockSpec((None, T, D), lambda b, i: (b, i, 0)),
        out_shape=jax.ShapeDtypeStruct(x.shape, F32),
        scratch_shapes=_ffn_scratch(T, D),
        compiler_params=_params(2),
        name="conv_ffn",
    )(x, x, x, gain, wup_r, cw_r, cb_r, wdn)


def _hy_in_kernel(x_ref, xp_ref, xn_ref, g_ref, win_ref, cw_ref, cb_ref, x0_ref, zt_ref,
                  xe_ref, s_ref, *, T, nchunk):
    i = pl.program_id(1)
    _fill_normed_ext(xe_ref, x_ref[...], xp_ref[...], xn_ref[...], g_ref[...], T,
                     i == 0, i == pl.num_programs(1) - 1)
    xe = xe_ref[...]
    ns = HY_CHUNK // LANES
    for j in range(nchunk):
        base = (j % 2) * 3 * ns
        for k in range(3):
            lo = k * D_MODEL + j * HY_CHUNK
            _park_slabs(s_ref, base + k * ns,
                        jnp.dot(xe, win_ref[:, lo:lo + HY_CHUNK], preferred_element_type=F32))
        for s in range(ns):
            col = j * HY_CHUNK + s * LANES
            w = [cw_ref[:, k * D_MODEL + col:k * D_MODEL + col + LANES] for k in range(3)]
            b = [cb_ref[:, k * D_MODEL + col:k * D_MODEL + col + LANES] for k in range(3)]
            for n in range(T // BLOCK):
                r0 = n * BLOCK
                x0_ref[r0:r0 + BLOCK, col:col + LANES] = _conv3_slab(s_ref, base + s, w[0], b[0], r0, BLOCK)
                z = (_conv3_slab(s_ref, base + ns + s, w[1], b[1], r0, BLOCK)
                     * _conv3_slab(s_ref, base + 2 * ns + s, w[2], b[2], r0, BLOCK))
                zt_ref[n, col:col + LANES, :] = z.T.astype(BF16)


def _hy_in(x, gain, win_r, cw_r, cb_r, T):
    B, L, D = x.shape
    nchunk = D // HY_CHUNK
    kern = functools.partial(_hy_in_kernel, T=T, nchunk=nchunk)
    return pl.pallas_call(
        kern,
        grid=(B, L // T),
        in_specs=_halo_specs(T, L, D) + [
            _const_spec((1, D)), _const_spec(win_r.shape), _const_spec(cw_r.shape),
            _const_spec(cb_r.shape)],
        out_specs=[pl.BlockSpec((None, T, D), lambda b, i: (b, i, 0)),
                   pl.BlockSpec((None, T // BLOCK, D, LANES), lambda b, i: (b, i, 0, 0))],
        out_shape=[jax.ShapeDtypeStruct(x.shape, F32),
                   jax.ShapeDtypeStruct((B, L // BLOCK, D, LANES), BF16)],
        scratch_shapes=[pltpu.VMEM((T + 2 * HALO, D), BF16),
                        pltpu.VMEM((2 * 3 * HY_CHUNK // LANES, T + 2 * HALO, LANES), F32)],
        compiler_params=_params(2),
        name="hyena_in",
    )(x, x, x, gain, win_r, cw_r, cb_r)


def _dot_f32(a, w_ref):
    a_hi = a.astype(BF16)
    a_lo = (a - a_hi.astype(F32)).astype(BF16)
    w_hi, w_lo = w_ref[0], w_ref[1]
    return (jnp.dot(a_hi, w_hi, preferred_element_type=F32)
            + jnp.dot(a_lo, w_hi, preferred_element_type=F32)
            + jnp.dot(a_hi, w_lo, preferred_element_type=F32))


def _filter_kernel(frl_ref, ph_ref, w1_ref, b1_ref, f1_ref, w2_ref, b2_ref, f2_ref, w3_ref, b3_ref,
                   f3_ref, wout_ref, delta_ref, hf_ref, hb_ref, *, L, R):
    hr = R // 2
    ja = pl.program_id(0) * R + lax.broadcasted_iota(jnp.int32, (hr, 1), 0)
    jb = ja + hr
    lane = lax.broadcasted_iota(jnp.int32, (hr, LANES), 1)
    j = jnp.where(lane < FILTER_WIDTH, ja, jb).astype(F32)
    t = j / float(L - 1)
    w = (2.0 * math.pi) * j / float(L)
    feats = jnp.where(lane % FILTER_WIDTH == 0, t, jnp.cos(w * frl_ref[...] + ph_ref[...]))
    h = jnp.sin(f1_ref[...] * (_dot_f32(feats, w1_ref) + b1_ref[...]))
    h = jnp.sin(f2_ref[...] * (_dot_f32(h, w2_ref) + b2_ref[...]))
    h = jnp.sin(f3_ref[...] * (_dot_f32(h, w3_ref) + b3_ref[...]))
    h = _dot_f32(h, wout_ref)
    for half, jidx in enumerate((ja, jb)):
        decay = jnp.exp(-(jidx.astype(F32) / float(L - 1)) * delta_ref[...])
        base = half * 2 * D_MODEL
        hf = h[:, base:base + D_MODEL] * decay
        hb = jnp.where(jidx == 0, 0.0, h[:, base + D_MODEL:base + 2 * D_MODEL] * decay)
        for n in range(hr // BLOCK):
            blk = half * (hr // BLOCK) + n
            hf_ref[blk] = hf[n * BLOCK:(n + 1) * BLOCK, :].T
            hb_ref[blk] = hb[n * BLOCK:(n + 1) * BLOCK, :].T


def _filter(L, frl, ph, w1, b1, f1, w2, b2, f2, w3, b3, f3, wout, delta):
    R = 512
    kern = functools.partial(_filter_kernel, L=L, R=R)
    consts = [frl, ph, w1, b1, f1, w2, b2, f2, w3, b3, f3, wout, delta]
    out_spec = pl.BlockSpec((R // BLOCK, D_MODEL, LANES), lambda i: (i, 0, 0))
    shape = jax.ShapeDtypeStruct((L // BLOCK, D_MODEL, LANES), F32)
    return pl.pallas_call(
        kern,
        grid=(L // R,),
        in_specs=[_const_spec(a.shape) for a in consts],
        out_specs=[out_spec, out_spec],
        out_shape=[shape, shape],
        compiler_params=_params(1),
        name="hyena_filter",
    )(*consts)


def _dft_consts(L):
    N = 2 * L
    N2 = FFT_N2
    N1 = N // N2
    h = N1 // 2
    a1 = -2.0 * np.pi * np.outer(np.arange(N1), np.arange(N1)) / N1
    f1r, f1i = np.cos(a1), np.sin(a1)
    a2 = -2.0 * np.pi * np.outer(np.arange(N2), np.arange(N2)) / N2
    f2r, f2i = np.cos(a2), np.sin(a2)
    at = -2.0 * np.pi * np.outer(np.arange(N1), np.arange(N2)) / N
    m1 = np.block([[f1r[:, :h], -f1i[:, :h]], [f1i[:, :h], f1r[:, :h]]])
    m1f = np.concatenate([f1r[:, :h], f1i[:, :h]], axis=0)
    m3 = np.block([[f1r[:h], f1i[:h]], [-f1i[:h], f1r[:h]]])
    c = lambda a: jnp.asarray(a, dtype=BF16)
    f = lambda a: jnp.asarray(a, dtype=F32)
    return dict(N1=N1, m1=c(m1), m1f=c(m1f), m3=c(m3),
                r2f=c(np.block([[f2r, f2i], [-f2i, f2r]])),
                r2i=c(np.block([[f2r, -f2i], [f2i, f2r]])),
                twr=f(np.cos(at)), twi=f(np.sin(at)))


def _cmul(ar, ai, br, bi):
    return ar * br - ai * bi, ar * bi + ai * br


def _slab8(ref, idx, c0, n):
    v = ref[idx + (slice(None), slice(c0, c0 + SUBLANES), slice(None))]
    return v.reshape(n * SUBLANES, LANES)


def _chan_rows(s_ref, base, c, n):
    return s_ref[pl.ds(base + c, n, stride=SUBLANES), :]


def _fft_kernel(x_ref, kf_ref, m1_ref, m3_ref, twr_ref, twi_ref, r2f_ref, r2i_ref, o_ref,
                *scratch, N1, nc):
    h = N1 // 2
    h8 = h * SUBLANES
    nsub = nc // SUBLANES
    twr, twi = twr_ref[...], twi_ref[...]
    cols = []
    for s16 in range(nsub // 2):
        lo16 = s16 * 2 * SUBLANES
        xr = x_ref[0, :, lo16:lo16 + 2 * SUBLANES, :].astype(F32)
        xi = x_ref[1, :, lo16:lo16 + 2 * SUBLANES, :].astype(F32)
        for half in range(2):
            sin_ref = scratch[2 * s16 + half]
            rows = slice(half * SUBLANES, (half + 1) * SUBLANES)
            sin_ref[0:h8] = xr[:, rows, :].reshape(h8, LANES)
            sin_ref[h8:2 * h8] = xi[:, rows, :].reshape(h8, LANES)
            cols += [jnp.concatenate([_chan_rows(sin_ref, 0, c, h), _chan_rows(sin_ref, h8, c, h)], axis=0)
                     for c in range(SUBLANES)]
    twr, twi = twr.astype(BF16), twi.astype(BF16)
    a = jnp.dot(m1_ref[...], jnp.concatenate(cols, axis=1).astype(BF16),
                preferred_element_type=F32).astype(BF16)
    lhs = []
    for c in range(nc):
        tr, ti = _cmul(a[:N1, c * LANES:(c + 1) * LANES], a[N1:, c * LANES:(c + 1) * LANES], twr, twi)
        lhs.append(jnp.concatenate([tr, ti], axis=1))
    lhs = jnp.concatenate(lhs, axis=0)
    X = jnp.dot(lhs, r2f_ref[...], preferred_element_type=F32)
    yr, yi = _cmul(X[:, :LANES], X[:, LANES:],
                   kf_ref[0].reshape(nc * N1, LANES), kf_ref[1].reshape(nc * N1, LANES))
    Bm = jnp.dot(jnp.concatenate([yr, yi], axis=1).astype(BF16), r2i_ref[...],
                 preferred_element_type=F32).astype(BF16)
    re_cols, im_cols = [], []
    for c in range(nc):
        br, bi = Bm[c * N1:(c + 1) * N1, :LANES], Bm[c * N1:(c + 1) * N1, LANES:]
        re_cols.append(br * twr + bi * twi)
        im_cols.append(bi * twr - br * twi)
    rhs = jnp.concatenate([jnp.concatenate(re_cols, axis=1),
                           jnp.concatenate(im_cols, axis=1)], axis=0)
    y = jnp.dot(m3_ref[...], rhs, preferred_element_type=F32)
    for s16 in range(nsub // 2):
        parts = ([], [])
        for half in range(2):
            c0 = (2 * s16 + half) * SUBLANES
            sout_ref = scratch[nsub + 2 * s16 + half]
            for c in range(SUBLANES):
                lo = (c0 + c) * LANES
                sout_ref[pl.ds(c, h, stride=SUBLANES), :] = y[:h, lo:lo + LANES]
                sout_ref[pl.ds(h8 + c, h, stride=SUBLANES), :] = y[h:, lo:lo + LANES]
            parts[0].append(sout_ref[0:h8].reshape(h, SUBLANES, LANES))
            parts[1].append(sout_ref[h8:2 * h8].reshape(h, SUBLANES, LANES))
        lo16 = s16 * 2 * SUBLANES
        for r in range(2):
            o_ref[r, :, lo16:lo16 + 2 * SUBLANES, :] = jnp.concatenate(parts[r], axis=1).astype(BF16)


def _fftconv(zt, kf, c):
    B, h, C, _ = zt.shape
    N1 = c["N1"]
    nc = min(FFT_ROWS // N1, C)
    kern = functools.partial(_fft_kernel, N1=N1, nc=nc)
    blk = pl.BlockSpec((2, h, nc, LANES), lambda cb, p: (p, 0, cb, 0))
    consts = [c["m1"], c["m3"], c["twr"], c["twi"], c["r2f"], c["r2i"]]
    return pl.pallas_call(
        kern,
        grid=(C // nc, B // 2),
        in_specs=[blk, pl.BlockSpec((2, nc, N1, LANES), lambda cb, p: (0, cb, 0, 0))]
        + [_const_spec(a.shape) for a in consts],
        out_specs=blk,
        out_shape=jax.ShapeDtypeStruct(zt.shape, BF16),
        scratch_shapes=[pltpu.VMEM((N1 * SUBLANES, LANES), F32)] * (2 * nc // SUBLANES),
        compiler_params=_params(2),
        name="fft_conv",
    )(zt, kf, *consts)


def _fspec_kernel(hf_ref, hb_ref, d_ref, m1f_ref, twr_ref, twi_ref, r2f_ref, o_ref,
                  *scratch, N1, nc, scale):
    h = N1 // 2
    h8 = h * SUBLANES
    twr, twi = twr_ref[...], twi_ref[...]
    cols = []
    for s8 in range(nc // SUBLANES):
        s_ref = scratch[s8]
        s_ref[0:h8] = _slab8(hf_ref, (), s8 * SUBLANES, h)
        s_ref[h8:2 * h8] = _slab8(hb_ref, (), s8 * SUBLANES, h)
        for c in range(SUBLANES):
            cols += [_chan_rows(s_ref, 0, c, h), _chan_rows(s_ref, h8, c, h)]
    twr, twi = twr.astype(BF16), twi.astype(BF16)
    a = jnp.dot(m1f_ref[...], jnp.concatenate(cols, axis=1).astype(BF16),
                preferred_element_type=F32).astype(BF16)
    lhs = []
    for k in range(2 * nc):
        tr, ti = _cmul(a[:N1, k * LANES:(k + 1) * LANES], a[N1:, k * LANES:(k + 1) * LANES], twr, twi)
        lhs.append(jnp.concatenate([tr, ti], axis=1))
    lhs = jnp.concatenate(lhs, axis=0)
    X = jnp.dot(lhs, r2f_ref[...], preferred_element_type=F32)
    for c in range(nc):
        xf = X[(2 * c) * N1:(2 * c + 1) * N1]
        xb = X[(2 * c + 1) * N1:(2 * c + 2) * N1]
        d = d_ref[c:c + 1, :]
        o_ref[0, c] = (xf[:, :LANES] + xb[:, :LANES] + d) * scale
        o_ref[1, c] = (xf[:, LANES:] - xb[:, LANES:]) * scale


def _filter_spectrum(hf, hb, d, c):
    h, C, _ = hf.shape
    N1 = c["N1"]
    nc = min(FFT_ROWS // N1, C)
    kern = functools.partial(_fspec_kernel, N1=N1, nc=nc, scale=1.0 / (N1 * FFT_N2))
    blk = pl.BlockSpec((h, nc, LANES), lambda cb: (0, cb, 0))
    consts = [c["m1f"], c["twr"], c["twi"], c["r2f"]]
    return pl.pallas_call(
        kern,
        grid=(C // nc,),
        in_specs=[blk, blk, pl.BlockSpec((nc, 1), lambda cb: (cb, 0))]
        + [_const_spec(a.shape) for a in consts],
        out_specs=pl.BlockSpec((2, nc, N1, LANES), lambda cb: (0, cb, 0, 0)),
        out_shape=jax.ShapeDtypeStruct((2, C, N1, LANES), F32),
        scratch_shapes=[pltpu.VMEM((N1 * SUBLANES, LANES), F32)] * (nc // SUBLANES),
        compiler_params=_params(1),
        name="filter_spectrum",
    )(hf, hb, d, *consts)


def _t5_bucket(rel):
    half = N_BUCKETS // 2
    max_exact = half // 2
    ret = jnp.where(rel > 0, half, 0)
    n = jnp.abs(rel)
    nf = jnp.maximum(n, 1).astype(jnp.float32)
    large = max_exact + (jnp.log(nf / max_exact) / math.log(MAX_DISTANCE / max_exact)
                         * (half - max_exact)).astype(jnp.int32)
    large = jnp.minimum(large, half - 1)
    return ret + jnp.where(n < max_exact, n, large)


LOG2E = 1.4426950408889634


def _bias_kernel(rb_ref, bucket_ref, rel_ref, o_ref):
    bucket = bucket_ref[...]
    rel = rel_ref[...]
    key = lax.broadcasted_iota(jnp.int32, bucket.shape, 0)
    for h in range(N_HEADS_A):
        acc = jnp.zeros(bucket.shape, F32)
        for b in range(N_BUCKETS):
            acc = jnp.where(bucket == b, rb_ref[b, h], acc)
        acc = acc * LOG2E
        for v in range(4):
            ok = jnp.abs(rel) <= WINDOW
            if v & 1:
                ok = ok & (key >= BLOCK)
            if v & 2:
                ok = ok & (key < 2 * BLOCK)
            o_ref[v, h] = jnp.where(ok, acc, NEG)


def _bias_table(rel_bias):
    rel = (jnp.arange(3 * BLOCK)[:, None] - BLOCK) - jnp.arange(BLOCK)[None, :]
    rel = rel.astype(jnp.int32)
    bucket = _t5_bucket(rel).astype(jnp.int32)
    return pl.pallas_call(
        _bias_kernel,
        in_specs=[pl.BlockSpec(memory_space=pltpu.SMEM),
                  pl.BlockSpec(memory_space=pltpu.VMEM), pl.BlockSpec(memory_space=pltpu.VMEM)],
        out_specs=pl.BlockSpec(memory_space=pltpu.VMEM),
        out_shape=jax.ShapeDtypeStruct((4, N_HEADS_A, 3 * BLOCK, BLOCK), F32),
        name="rel_bias_table",
    )(rel_bias.astype(F32), bucket, rel)


def _group_ms(v, gmat):
    return jnp.dot((v * v).astype(BF16), gmat, preferred_element_type=F32)


def _ab_kernel(sink_ref, x_ref, xp_ref, xn_ref, g_ref, win_ref, wout_ref, gm_ref,
               kg_ref, bias_ref, vg_ref, ws_ref, bs_ref, o_ref, *, T):
    i = pl.program_id(1)
    nb = T // BLOCK
    nblocks = pl.num_programs(1) * nb
    gain = g_ref[...]
    x = x_ref[...]
    proj = jnp.dot(_rms(x, gain).astype(BF16), win_ref[...], preferred_element_type=F32)
    xh = jnp.concatenate([_rms(xp_ref[...], gain), _rms(xn_ref[...], gain)], axis=0)
    kvh = jnp.dot(xh.astype(BF16), win_ref[:, ATTN_W:ATTN_W + 2 * BLOCK],
                  preferred_element_type=F32)
    gm = gm_ref[...]
    gm_k = gm[:BLOCK, :BLOCK]

    k_all = jnp.concatenate([kvh[:BLOCK, :BLOCK], proj[:, ATTN_W:ATTN_W + BLOCK],
                             kvh[BLOCK:, :BLOCK]], axis=0)
    v_all = jnp.concatenate([kvh[:BLOCK, BLOCK:], proj[:, ATTN_W + BLOCK:ATTN_W + 2 * BLOCK],
                             kvh[BLOCK:, BLOCK:]], axis=0)
    kn = k_all * lax.rsqrt(_group_ms(k_all, gm_k) + EPS) * kg_ref[...]

    su = _gelu(proj[:, ATTN_W + 2 * BLOCK:ATTN_W + 2 * BLOCK + SGU_W])
    sv = _gelu(proj[:, ATTN_W + 2 * BLOCK + SGU_W:])
    svn = (sv * lax.rsqrt(_group_ms(sv, gm) + EPS) * vg_ref[...]).astype(BF16)
    low = lax.broadcasted_iota(jnp.int32, (1, BLOCK * nb), 1) % BLOCK < CH_B
    slabs = []
    for j in range(SGU_W // BLOCK):
        rhs = jnp.concatenate([svn[n * BLOCK:(n + 1) * BLOCK, j * BLOCK:(j + 1) * BLOCK]
                               for n in range(nb)], axis=1)
        a = jnp.dot(ws_ref[2 * j], rhs, preferred_element_type=F32)
        b = jnp.dot(ws_ref[2 * j + 1], rhs, preferred_element_type=F32)
        slabs.append(jnp.where(low, a, b))
    mixed = jnp.concatenate(
        [jnp.concatenate([slabs[j][:, n * BLOCK:(n + 1) * BLOCK] for j in range(SGU_W // BLOCK)],
                         axis=1) + bs_ref[...] for n in range(nb)], axis=0)
    sgu_out = jnp.dot((su * mixed).astype(BF16), wout_ref[ATTN_W:, :], preferred_element_type=F32)

    q_t = proj[:, :ATTN_W].T
    qt = []
    for h in range(N_HEADS_A):
        qh = q_t[h * HEAD_DIM:(h + 1) * HEAD_DIM]
        qt.append((qh * lax.rsqrt(jnp.mean(qh * qh, axis=0, keepdims=True) + EPS)).astype(BF16))
    qt = jnp.concatenate(qt, axis=0)
    vt = v_all.T.astype(BF16)
    knb = kn.astype(BF16)
    zeros_q = jnp.zeros((HEAD_DIM, GQA * BLOCK), BF16)
    ones_rows = jnp.ones((2 * SUBLANES, 3 * BLOCK), BF16)
    units = [(n, hk) for n in range(nb) for hk in range(N_KV_A)]
    sks = [jnp.concatenate([jnp.full((1, BLOCK), sink_ref[hk * GQA + g] * LOG2E, F32)
                            for g in range(GQA)], axis=1) for hk in range(N_KV_A)]
    scores = []
    for n, hk in units:
        blk = i * nb + n
        variant = jnp.where(blk == 0, 1, 0) + jnp.where(blk == nblocks - 1, 2, 0)
        qh = jnp.concatenate(
            [qt[(hk * GQA + g) * HEAD_DIM:(hk * GQA + g + 1) * HEAD_DIM, n * BLOCK:(n + 1) * BLOCK]
             for g in range(GQA)], axis=1)
        qz = jnp.concatenate([qh, zeros_q] if hk == 0 else [zeros_q, qh], axis=0)
        s = jnp.dot(knb[n * BLOCK:n * BLOCK + 3 * BLOCK, :], qz,
                    preferred_element_type=F32)
        scores.append(s + jnp.concatenate([bias_ref[variant, hk * GQA + g] for g in range(GQA)], axis=1))
    probs = []
    for (n, hk), s in zip(units, scores):
        m = jnp.maximum(jnp.max(s, axis=0, keepdims=True), sks[hk])
        probs.append((jnp.exp2(s - m).astype(BF16), jnp.exp2(sks[hk] - m)))
    outs = []
    for (n, hk), (p, psink) in zip(units, probs):
        lhs = jnp.concatenate([vt[hk * HEAD_DIM:(hk + 1) * HEAD_DIM, n * BLOCK:n * BLOCK + 3 * BLOCK],
                               ones_rows], axis=0)
        pv = jnp.dot(lhs, p, preferred_element_type=F32)
        outs.append(pv[:HEAD_DIM] / (pv[HEAD_DIM:HEAD_DIM + 1] + psink))
    out_cols = []
    for n in range(nb):
        out_cols.append(jnp.concatenate(
            [outs[n * N_KV_A + hk][:, g * BLOCK:(g + 1) * BLOCK]
             for hk in range(N_KV_A) for g in range(GQA)], axis=0))
    attn = jnp.concatenate(out_cols, axis=1).T.astype(BF16)
    o_ref[...] = x + sgu_out + jnp.dot(attn, wout_ref[:ATTN_W, :], preferred_element_type=F32)


def _ab_layer(x, sink, gain, win, wout, gm, kg, bias_tab, vg, ws, bs, T):
    B, L, D = x.shape
    hb = T // BLOCK
    last = L // BLOCK - 1
    kern = functools.partial(_ab_kernel, T=T)
    consts = [gain, win, wout, gm, kg, bias_tab, vg, ws, bs]
    return pl.pallas_call(
        kern,
        grid=(B, L // T),
        in_specs=[pl.BlockSpec(memory_space=pltpu.SMEM),
                  pl.BlockSpec((None, T, D), lambda b, i: (b, i, 0)),
                  pl.BlockSpec((None, BLOCK, D), lambda b, i: (b, jnp.maximum(i * hb - 1, 0), 0)),
                  pl.BlockSpec((None, BLOCK, D), lambda b, i: (b, jnp.minimum((i + 1) * hb, last), 0)),
                  ] + [_const_spec(a.shape) for a in consts],
        out_specs=pl.BlockSpec((None, T, D), lambda b, i: (b, i, 0)),
        out_shape=jax.ShapeDtypeStruct(x.shape, F32),
        compiler_params=_params(2),
        name="attn_sgu_mixer",
    )(sink, x, x, x, *consts)


def _prepare(p):
    depth = p["mix_norm"].shape[0]
    q = dict(depth=depth)
    q["mix_norm"] = [p["mix_norm"][l][None, :] for l in range(depth)]
    q["ffn_norm"] = [p["ffn_norm"][l][None, :] for l in range(depth)]
    q["ffn_wup"] = [p["ffn_w_up"][l].astype(BF16) for l in range(depth)]
    q["ffn_cw"] = [p["ffn_conv_w"][l] for l in range(depth)]
    q["ffn_cb"] = [p["ffn_conv_b"][l][None, :] for l in range(depth)]
    q["ffn_wdn"] = [p["ffn_w_down"][l].astype(BF16) for l in range(depth)]

    n_even = p["ab_w_in"].shape[0]
    q["ab_win"] = [p["ab_w_in"][i].astype(BF16) for i in range(n_even)]
    q["ab_wout"] = [p["ab_w_out"][i].astype(BF16) for i in range(n_even)]
    q["kg"] = [jnp.tile(p["k_norm"][i] * p["q_norm"][i] * (HEAD_DIM ** -0.5 * LOG2E), N_KV_A)[None, :]
               for i in range(n_even)]
    q["sink"] = [p["attn_sink"][i].astype(F32) for i in range(n_even)]
    q["vg"] = [p["sgu_v_norm"][i].reshape(1, SGU_W) for i in range(n_even)]
    q["ws"] = [p["sgu_w"][i].astype(BF16) for i in range(n_even)]
    q["bs"] = [jnp.repeat(p["sgu_b"][i].T, CH_B, axis=1) for i in range(n_even)]
    grp = np.arange(ATTN_W) // HEAD_DIM
    q["gm"] = jnp.asarray((grp[:, None] == grp[None, :]) / float(HEAD_DIM), dtype=BF16)

    n_odd = p["hy_w_in"].shape[0]
    q["hy_win"] = [p["hy_w_in"][i].astype(BF16) for i in range(n_odd)]
    q["hy_cw"] = [p["hy_conv_w"][i] for i in range(n_odd)]
    q["hy_cb"] = [p["hy_conv_b"][i][None, :] for i in range(n_odd)]
    q["hy_d"] = [p["hy_d"][i][:, None].astype(F32) for i in range(n_odd)]
    q["hy_wout"] = [p["hy_w_out"][i].astype(BF16) for i in range(n_odd)]
    slot = np.arange(LANES) % FILTER_WIDTH
    fr = np.linspace(1e-4, FILTER_BANDS - 1, FILTER_BANDS).astype(np.float32)
    frl = np.where((slot >= 1) & (slot <= 2 * FILTER_BANDS), fr[(slot - 1) % FILTER_BANDS], 0.0)
    ph = np.where((slot > FILTER_BANDS) & (slot <= 2 * FILTER_BANDS), 0.5 * np.pi, 0.0)
    q["frl"] = jnp.asarray(frl[None, :], dtype=F32)
    q["ph"] = jnp.asarray(ph[None, :], dtype=F32)
    q["delta"] = jnp.abs(jnp.linspace(MIN_DECAY, MAX_DECAY, D_MODEL, dtype=F32))[None, :]

    def twice(w):
        z = jnp.zeros_like(w)
        w = jnp.concatenate([jnp.concatenate([w, z], axis=1), jnp.concatenate([z, w], axis=1)], axis=0)
        hi = w.astype(BF16)
        return jnp.stack([hi, (w - hi.astype(F32)).astype(BF16)])

    filt = []
    for i in range(n_odd):
        w1p = jnp.zeros((FILTER_WIDTH, FILTER_WIDTH), F32).at[:FILTER_EMB].set(p["hy_f_w1"][i])
        row = lambda a: jnp.tile(a[None, :].astype(F32), (1, 2))
        filt.append((twice(w1p), row(p["hy_f_b1"][i]), row(p["hy_f_freq1"][i]),
                     twice(p["hy_f_w2"][i]), row(p["hy_f_b2"][i]), row(p["hy_f_freq2"][i]),
                     twice(p["hy_f_w3"][i]), row(p["hy_f_b3"][i]), row(p["hy_f_freq3"][i]),
                     twice(p["hy_f_wout"][i])))
    q["filt"] = filt
    q["bias_tab"] = _bias_table(p["rel_bias"])
    return q


def _hyena_conv(x, q, l, consts, T):
    i = l // 2
    L = x.shape[1]
    hf, hb = _filter(L, q["frl"], q["ph"], *q["filt"][i], q["delta"])
    kf = _filter_spectrum(hf, hb, q["hy_d"][i], consts)
    x0, zt = _hy_in(x, q["mix_norm"][l], q["hy_win"][i], q["hy_cw"][i], q["hy_cb"][i], T)
    return x0, _fftconv(zt, kf, consts)


def _trunk(x, q):
    B, L, D = x.shape
    consts = _dft_consts(L)
    t_mix, t_ffn = min(TOKENS_MIXER, L), min(TOKENS_FFN, L)
    for l in range(q["depth"]):
        i = l // 2
        ffn = (q["ffn_norm"][l], q["ffn_wup"][l], q["ffn_cw"][l], q["ffn_cb"][l], q["ffn_wdn"][l])
        if l % 2 == 0:
            x = _ab_layer(x, q["sink"][i], q["mix_norm"][l], q["ab_win"][i], q["ab_wout"][i],
                          q["gm"], q["kg"][i], q["bias_tab"], q["vg"][i], q["ws"][i],
                          q["bs"][i], t_mix)
            x = _ffn(x, *ffn, t_ffn)
        else:
            x0, yt = _hyena_conv(x, q, l, consts, t_mix)
            x = _hy_ffn(x, x0, yt, q["hy_wout"][i], *ffn, t_ffn)
    return x


def kernel(x_prompt, x_sample, rel_bias, mix_norm, ffn_norm, ab_w_in, q_norm, k_norm, attn_sink, sgu_v_norm, sgu_w, sgu_b, ab_w_out, hy_w_in, hy_conv_w, hy_conv_b, hy_f_w1, hy_f_b1, hy_f_freq1, hy_f_w2, hy_f_b2, hy_f_freq2, hy_f_w3, hy_f_b3, hy_f_freq3, hy_f_wout, hy_d, hy_w_out, ffn_w_up, ffn_conv_w, ffn_conv_b, ffn_w_down):
    p = dict(rel_bias=rel_bias, mix_norm=mix_norm, ffn_norm=ffn_norm, ab_w_in=ab_w_in,
             q_norm=q_norm, k_norm=k_norm, attn_sink=attn_sink, sgu_v_norm=sgu_v_norm,
             sgu_w=sgu_w, sgu_b=sgu_b, ab_w_out=ab_w_out, hy_w_in=hy_w_in, hy_conv_w=hy_conv_w,
             hy_conv_b=hy_conv_b, hy_f_w1=hy_f_w1, hy_f_b1=hy_f_b1, hy_f_freq1=hy_f_freq1,
             hy_f_w2=hy_f_w2, hy_f_b2=hy_f_b2, hy_f_freq2=hy_f_freq2, hy_f_w3=hy_f_w3,
             hy_f_b3=hy_f_b3, hy_f_freq3=hy_f_freq3, hy_f_wout=hy_f_wout, hy_d=hy_d,
             hy_w_out=hy_w_out, ffn_w_up=ffn_w_up, ffn_conv_w=ffn_conv_w, ffn_conv_b=ffn_conv_b,
             ffn_w_down=ffn_w_down)
    q = _prepare(p)
    return (_trunk(x_prompt, q), _trunk(x_sample, q))
```

```python
import functools
import math

import numpy as np
import jax
import jax.numpy as jnp
from jax import lax
from jax.experimental import pallas as pl
from jax.experimental.pallas import tpu as pltpu

F32 = jnp.float32
BF16 = jnp.bfloat16

D_MODEL = 1024
HEAD_DIM = 64
N_HEADS_A = 8
N_KV_A = 2
GQA = 4
ATTN_W = 512
WINDOW = 128
BLOCK = 128
N_BUCKETS = 32
MAX_DISTANCE = 128
SGU_W = 512
N_GROUPS_B = 8
CH_B = 64
FILTER_EMB = 33
FILTER_BANDS = 16
FILTER_WIDTH = 64
DECAY_TARGET = 1e-2
MIN_DECAY = math.log(DECAY_TARGET) / 1.5
MAX_DECAY = math.log(DECAY_TARGET) / 0.3
FFN_HIDDEN = 2816
EPS = 1e-6
NEG = -1e30

LANES = 128
SUBLANES = 8
TOKENS_FFN = 512
TOKENS_MIXER = 1024
FFN_CHUNK = 256
HY_CHUNK = 256
HALO = 8
ROW_BLOCK = 64
FFN_SLAB_SETS = FFN_HIDDEN // FFN_CHUNK
FFT_N2 = LANES
FFT_ROWS = 8192
SPECTRUM_ROWS = 4096
VMEM_LIMIT = 56 * 1024 * 1024


def _gelu(x):
    t = jnp.tanh(x * (0.7978845608028654 + (0.7978845608028654 * 0.044715) * (x * x)))
    hx = 0.5 * x
    return hx + hx * t


def _rms(x, gain):
    return x * lax.rsqrt(jnp.mean(x * x, axis=-1, keepdims=True) + EPS) * gain


def _const_spec(shape):
    nd = len(shape)
    return pl.BlockSpec(shape, lambda *_: (0,) * nd, pipeline_mode=pl.Buffered(1))


def _params(n_axes):
    return pltpu.CompilerParams(dimension_semantics=("arbitrary",) * n_axes,
                                vmem_limit_bytes=VMEM_LIMIT)


def _park_slabs(s_ref, base, h):
    for s in range(h.shape[1] // LANES):
        s_ref[base + s] = h[:, s * LANES:(s + 1) * LANES]


def _conv3_slab(s_ref, slab, w, b, r0, rows):
    lo = HALO + r0
    return (s_ref[slab, lo - 1:lo - 1 + rows, :] * w[0:1]
            + s_ref[slab, lo:lo + rows, :] * w[1:2]
            + s_ref[slab, lo + 1:lo + 1 + rows, :] * w[2:3]
            + b)


def _fill_normed_ext(xe_ref, x, xp, xn, gain, rows, first, last):
    pm = jnp.where(first, 0.0, 1.0)
    nm = jnp.where(last, 0.0, 1.0)
    xe_ref[0:HALO, :] = (_rms(xp, gain) * pm).astype(BF16)
    xe_ref[HALO:HALO + rows, :] = _rms(x, gain).astype(BF16)
    xe_ref[HALO + rows:HALO + rows + HALO, :] = (_rms(xn, gain) * nm).astype(BF16)


def _halo_specs(T, L, D):
    hb = T // HALO
    last = L // HALO - 1
    return [
        pl.BlockSpec((None, T, D), lambda b, i: (b, i, 0)),
        pl.BlockSpec((None, HALO, D), lambda b, i: (b, jnp.maximum(i * hb - 1, 0), 0)),
        pl.BlockSpec((None, HALO, D), lambda b, i: (b, jnp.minimum((i + 1) * hb, last), 0)),
    ]


def _ffn_body(x, xp, xn, g_ref, wup_ref, cw_ref, cb_ref, wdn_ref, o_ref, xe_ref, a_ref, s_ref,
              T, nchunk):
    i = pl.program_id(1)
    _fill_normed_ext(xe_ref, x, xp, xn, g_ref[...], T, i == 0, i == pl.num_programs(1) - 1)
    xe = xe_ref[...]
    ns = FFN_CHUNK // LANES
    for j in range(nchunk):
        base = (j % FFN_SLAB_SETS) * 2 * ns
        for part in range(2):
            lo = part * FFN_HIDDEN + j * FFN_CHUNK
            _park_slabs(s_ref, base + part * ns,
                        jnp.dot(xe, wup_ref[:, lo:lo + FFN_CHUNK], preferred_element_type=F32))
        for s in range(ns):
            col = j * FFN_CHUNK + s * LANES
            wg, bg = cw_ref[:, col:col + LANES], cb_ref[:, col:col + LANES]
            wu = cw_ref[:, FFN_HIDDEN + col:FFN_HIDDEN + col + LANES]
            bu = cb_ref[:, FFN_HIDDEN + col:FFN_HIDDEN + col + LANES]
            for r0 in range(0, T, ROW_BLOCK):
                g = _conv3_slab(s_ref, base + s, wg, bg, r0, ROW_BLOCK)
                u = _conv3_slab(s_ref, base + ns + s, wu, bu, r0, ROW_BLOCK)
                a_ref[r0:r0 + ROW_BLOCK, col:col + LANES] = (_gelu(g) * u).astype(BF16)
    o_ref[...] = x + jnp.dot(a_ref[...], wdn_ref[...], preferred_element_type=F32)


def _ffn_kernel(x_ref, xp_ref, xn_ref, g_ref, wup_ref, cw_ref, cb_ref, wdn_ref, o_ref,
                xe_ref, a_ref, s_ref, *, T, nchunk):
    _ffn_body(x_ref[...], xp_ref[...], xn_ref[...], g_ref, wup_ref, cw_ref, cb_ref, wdn_ref,
              o_ref, xe_ref, a_ref, s_ref, T, nchunk)


def _ffn_scratch(T, D):
    return [pltpu.VMEM((T + 2 * HALO, D), BF16), pltpu.VMEM((T, FFN_HIDDEN), BF16),
            pltpu.VMEM((FFN_SLAB_SETS * 2 * FFN_CHUNK // LANES, T + 2 * HALO, LANES), F32)]


def _hy_ffn_kernel(x_ref, xp_ref, xn_ref, x0_ref, x0p_ref, x0n_ref, yt_ref, ytp_ref, ytn_ref,
                   wo_ref, g_ref, wup_ref, cw_ref, cb_ref, wdn_ref, o_ref,
                   xe_ref, a_ref, s_ref, m_ref, *, T, nchunk):
    tok = lambda blk: blk.astype(F32).T
    m_ref[0:HALO, :] = (x0p_ref[...] * tok(ytp_ref[...])[BLOCK - HALO:, :]).astype(BF16)
    for n in range(T // BLOCK):
        lo = HALO + n * BLOCK
        m_ref[lo:lo + BLOCK, :] = (x0_ref[n * BLOCK:(n + 1) * BLOCK, :] * tok(yt_ref[n])).astype(BF16)
    m_ref[HALO + T:, :] = (x0n_ref[...] * tok(ytn_ref[...])[:HALO, :]).astype(BF16)
    mix = jnp.dot(m_ref[...], wo_ref[...], preferred_element_type=F32)
    _ffn_body(x_ref[...] + mix[HALO:HALO + T], xp_ref[...] + mix[:HALO], xn_ref[...] + mix[HALO + T:],
              g_ref, wup_ref, cw_ref, cb_ref, wdn_ref, o_ref, xe_ref, a_ref, s_ref, T, nchunk)


def _hy_ffn(x, x0, yt, wo, gain, wup_r, cw_r, cb_r, wdn, T):
    B, L, D = x.shape
    nchunk = FFN_HIDDEN // FFN_CHUNK
    nb = T // BLOCK
    last = L // BLOCK - 1
    kern = functools.partial(_hy_ffn_kernel, T=T, nchunk=nchunk)
    return pl.pallas_call(
        kern,
        grid=(B, L // T),
        in_specs=_halo_specs(T, L, D) + _halo_specs(T, L, D) + [
            pl.BlockSpec((None, nb, D, LANES), lambda b, i: (b, i, 0, 0)),
            pl.BlockSpec((None, None, D, LANES), lambda b, i: (b, jnp.maximum(i * nb - 1, 0), 0, 0)),
            pl.BlockSpec((None, None, D, LANES), lambda b, i: (b, jnp.minimum((i + 1) * nb, last), 0, 0)),
            _const_spec(wo.shape), _const_spec((1, D)), _const_spec(wup_r.shape),
            _const_spec(cw_r.shape), _const_spec(cb_r.shape), _const_spec(wdn.shape)],
        out_specs=pl.BlockSpec((None, T, D), lambda b, i: (b, i, 0)),
        out_shape=jax.ShapeDtypeStruct(x.shape, F32),
        scratch_shapes=_ffn_scratch(T, D) + [pltpu.VMEM((T + 2 * HALO, D), BF16)],
        compiler_params=_params(2),
        name="hyena_out_conv_ffn",
    )(x, x, x, x0, x0, x0, yt, yt, yt, wo, gain, wup_r, cw_r, cb_r, wdn)


def _ffn(x, gain, wup_r, cw_r, cb_r, wdn, T):
    B, L, D = x.shape
    nchunk = FFN_HIDDEN // FFN_CHUNK
    kern = functools.partial(_ffn_kernel, T=T, nchunk=nchunk)
    return pl.pallas_call(
        kern,
        grid=(B, L // T),
        in_specs=_halo_specs(T, L, D) + [
            _const_spec((1, D)), _const_spec(wup_r.shape), _const_spec(cw_r.shape),
            _const_spec(cb_r.shape), _const_spec(wdn.shape)],
        out_specs=pl.BlockSpec((None, T, D), lambda b, i: (b, i, 0)),
        out_shape=jax.ShapeDtypeStruct(x.shape, F32),
        scratch_shapes=_ffn_scratch(T, D),
        compiler_params=_params(2),
        name="conv_ffn",
    )(x, x, x, gain, wup_r, cw_r, cb_r, wdn)


def _hy_in_kernel(x_ref, xp_ref, xn_ref, g_ref, win_ref, cw_ref, cb_ref, x0_ref, zt_ref,
                  xe_ref, s_ref, *, T, nchunk):
    i = pl.program_id(1)
    _fill_normed_ext(xe_ref, x_ref[...], xp_ref[...], xn_ref[...], g_ref[...], T,
                     i == 0, i == pl.num_programs(1) - 1)
    xe = xe_ref[...]
    ns = HY_CHUNK // LANES
    for j in range(nchunk):
        base = (j % 2) * 3 * ns
        for k in range(3):
            lo = k * D_MODEL + j * HY_CHUNK
            _park_slabs(s_ref, base + k * ns,
                        jnp.dot(xe, win_ref[:, lo:lo + HY_CHUNK], preferred_element_type=F32))
        for s in range(ns):
            col = j * HY_CHUNK + s * LANES
            w = [cw_ref[:, k * D_MODEL + col:k * D_MODEL + col + LANES] for k in range(3)]
            b = [cb_ref[:, k * D_MODEL + col:k * D_MODEL + col + LANES] for k in range(3)]
            for n in range(T // BLOCK):
                r0 = n * BLOCK
                x0_ref[r0:r0 + BLOCK, col:col + LANES] = _conv3_slab(s_ref, base + s, w[0], b[0], r0, BLOCK)
                z = (_conv3_slab(s_ref, base + ns + s, w[1], b[1], r0, BLOCK)
                     * _conv3_slab(s_ref, base + 2 * ns + s, w[2], b[2], r0, BLOCK))
                zt_ref[n, col:col + LANES, :] = z.T.astype(BF16)


def _hy_in(x, gain, win_r, cw_r, cb_r, T):
    B, L, D = x.shape
    nchunk = D // HY_CHUNK
    kern = functools.partial(_hy_in_kernel, T=T, nchunk=nchunk)
    return pl.pallas_call(
        kern,
        grid=(B, L // T),
        in_specs=_halo_specs(T, L, D) + [
            _const_spec((1, D)), _const_spec(win_r.shape), _const_spec(cw_r.shape),
            _const_spec(cb_r.shape)],
        out_specs=[pl.BlockSpec((None, T, D), lambda b, i: (b, i, 0)),
                   pl.BlockSpec((None, T // BLOCK, D, LANES), lambda b, i: (b, i, 0, 0))],
        out_shape=[jax.ShapeDtypeStruct(x.shape, F32),
                   jax.ShapeDtypeStruct((B, L // BLOCK, D, LANES), BF16)],
        scratch_shapes=[pltpu.VMEM((T + 2 * HALO, D), BF16),
                        pltpu.VMEM((2 * 3 * HY_CHUNK // LANES, T + 2 * HALO, LANES), F32)],
        compiler_params=_params(2),
        name="hyena_in",
    )(x, x, x, gain, win_r, cw_r, cb_r)


def _dot_f32(a, w_ref):
    a_hi = a.astype(BF16)
    a_lo = (a - a_hi.astype(F32)).astype(BF16)
    w_hi, w_lo = w_ref[0], w_ref[1]
    return (jnp.dot(a_hi, w_hi, preferred_element_type=F32)
            + jnp.dot(a_lo, w_hi, preferred_element_type=F32)
            + jnp.dot(a_hi, w_lo, preferred_element_type=F32))


def _filter_kernel(frl_ref, ph_ref, w1_ref, b1_ref, f1_ref, w2_ref, b2_ref, f2_ref, w3_ref, b3_ref,
                   f3_ref, wout_ref, delta_ref, hf_ref, hb_ref, *, L, R):
    hr = R // 2
    ja = pl.program_id(0) * R + lax.broadcasted_iota(jnp.int32, (hr, 1), 0)
    jb = ja + hr
    lane = lax.broadcasted_iota(jnp.int32, (hr, LANES), 1)
    j = jnp.where(lane < FILTER_WIDTH, ja, jb).astype(F32)
    t = j / float(L - 1)
    w = (2.0 * math.pi) * j / float(L)
    feats = jnp.where(lane % FILTER_WIDTH == 0, t, jnp.cos(w * frl_ref[...] + ph_ref[...]))
    h = jnp.sin(f1_ref[...] * (_dot_f32(feats, w1_ref) + b1_ref[...]))
    h = jnp.sin(f2_ref[...] * (_dot_f32(h, w2_ref) + b2_ref[...]))
    h = jnp.sin(f3_ref[...] * (_dot_f32(h, w3_ref) + b3_ref[...]))
    h = _dot_f32(h, wout_ref)
    for half, jidx in enumerate((ja, jb)):
        decay = jnp.exp(-(jidx.astype(F32) / float(L - 1)) * delta_ref[...])
        base = half * 2 * D_MODEL
        hf = h[:, base:base + D_MODEL] * decay
        hb = jnp.where(jidx == 0, 0.0, h[:, base + D_MODEL:base + 2 * D_MODEL] * decay)
        for n in range(hr // BLOCK):
            blk = half * (hr // BLOCK) + n
            hf_ref[blk] = hf[n * BLOCK:(n + 1) * BLOCK, :].T
            hb_ref[blk] = hb[n * BLOCK:(n + 1) * BLOCK, :].T


def _filter(L, frl, ph, w1, b1, f1, w2, b2, f2, w3, b3, f3, wout, delta):
    R = 512
    kern = functools.partial(_filter_kernel, L=L, R=R)
    consts = [frl, ph, w1, b1, f1, w2, b2, f2, w3, b3, f3, wout, delta]
    out_spec = pl.BlockSpec((R // BLOCK, D_MODEL, LANES), lambda i: (i, 0, 0))
    shape = jax.ShapeDtypeStruct((L // BLOCK, D_MODEL, LANES), F32)
    return pl.pallas_call(
        kern,
        grid=(L // R,),
        in_specs=[_const_spec(a.shape) for a in consts],
        out_specs=[out_spec, out_spec],
        out_shape=[shape, shape],
        compiler_params=_params(1),
        name="hyena_filter",
    )(*consts)


def _dft_consts(L):
    N = 2 * L
    N2 = FFT_N2
    N1 = N // N2
    h = N1 // 2
    a1 = -2.0 * np.pi * np.outer(np.arange(N1), np.arange(N1)) / N1
    f1r, f1i = np.cos(a1), np.sin(a1)
    a2 = -2.0 * np.pi * np.outer(np.arange(N2), np.arange(N2)) / N2
    f2r, f2i = np.cos(a2), np.sin(a2)
    at = -2.0 * np.pi * np.outer(np.arange(N1), np.arange(N2)) / N
    m1 = np.block([[f1r[:, :h], -f1i[:, :h]], [f1i[:, :h], f1r[:, :h]]])
    m1f = np.concatenate([f1r[:, :h], f1i[:, :h]], axis=0)
    m3 = np.block([[f1r[:h], f1i[:h]], [-f1i[:h], f1r[:h]]])
    c = lambda a: jnp.asarray(a, dtype=BF16)
    f = lambda a: jnp.asarray(a, dtype=F32)
    return dict(N1=N1, m1=c(m1), m1f=c(m1f), m3=c(m3),
                r2f=c(np.block([[f2r, f2i], [-f2i, f2r]])),
                r2i=c(np.block([[f2r, -f2i], [f2i, f2r]])),
                twr=f(np.cos(at)), twi=f(np.sin(at)))


def _cmul(ar, ai, br, bi):
    return ar * br - ai * bi, ar * bi + ai * br


def _slab8(ref, idx, c0, n):
    v = ref[idx + (slice(None), slice(c0, c0 + SUBLANES), slice(None))]
    return v.reshape(n * SUBLANES, LANES)


def _chan_rows(s_ref, base, c, n):
    return s_ref[pl.ds(base + c, n, stride=SUBLANES), :]


def _fft_kernel(x_ref, kf_ref, m1_ref, m3_ref, twr_ref, twi_ref, r2f_ref, r2i_ref, o_ref,
                *scratch, N1, nc):
    h = N1 // 2
    h8 = h * SUBLANES
    nsub = nc // SUBLANES
    twr, twi = twr_ref[...], twi_ref[...]
    cols = []
    for s16 in range(nsub // 2):
        lo16 = s16 * 2 * SUBLANES
        xr = x_ref[0, :, lo16:lo16 + 2 * SUBLANES, :].astype(F32)
        xi = x_ref[1, :, lo16:lo16 + 2 * SUBLANES, :].astype(F32)
        for half in range(2):
            sin_ref = scratch[2 * s16 + half]
            rows = slice(half * SUBLANES, (half + 1) * SUBLANES)
            sin_ref[0:h8] = xr[:, rows, :].reshape(h8, LANES)
            sin_ref[h8:2 * h8] = xi[:, rows, :].reshape(h8, LANES)
            cols += [jnp.concatenate([_chan_rows(sin_ref, 0, c, h), _chan_rows(sin_ref, h8, c, h)], axis=0)
                     for c in range(SUBLANES)]
    twr, twi = twr.astype(BF16), twi.astype(BF16)
    a = jnp.dot(m1_ref[...], jnp.concatenate(cols, axis=1).astype(BF16),
                preferred_element_type=F32).astype(BF16)
    lhs = []
    for c in range(nc):
        tr, ti = _cmul(a[:N1, c * LANES:(c + 1) * LANES], a[N1:, c * LANES:(c + 1) * LANES], twr, twi)
        lhs.append(jnp.concatenate([tr, ti], axis=1))
    lhs = jnp.concatenate(lhs, axis=0)
    X = jnp.dot(lhs, r2f_ref[...], preferred_element_type=F32)
    yr, yi = _cmul(X[:, :LANES], X[:, LANES:],
                   kf_ref[0].reshape(nc * N1, LANES), kf_ref[1].reshape(nc * N1, LANES))
    Bm = jnp.dot(jnp.concatenate([yr, yi], axis=1).astype(BF16), r2i_ref[...],
                 preferred_element_type=F32).astype(BF16)
    re_cols, im_cols = [], []
    for c in range(nc):
        br, bi = Bm[c * N1:(c + 1) * N1, :LANES], Bm[c * N1:(c + 1) * N1, LANES:]
        re_cols.append(br * twr + bi * twi)
        im_cols.append(bi * twr - br * twi)
    rhs = jnp.concatenate([jnp.concatenate(re_cols, axis=1),
                           jnp.concatenate(im_cols, axis=1)], axis=0)
    y = jnp.dot(m3_ref[...], rhs, preferred_element_type=F32)
    for s16 in range(nsub // 2):
        parts = ([], [])
        for half in range(2):
            c0 = (2 * s16 + half) * SUBLANES
            sout_ref = scratch[nsub + 2 * s16 + half]
            for c in range(SUBLANES):
                lo = (c0 + c) * LANES
                sout_ref[pl.ds(c, h, stride=SUBLANES), :] = y[:h, lo:lo + LANES]
                sout_ref[pl.ds(h8 + c, h, stride=SUBLANES), :] = y[h:, lo:lo + LANES]
            parts[0].append(sout_ref[0:h8].reshape(h, SUBLANES, LANES))
            parts[1].append(sout_ref[h8:2 * h8].reshape(h, SUBLANES, LANES))
        lo16 = s16 * 2 * SUBLANES
        for r in range(2):
            o_ref[r, :, lo16:lo16 + 2 * SUBLANES, :] = jnp.concatenate(parts[r], axis=1).astype(BF16)


def _fftconv(zt, kf, c):
    B, h, C, _ = zt.shape
    N1 = c["N1"]
    nc = min(FFT_ROWS // N1, C)
    kern = functools.partial(_fft_kernel, N1=N1, nc=nc)
    blk = pl.BlockSpec((2, h, nc, LANES), lambda cb, p: (p, 0, cb, 0))
    consts = [c["m1"], c["m3"], c["twr"], c["twi"], c["r2f"], c["r2i"]]
    return pl.pallas_call(
        kern,
        grid=(C // nc, B // 2),
        in_specs=[blk, pl.BlockSpec((2, nc, N1, LANES), lambda cb, p: (0, cb, 0, 0))]
        + [_const_spec(a.shape) for a in consts],
        out_specs=blk,
        out_shape=jax.ShapeDtypeStruct(zt.shape, BF16),
        scratch_shapes=[pltpu.VMEM((N1 * SUBLANES, LANES), F32)] * (2 * nc // SUBLANES),
        compiler_params=_params(2),
        name="fft_conv",
    )(zt, kf, *consts)


def _fspec_kernel(hf_ref, hb_ref, d_ref, m1f_ref, twr_ref, twi_ref, r2f_ref, o_ref,
                  *scratch, N1, nc, scale):
    h = N1 // 2
    h8 = h * SUBLANES
    twr, twi = twr_ref[...], twi_ref[...]
    cols = []
    for s8 in range(nc // SUBLANES):
        s_ref = scratch[s8]
        s_ref[0:h8] = _slab8(hf_ref, (), s8 * SUBLANES, h)
        s_ref[h8:2 * h8] = _slab8(hb_ref, (), s8 * SUBLANES, h)
        for c in range(SUBLANES):
            cols += [_chan_rows(s_ref, 0, c, h), _chan_rows(s_ref, h8, c, h)]
    twr, twi = twr.astype(BF16), twi.astype(BF16)
    a = jnp.dot(m1f_ref[...], jnp.concatenate(cols, axis=1).astype(BF16),
                preferred_element_type=F32).astype(BF16)
    lhs = []
    for k in range(2 * nc):
        tr, ti = _cmul(a[:N1, k * LANES:(k + 1) * LANES], a[N1:, k * LANES:(k + 1) * LANES], twr, twi)
        lhs.append(jnp.concatenate([tr, ti], axis=1))
    lhs = jnp.concatenate(lhs, axis=0)
    X = jnp.dot(lhs, r2f_ref[...], preferred_element_type=F32)
    for c in range(nc):
        xf = X[(2 * c) * N1:(2 * c + 1) * N1]
        xb = X[(2 * c + 1) * N1:(2 * c + 2) * N1]
        d = d_ref[c:c + 1, :]
        o_ref[0, c] = (xf[:, :LANES] + xb[:, :LANES] + d) * scale
        o_ref[1, c] = (xf[:, LANES:] - xb[:, LANES:]) * scale


def _filter_spectrum(hf, hb, d, c):
    h, C, _ = hf.shape
    N1 = c["N1"]
    nc = min(SPECTRUM_ROWS // N1, C)
    kern = functools.partial(_fspec_kernel, N1=N1, nc=nc, scale=1.0 / (N1 * FFT_N2))
    blk = pl.BlockSpec((h, nc, LANES), lambda cb: (0, cb, 0))
    consts = [c["m1f"], c["twr"], c["twi"], c["r2f"]]
    return pl.pallas_call(
        kern,
        grid=(C // nc,),
        in_specs=[blk, blk, pl.BlockSpec((nc, 1), lambda cb: (cb, 0))]
        + [_const_spec(a.shape) for a in consts],
        out_specs=pl.BlockSpec((2, nc, N1, LANES), lambda cb: (0, cb, 0, 0)),
        out_shape=jax.ShapeDtypeStruct((2, C, N1, LANES), F32),
        scratch_shapes=[pltpu.VMEM((N1 * SUBLANES, LANES), F32)] * (nc // SUBLANES),
        compiler_params=_params(1),
        name="filter_spectrum",
    )(hf, hb, d, *consts)


def _t5_bucket(rel):
    half = N_BUCKETS // 2
    max_exact = half // 2
    ret = jnp.where(rel > 0, half, 0)
    n = jnp.abs(rel)
    nf = jnp.maximum(n, 1).astype(jnp.float32)
    large = max_exact + (jnp.log(nf / max_exact) / math.log(MAX_DISTANCE / max_exact)
                         * (half - max_exact)).astype(jnp.int32)
    large = jnp.minimum(large, half - 1)
    return ret + jnp.where(n < max_exact, n, large)


LOG2E = 1.4426950408889634


def _bias_kernel(rb_ref, bucket_ref, rel_ref, o_ref):
    bucket = bucket_ref[...]
    rel = rel_ref[...]
    key = lax.broadcasted_iota(jnp.int32, bucket.shape, 0)
    for h in range(N_HEADS_A):
        acc = jnp.zeros(bucket.shape, F32)
        for b in range(N_BUCKETS):
            acc = jnp.where(bucket == b, rb_ref[b, h], acc)
        acc = acc * LOG2E
        for v in range(4):
            ok = jnp.abs(rel) <= WINDOW
            if v & 1:
                ok = ok & (key >= BLOCK)
            if v & 2:
                ok = ok & (key < 2 * BLOCK)
            o_ref[v, h] = jnp.where(ok, acc, NEG)


def _bias_table(rel_bias):
    rel = (jnp.arange(3 * BLOCK)[:, None] - BLOCK) - jnp.arange(BLOCK)[None, :]
    rel = rel.astype(jnp.int32)
    bucket = _t5_bucket(rel).astype(jnp.int32)
    return pl.pallas_call(
        _bias_kernel,
        in_specs=[pl.BlockSpec(memory_space=pltpu.SMEM),
                  pl.BlockSpec(memory_space=pltpu.VMEM), pl.BlockSpec(memory_space=pltpu.VMEM)],
        out_specs=pl.BlockSpec(memory_space=pltpu.VMEM),
        out_shape=jax.ShapeDtypeStruct((4, N_HEADS_A, 3 * BLOCK, BLOCK), F32),
        name="rel_bias_table",
    )(rel_bias.astype(F32), bucket, rel)


def _group_ms(v, gmat):
    return jnp.dot((v * v).astype(BF16), gmat, preferred_element_type=F32)


def _ab_kernel(sink_ref, x_ref, xp_ref, xn_ref, g_ref, win_ref, wout_ref, gm_ref,
               kg_ref, bias_ref, vg_ref, ws_ref, bs_ref, o_ref, *, T):
    i = pl.program_id(1)
    nb = T // BLOCK
    nblocks = pl.num_programs(1) * nb
    gain = g_ref[...]
    x = x_ref[...]
    proj = jnp.dot(_rms(x, gain).astype(BF16), win_ref[...], preferred_element_type=F32)
    xh = jnp.concatenate([_rms(xp_ref[...], gain), _rms(xn_ref[...], gain)], axis=0)
    kvh = jnp.dot(xh.astype(BF16), win_ref[:, ATTN_W:ATTN_W + 2 * BLOCK],
                  preferred_element_type=F32)
    gm = gm_ref[...]
    gm_k = gm[:BLOCK, :BLOCK]

    k_all = jnp.concatenate([kvh[:BLOCK, :BLOCK], proj[:, ATTN_W:ATTN_W + BLOCK],
                             kvh[BLOCK:, :BLOCK]], axis=0)
    v_all = jnp.concatenate([kvh[:BLOCK, BLOCK:], proj[:, ATTN_W + BLOCK:ATTN_W + 2 * BLOCK],
                             kvh[BLOCK:, BLOCK:]], axis=0)
    kn = k_all * lax.rsqrt(_group_ms(k_all, gm_k) + EPS) * kg_ref[...]

    su = _gelu(proj[:, ATTN_W + 2 * BLOCK:ATTN_W + 2 * BLOCK + SGU_W])
    sv = _gelu(proj[:, ATTN_W + 2 * BLOCK + SGU_W:])
    svn = (sv * lax.rsqrt(_group_ms(sv, gm) + EPS) * vg_ref[...]).astype(BF16)
    low = lax.broadcasted_iota(jnp.int32, (1, BLOCK * nb), 1) % BLOCK < CH_B
    slabs = []
    for j in range(SGU_W // BLOCK):
        rhs = jnp.concatenate([svn[n * BLOCK:(n + 1) * BLOCK, j * BLOCK:(j + 1) * BLOCK]
                               for n in range(nb)], axis=1)
        a = jnp.dot(ws_ref[2 * j], rhs, preferred_element_type=F32)
        b = jnp.dot(ws_ref[2 * j + 1], rhs, preferred_element_type=F32)
        slabs.append(jnp.where(low, a, b))
    mixed = jnp.concatenate(
        [jnp.concatenate([slabs[j][:, n * BLOCK:(n + 1) * BLOCK] for j in range(SGU_W // BLOCK)],
                         axis=1) + bs_ref[...] for n in range(nb)], axis=0)
    sgu_out = jnp.dot((su * mixed).astype(BF16), wout_ref[ATTN_W:, :], preferred_element_type=F32)

    q_t = proj[:, :ATTN_W].T
    qt = []
    for h in range(N_HEADS_A):
        qh = q_t[h * HEAD_DIM:(h + 1) * HEAD_DIM]
        qt.append((qh * lax.rsqrt(jnp.mean(qh * qh, axis=0, keepdims=True) + EPS)).astype(BF16))
    qt = jnp.concatenate(qt, axis=0)
    vt = v_all.T.astype(BF16)
    knb = kn.astype(BF16)
    zeros_q = jnp.zeros((HEAD_DIM, GQA * BLOCK), BF16)
    ones_rows = jnp.ones((2 * SUBLANES, 3 * BLOCK), BF16)
    units = [(n, hk) for n in range(nb) for hk in range(N_KV_A)]
    sks = [jnp.concatenate([jnp.full((1, BLOCK), sink_ref[hk * GQA + g] * LOG2E, F32)
                            for g in range(GQA)], axis=1) for hk in range(N_KV_A)]
    scores = []
    for n, hk in units:
        blk = i * nb + n
        variant = jnp.where(blk == 0, 1, 0) + jnp.where(blk == nblocks - 1, 2, 0)
        qh = jnp.concatenate(
            [qt[(hk * GQA + g) * HEAD_DIM:(hk * GQA + g + 1) * HEAD_DIM, n * BLOCK:(n + 1) * BLOCK]
             for g in range(GQA)], axis=1)
        qz = jnp.concatenate([qh, zeros_q] if hk == 0 else [zeros_q, qh], axis=0)
        s = jnp.dot(knb[n * BLOCK:n * BLOCK + 3 * BLOCK, :], qz,
                    preferred_element_type=F32)
        scores.append(s + jnp.concatenate([bias_ref[variant, hk * GQA + g] for g in range(GQA)], axis=1))
    probs = []
    for (n, hk), s in zip(units, scores):
        m = jnp.maximum(jnp.max(s, axis=0, keepdims=True), sks[hk])
        probs.append((jnp.exp2(s - m).astype(BF16), jnp.exp2(sks[hk] - m)))
    outs = []
    for (n, hk), (p, psink) in zip(units, probs):
        lhs = jnp.concatenate([vt[hk * HEAD_DIM:(hk + 1) * HEAD_DIM, n * BLOCK:n * BLOCK + 3 * BLOCK],
                               ones_rows], axis=0)
        pv = jnp.dot(lhs, p, preferred_element_type=F32)
        outs.append(pv[:HEAD_DIM] / (pv[HEAD_DIM:HEAD_DIM + 1] + psink))
    out_cols = []
    for n in range(nb):
        out_cols.append(jnp.concatenate(
            [outs[n * N_KV_A + hk][:, g * BLOCK:(g + 1) * BLOCK]
             for hk in range(N_KV_A) for g in range(GQA)], axis=0))
    attn = jnp.concatenate(out_cols, axis=1).T.astype(BF16)
    o_ref[...] = x + sgu_out + jnp.dot(attn, wout_ref[:ATTN_W, :], preferred_element_type=F32)


def _ab_layer(x, sink, gain, win, wout, gm, kg, bias_tab, vg, ws, bs, T):
    B, L, D = x.shape
    hb = T // BLOCK
    last = L // BLOCK - 1
    kern = functools.partial(_ab_kernel, T=T)
    consts = [gain, win, wout, gm, kg, bias_tab, vg, ws, bs]
    return pl.pallas_call(
        kern,
        grid=(B, L // T),
        in_specs=[pl.BlockSpec(memory_space=pltpu.SMEM),
                  pl.BlockSpec((None, T, D), lambda b, i: (b, i, 0)),
                  pl.BlockSpec((None, BLOCK, D), lambda b, i: (b, jnp.maximum(i * hb - 1, 0), 0)),
                  pl.BlockSpec((None, BLOCK, D), lambda b, i: (b, jnp.minimum((i + 1) * hb, last), 0)),
                  ] + [_const_spec(a.shape) for a in consts],
        out_specs=pl.BlockSpec((None, T, D), lambda b, i: (b, i, 0)),
        out_shape=jax.ShapeDtypeStruct(x.shape, F32),
        compiler_params=_params(2),
        name="attn_sgu_mixer",
    )(sink, x, x, x, *consts)


def _prepare(p):
    depth = p["mix_norm"].shape[0]
    q = dict(depth=depth)
    q["mix_norm"] = [p["mix_norm"][l][None, :] for l in range(depth)]
    q["ffn_norm"] = [p["ffn_norm"][l][None, :] for l in range(depth)]
    q["ffn_wup"] = [p["ffn_w_up"][l].astype(BF16) for l in range(depth)]
    q["ffn_cw"] = [p["ffn_conv_w"][l] for l in range(depth)]
    q["ffn_cb"] = [p["ffn_conv_b"][l][None, :] for l in range(depth)]
    q["ffn_wdn"] = [p["ffn_w_down"][l].astype(BF16) for l in range(depth)]

    n_even = p["ab_w_in"].shape[0]
    q["ab_win"] = [p["ab_w_in"][i].astype(BF16) for i in range(n_even)]
    q["ab_wout"] = [p["ab_w_out"][i].astype(BF16) for i in range(n_even)]
    q["kg"] = [jnp.tile(p["k_norm"][i] * p["q_norm"][i] * (HEAD_DIM ** -0.5 * LOG2E), N_KV_A)[None, :]
               for i in range(n_even)]
    q["sink"] = [p["attn_sink"][i].astype(F32) for i in range(n_even)]
    q["vg"] = [p["sgu_v_norm"][i].reshape(1, SGU_W) for i in range(n_even)]
    q["ws"] = [p["sgu_w"][i].astype(BF16) for i in range(n_even)]
    q["bs"] = [jnp.repeat(p["sgu_b"][i].T, CH_B, axis=1) for i in range(n_even)]
    grp = np.arange(ATTN_W) // HEAD_DIM
    q["gm"] = jnp.asarray((grp[:, None] == grp[None, :]) / float(HEAD_DIM), dtype=BF16)

    n_odd = p["hy_w_in"].shape[0]
    q["hy_win"] = [p["hy_w_in"][i].astype(BF16) for i in range(n_odd)]
    q["hy_cw"] = [p["hy_conv_w"][i] for i in range(n_odd)]
    q["hy_cb"] = [p["hy_conv_b"][i][None, :] for i in range(n_odd)]
    q["hy_d"] = [p["hy_d"][i][:, None].astype(F32) for i in range(n_odd)]
    q["hy_wout"] = [p["hy_w_out"][i].astype(BF16) for i in range(n_odd)]
    slot = np.arange(LANES) % FILTER_WIDTH
    fr = np.linspace(1e-4, FILTER_BANDS - 1, FILTER_BANDS).astype(np.float32)
    frl = np.where((slot >= 1) & (slot <= 2 * FILTER_BANDS), fr[(slot - 1) % FILTER_BANDS], 0.0)
    ph = np.where((slot > FILTER_BANDS) & (slot <= 2 * FILTER_BANDS), 0.5 * np.pi, 0.0)
    q["frl"] = jnp.asarray(frl[None, :], dtype=F32)
    q["ph"] = jnp.asarray(ph[None, :], dtype=F32)
    q["delta"] = jnp.abs(jnp.linspace(MIN_DECAY, MAX_DECAY, D_MODEL, dtype=F32))[None, :]

    def twice(w):
        z = jnp.zeros_like(w)
        w = jnp.concatenate([jnp.concatenate([w, z], axis=1), jnp.concatenate([z, w], axis=1)], axis=0)
        hi = w.astype(BF16)
        return jnp.stack([hi, (w - hi.astype(F32)).astype(BF16)])

    filt = []
    for i in range(n_odd):
        w1p = jnp.zeros((FILTER_WIDTH, FILTER_WIDTH), F32).at[:FILTER_EMB].set(p["hy_f_w1"][i])
        row = lambda a: jnp.tile(a[None, :].astype(F32), (1, 2))
        filt.append((twice(w1p), row(p["hy_f_b1"][i]), row(p["hy_f_freq1"][i]),
                     twice(p["hy_f_w2"][i]), row(p["hy_f_b2"][i]), row(p["hy_f_freq2"][i]),
                     twice(p["hy_f_w3"][i]), row(p["hy_f_b3"][i]), row(p["hy_f_freq3"][i]),
                     twice(p["hy_f_wout"][i])))
    q["filt"] = filt
    q["bias_tab"] = _bias_table(p["rel_bias"])
    return q


def _hyena_conv(x, q, l, consts, T):
    i = l // 2
    L = x.shape[1]
    hf, hb = _filter(L, q["frl"], q["ph"], *q["filt"][i], q["delta"])
    kf = _filter_spectrum(hf, hb, q["hy_d"][i], consts)
    x0, zt = _hy_in(x, q["mix_norm"][l], q["hy_win"][i], q["hy_cw"][i], q["hy_cb"][i], T)
    return x0, _fftconv(zt, kf, consts)


def _trunk(x, q):
    B, L, D = x.shape
    consts = _dft_consts(L)
    t_mix, t_ffn = min(TOKENS_MIXER, L), min(TOKENS_FFN, L)
    for l in range(q["depth"]):
        i = l // 2
        ffn = (q["ffn_norm"][l], q["ffn_wup"][l], q["ffn_cw"][l], q["ffn_cb"][l], q["ffn_wdn"][l])
        if l % 2 == 0:
            x = _ab_layer(x, q["sink"][i], q["mix_norm"][l], q["ab_win"][i], q["ab_wout"][i],
                          q["gm"], q["kg"][i], q["bias_tab"], q["vg"][i], q["ws"][i],
                          q["bs"][i], t_mix)
            x = _ffn(x, *ffn, t_ffn)
        else:
            x0, yt = _hyena_conv(x, q, l, consts, t_mix)
            x = _hy_ffn(x, x0, yt, q["hy_wout"][i], *ffn, t_ffn)
    return x


def kernel(x_prompt, x_sample, rel_bias, mix_norm, ffn_norm, ab_w_in, q_norm, k_norm, attn_sink, sgu_v_norm, sgu_w, sgu_b, ab_w_out, hy_w_in, hy_conv_w, hy_conv_b, hy_f_w1, hy_f_b1, hy_f_freq1, hy_f_w2, hy_f_b2, hy_f_freq2, hy_f_w3, hy_f_b3, hy_f_freq3, hy_f_wout, hy_d, hy_w_out, ffn_w_up, ffn_conv_w, ffn_conv_b, ffn_w_down):
    p = dict(rel_bias=rel_bias, mix_norm=mix_norm, ffn_norm=ffn_norm, ab_w_in=ab_w_in,
             q_norm=q_norm, k_norm=k_norm, attn_sink=attn_sink, sgu_v_norm=sgu_v_norm,
             sgu_w=sgu_w, sgu_b=sgu_b, ab_w_out=ab_w_out, hy_w_in=hy_w_in, hy_conv_w=hy_conv_w,
             hy_conv_b=hy_conv_b, hy_f_w1=hy_f_w1, hy_f_b1=hy_f_b1, hy_f_freq1=hy_f_freq1,
             hy_f_w2=hy_f_w2, hy_f_b2=hy_f_b2, hy_f_freq2=hy_f_freq2, hy_f_w3=hy_f_w3,
             hy_f_b3=hy_f_b3, hy_f_freq3=hy_f_freq3, hy_f_wout=hy_f_wout, hy_d=hy_d,
             hy_w_out=hy_w_out, ffn_w_up=ffn_w_up, ffn_conv_w=ffn_conv_w, ffn_conv_b=ffn_conv_b,
             ffn_w_down=ffn_w_down)
    q = _prepare(p)
    return (_trunk(x_prompt, q), _trunk(x_sample, q))
```

```python
import functools
import math

import numpy as np
import jax
import jax.numpy as jnp
from jax import lax
from jax.experimental import pallas as pl
from jax.experimental.pallas import tpu as pltpu

F32 = jnp.float32
BF16 = jnp.bfloat16

D_MODEL = 1024
HEAD_DIM = 64
N_HEADS_A = 8
N_KV_A = 2
GQA = 4
ATTN_W = 512
WINDOW = 128
BLOCK = 128
N_BUCKETS = 32
MAX_DISTANCE = 128
SGU_W = 512
N_GROUPS_B = 8
CH_B = 64
FILTER_EMB = 33
FILTER_BANDS = 16
FILTER_WIDTH = 64
DECAY_TARGET = 1e-2
MIN_DECAY = math.log(DECAY_TARGET) / 1.5
MAX_DECAY = math.log(DECAY_TARGET) / 0.3
FFN_HIDDEN = 2816
EPS = 1e-6
NEG = -1e30

LANES = 128
SUBLANES = 8
TOKENS_FFN = 512
TOKENS_MIXER = 1024
FFN_CHUNK = 256
HY_CHUNK = 256
HALO = 8
ROW_BLOCK = 64
FFN_SLAB_SETS = FFN_HIDDEN // FFN_CHUNK
FFT_N2 = LANES
FFT_ROWS = 8192
SPECTRUM_ROWS = 4096
VMEM_LIMIT = 56 * 1024 * 1024


def _gelu(x):
    t = jnp.tanh(x * (0.7978845608028654 + (0.7978845608028654 * 0.044715) * (x * x)))
    hx = 0.5 * x
    return hx + hx * t


def _rms(x, gain):
    return x * lax.rsqrt(jnp.mean(x * x, axis=-1, keepdims=True) + EPS) * gain


def _const_spec(shape):
    nd = len(shape)
    return pl.BlockSpec(shape, lambda *_: (0,) * nd, pipeline_mode=pl.Buffered(1))


def _params(n_axes):
    return pltpu.CompilerParams(dimension_semantics=("arbitrary",) * n_axes,
                                vmem_limit_bytes=VMEM_LIMIT)


def _park_slabs(s_ref, base, h):
    for s in range(h.shape[1] // LANES):
        s_ref[base + s] = h[:, s * LANES:(s + 1) * LANES]


def _conv3_slab(s_ref, slab, w, b, r0, rows):
    lo = HALO + r0
    return (s_ref[slab, lo - 1:lo - 1 + rows, :] * w[0:1]
            + s_ref[slab, lo:lo + rows, :] * w[1:2]
            + s_ref[slab, lo + 1:lo + 1 + rows, :] * w[2:3]
            + b)


def _fill_normed_ext(xe_ref, x, xp, xn, gain, rows, first, last):
    pm = jnp.where(first, 0.0, 1.0)
    nm = jnp.where(last, 0.0, 1.0)
    xe_ref[0:HALO, :] = (_rms(xp, gain) * pm).astype(BF16)
    xe_ref[HALO:HALO + rows, :] = _rms(x, gain).astype(BF16)
    xe_ref[HALO + rows:HALO + rows + HALO, :] = (_rms(xn, gain) * nm).astype(BF16)


def _halo_specs(T, L, D):
    hb = T // HALO
    last = L // HALO - 1
    return [
        pl.BlockSpec((None, T, D), lambda b, i: (b, i, 0)),
        pl.BlockSpec((None, HALO, D), lambda b, i: (b, jnp.maximum(i * hb - 1, 0), 0)),
        pl.BlockSpec((None, HALO, D), lambda b, i: (b, jnp.minimum((i + 1) * hb, last), 0)),
    ]


def _ffn_body(x, xp, xn, g_ref, wup_ref, cw_ref, cb_ref, wdn_ref, o_ref, xe_ref, a_ref, s_ref,
              T, nchunk):
    i = pl.program_id(1)
    _fill_normed_ext(xe_ref, x, xp, xn, g_ref[...], T, i == 0, i == pl.num_programs(1) - 1)
    xe = xe_ref[...]
    ns = FFN_CHUNK // LANES
    for j in range(nchunk):
        base = (j % FFN_SLAB_SETS) * 2 * ns
        for part in range(2):
            lo = part * FFN_HIDDEN + j * FFN_CHUNK
            _park_slabs(s_ref, base + part * ns,
                        jnp.dot(xe, wup_ref[:, lo:lo + FFN_CHUNK], preferred_element_type=F32))
        for s in range(ns):
            col = j * FFN_CHUNK + s * LANES
            wg, bg = cw_ref[:, col:col + LANES], cb_ref[:, col:col + LANES]
            wu = cw_ref[:, FFN_HIDDEN + col:FFN_HIDDEN + col + LANES]
            bu = cb_ref[:, FFN_HIDDEN + col:FFN_HIDDEN + col + LANES]
            for r0 in range(0, T, ROW_BLOCK):
                g = _conv3_slab(s_ref, base + s, wg, bg, r0, ROW_BLOCK)
                u = _conv3_slab(s_ref, base + ns + s, wu, bu, r0, ROW_BLOCK)
                a_ref[r0:r0 + ROW_BLOCK, col:col + LANES] = (_gelu(g) * u).astype(BF16)
    o_ref[...] = x + jnp.dot(a_ref[...], wdn_ref[...], preferred_element_type=F32)


def _ffn_kernel(x_ref, xp_ref, xn_ref, g_ref, wup_ref, cw_ref, cb_ref, wdn_ref, o_ref,
                xe_ref, a_ref, s_ref, *, T, nchunk):
    _ffn_body(x_ref[...], xp_ref[...], xn_ref[...], g_ref, wup_ref, cw_ref, cb_ref, wdn_ref,
              o_ref, xe_ref, a_ref, s_ref, T, nchunk)


def _ffn_scratch(T, D):
    return [pltpu.VMEM((T + 2 * HALO, D), BF16), pltpu.VMEM((T, FFN_HIDDEN), BF16),
            pltpu.VMEM((FFN_SLAB_SETS * 2 * FFN_CHUNK // LANES, T + 2 * HALO, LANES), F32)]


def _hy_ffn_kernel(x_ref, xp_ref, xn_ref, x0_ref, x0p_ref, x0n_ref, yt_ref, ytp_ref, ytn_ref,
                   wo_ref, g_ref, wup_ref, cw_ref, cb_ref, wdn_ref, o_ref,
                   xe_ref, a_ref, s_ref, m_ref, *, T, nchunk):
    tok = lambda blk: blk.astype(F32).T
    m_ref[0:HALO, :] = (x0p_ref[...] * tok(ytp_ref[...])[BLOCK - HALO:, :]).astype(BF16)
    for n in range(T // BLOCK):
        lo = HALO + n * BLOCK
        m_ref[lo:lo + BLOCK, :] = (x0_ref[n * BLOCK:(n + 1) * BLOCK, :] * tok(yt_ref[n])).astype(BF16)
    m_ref[HALO + T:, :] = (x0n_ref[...] * tok(ytn_ref[...])[:HALO, :]).astype(BF16)
    mix = jnp.dot(m_ref[...], wo_ref[...], preferred_element_type=F32)
    _ffn_body(x_ref[...] + mix[HALO:HALO + T], xp_ref[...] + mix[:HALO], xn_ref[...] + mix[HALO + T:],
              g_ref, wup_ref, cw_ref, cb_ref, wdn_ref, o_ref, xe_ref, a_ref, s_ref, T, nchunk)


def _hy_ffn(x, x0, yt, wo, gain, wup_r, cw_r, cb_r, wdn, T):
    B, L, D = x.shape
    nchunk = FFN_HIDDEN // FFN_CHUNK
    nb = T // BLOCK
    last = L // BLOCK - 1
    kern = functools.partial(_hy_ffn_kernel, T=T, nchunk=nchunk)
    return pl.pallas_call(
        kern,
        grid=(B, L // T),
        in_specs=_halo_specs(T, L, D) + _halo_specs(T, L, D) + [
            pl.BlockSpec((None, nb, D, LANES), lambda b, i: (b, i, 0, 0)),
            pl.BlockSpec((None, None, D, LANES), lambda b, i: (b, jnp.maximum(i * nb - 1, 0), 0, 0)),
            pl.BlockSpec((None, None, D, LANES), lambda b, i: (b, jnp.minimum((i + 1) * nb, last), 0, 0)),
            _const_spec(wo.shape), _const_spec((1, D)), _const_spec(wup_r.shape),
            _const_spec(cw_r.shape), _const_spec(cb_r.shape), _const_spec(wdn.shape)],
        out_specs=pl.BlockSpec((None, T, D), lambda b, i: (b, i, 0)),
        out_shape=jax.ShapeDtypeStruct(x.shape, F32),
        scratch_shapes=_ffn_scratch(T, D) + [pltpu.VMEM((T + 2 * HALO, D), BF16)],
        compiler_params=_params(2),
        name="hyena_out_conv_ffn",
    )(x, x, x, x0, x0, x0, yt, yt, yt, wo, gain, wup_r, cw_r, cb_r, wdn)


def _ffn(x, gain, wup_r, cw_r, cb_r, wdn, T):
    B, L, D = x.shape
    nchunk = FFN_HIDDEN // FFN_CHUNK
    kern = functools.partial(_ffn_kernel, T=T, nchunk=nchunk)
    return pl.pallas_call(
        kern,
        grid=(B, L // T),
        in_specs=_halo_specs(T, L, D) + [
            _const_spec((1, D)), _const_spec(wup_r.shape), _const_spec(cw_r.shape),
            _const_spec(cb_r.shape), _const_spec(wdn.shape)],
        out_specs=pl.BlockSpec((None, T, D), lambda b, i: (b, i, 0)),
        out_shape=jax.ShapeDtypeStruct(x.shape, F32),
        scratch_shapes=_ffn_scratch(T, D),
        compiler_params=_params(2),
        name="conv_ffn",
    )(x, x, x, gain, wup_r, cw_r, cb_r, wdn)


def _hy_in_kernel(x_ref, xp_ref, xn_ref, g_ref, win_ref, cw_ref, cb_ref, x0_ref, zt_ref,
                  xe_ref, s_ref, *, T, nchunk):
    i = pl.program_id(1)
    _fill_normed_ext(xe_ref, x_ref[...], xp_ref[...], xn_ref[...], g_ref[...], T,
                     i == 0, i == pl.num_programs(1) - 1)
    xe = xe_ref[...]
    ns = HY_CHUNK // LANES
    for j in range(nchunk):
        base = (j % 2) * 3 * ns
        for k in range(3):
            lo = k * D_MODEL + j * HY_CHUNK
            _park_slabs(s_ref, base + k * ns,
                        jnp.dot(xe, win_ref[:, lo:lo + HY_CHUNK], preferred_element_type=F32))
        for s in range(ns):
            col = j * HY_CHUNK + s * LANES
            w = [cw_ref[:, k * D_MODEL + col:k * D_MODEL + col + LANES] for k in range(3)]
            b = [cb_ref[:, k * D_MODEL + col:k * D_MODEL + col + LANES] for k in range(3)]
            for n in range(T // BLOCK):
                r0 = n * BLOCK
                x0_ref[r0:r0 + BLOCK, col:col + LANES] = _conv3_slab(s_ref, base + s, w[0], b[0], r0, BLOCK)
                z = (_conv3_slab(s_ref, base + ns + s, w[1], b[1], r0, BLOCK)
                     * _conv3_slab(s_ref, base + 2 * ns + s, w[2], b[2], r0, BLOCK))
                zt_ref[n, col:col + LANES, :] = z.T.astype(BF16)


def _hy_in(x, gain, win_r, cw_r, cb_r, T):
    B, L, D = x.shape
    nchunk = D // HY_CHUNK
    kern = functools.partial(_hy_in_kernel, T=T, nchunk=nchunk)
    return pl.pallas_call(
        kern,
        grid=(B, L // T),
        in_specs=_halo_specs(T, L, D) + [
            _const_spec((1, D)), _const_spec(win_r.shape), _const_spec(cw_r.shape),
            _const_spec(cb_r.shape)],
        out_specs=[pl.BlockSpec((None, T, D), lambda b, i: (b, i, 0)),
                   pl.BlockSpec((None, T // BLOCK, D, LANES), lambda b, i: (b, i, 0, 0))],
        out_shape=[jax.ShapeDtypeStruct(x.shape, F32),
                   jax.ShapeDtypeStruct((B, L // BLOCK, D, LANES), BF16)],
        scratch_shapes=[pltpu.VMEM((T + 2 * HALO, D), BF16),
                        pltpu.VMEM((2 * 3 * HY_CHUNK // LANES, T + 2 * HALO, LANES), F32)],
        compiler_params=_params(2),
        name="hyena_in",
    )(x, x, x, gain, win_r, cw_r, cb_r)


def _dot_f32(a, w_ref):
    a_hi = a.astype(BF16)
    a_lo = (a - a_hi.astype(F32)).astype(BF16)
    w_hi, w_lo = w_ref[0], w_ref[1]
    return (jnp.dot(a_hi, w_hi, preferred_element_type=F32)
            + jnp.dot(a_lo, w_hi, preferred_element_type=F32)
            + jnp.dot(a_hi, w_lo, preferred_element_type=F32))


def _filter_kernel(frl_ref, ph_ref, w1_ref, b1_ref, f1_ref, w2_ref, b2_ref, f2_ref, w3_ref, b3_ref,
                   f3_ref, wout_ref, delta_ref, hf_ref, hb_ref, *, L, R):
    hr = R // 2
    ja = pl.program_id(0) * R + lax.broadcasted_iota(jnp.int32, (hr, 1), 0)
    jb = ja + hr
    lane = lax.broadcasted_iota(jnp.int32, (hr, LANES), 1)
    j = jnp.where(lane < FILTER_WIDTH, ja, jb).astype(F32)
    t = j / float(L - 1)
    w = (2.0 * math.pi) * j / float(L)
    feats = jnp.where(lane % FILTER_WIDTH == 0, t, jnp.cos(w * frl_ref[...] + ph_ref[...]))
    h = jnp.sin(f1_ref[...] * (_dot_f32(feats, w1_ref) + b1_ref[...]))
    h = jnp.sin(f2_ref[...] * (_dot_f32(h, w2_ref) + b2_ref[...]))
    h = jnp.sin(f3_ref[...] * (_dot_f32(h, w3_ref) + b3_ref[...]))
    h = _dot_f32(h, wout_ref)
    for half, jidx in enumerate((ja, jb)):
        decay = jnp.exp(-(jidx.astype(F32) / float(L - 1)) * delta_ref[...])
        base = half * 2 * D_MODEL
        hf = h[:, base:base + D_MODEL] * decay
        hb = jnp.where(jidx == 0, 0.0, h[:, base + D_MODEL:base + 2 * D_MODEL] * decay)
        for n in range(hr // BLOCK):
            blk = half * (hr // BLOCK) + n
            hf_ref[blk] = hf[n * BLOCK:(n + 1) * BLOCK, :].T
            hb_ref[blk] = hb[n * BLOCK:(n + 1) * BLOCK, :].T


def _filter(L, frl, ph, w1, b1, f1, w2, b2, f2, w3, b3, f3, wout, delta):
    R = 512
    kern = functools.partial(_filter_kernel, L=L, R=R)
    consts = [frl, ph, w1, b1, f1, w2, b2, f2, w3, b3, f3, wout, delta]
    out_spec = pl.BlockSpec((R // BLOCK, D_MODEL, LANES), lambda i: (i, 0, 0))
    shape = jax.ShapeDtypeStruct((L // BLOCK, D_MODEL, LANES), F32)
    return pl.pallas_call(
        kern,
        grid=(L // R,),
        in_specs=[_const_spec(a.shape) for a in consts],
        out_specs=[out_spec, out_spec],
        out_shape=[shape, shape],
        compiler_params=_params(1),
        name="hyena_filter",
    )(*consts)


def _dft_consts(L):
    N = 2 * L
    N2 = FFT_N2
    N1 = N // N2
    h = N1 // 2
    a1 = -2.0 * np.pi * np.outer(np.arange(N1), np.arange(N1)) / N1
    f1r, f1i = np.cos(a1), np.sin(a1)
    a2 = -2.0 * np.pi * np.outer(np.arange(N2), np.arange(N2)) / N2
    f2r, f2i = np.cos(a2), np.sin(a2)
    at = -2.0 * np.pi * np.outer(np.arange(N1), np.arange(N2)) / N
    m1 = np.block([[f1r[:, :h], -f1i[:, :h]], [f1i[:, :h], f1r[:, :h]]])
    m1f = np.concatenate([f1r[:, :h], f1i[:, :h]], axis=0)
    m3 = np.block([[f1r[:h], f1i[:h]], [-f1i[:h], f1r[:h]]])
    c = lambda a: jnp.asarray(a, dtype=BF16)
    f = lambda a: jnp.asarray(a, dtype=F32)
    return dict(N1=N1, m1=c(m1), m1f=c(m1f), m3=c(m3),
                r2f=c(np.block([[f2r, f2i], [-f2i, f2r]])),
                r2i=c(np.block([[f2r, -f2i], [f2i, f2r]])),
                twr=f(np.cos(at)), twi=f(np.sin(at)))


def _cmul(ar, ai, br, bi):
    return ar * br - ai * bi, ar * bi + ai * br


def _slab8(ref, idx, c0, n):
    v = ref[idx + (slice(None), slice(c0, c0 + SUBLANES), slice(None))]
    return v.reshape(n * SUBLANES, LANES)


def _chan_rows(s_ref, base, c, n):
    return s_ref[pl.ds(base + c, n, stride=SUBLANES), :]


def _fft_kernel(x_ref, kf_ref, m1_ref, m3_ref, twr_ref, twi_ref, r2f_ref, r2i_ref, o_ref,
                *scratch, N1, nc):
    h = N1 // 2
    h8 = h * SUBLANES
    nsub = nc // SUBLANES
    twr, twi = twr_ref[...], twi_ref[...]
    cols = []
    for s16 in range(nsub // 2):
        lo16 = s16 * 2 * SUBLANES
        xr = x_ref[0, :, lo16:lo16 + 2 * SUBLANES, :].astype(F32)
        xi = x_ref[1, :, lo16:lo16 + 2 * SUBLANES, :].astype(F32)
        for half in range(2):
            sin_ref = scratch[2 * s16 + half]
            rows = slice(half * SUBLANES, (half + 1) * SUBLANES)
            sin_ref[0:h8] = xr[:, rows, :].reshape(h8, LANES)
            sin_ref[h8:2 * h8] = xi[:, rows, :].reshape(h8, LANES)
            cols += [jnp.concatenate([_chan_rows(sin_ref, 0, c, h), _chan_rows(sin_ref, h8, c, h)], axis=0)
                     for c in range(SUBLANES)]
    twr, twi = twr.astype(BF16), twi.astype(BF16)
    a = jnp.dot(m1_ref[...], jnp.concatenate(cols, axis=1).astype(BF16),
                preferred_element_type=F32).astype(BF16)
    lhs = []
    for c in range(nc):
        tr, ti = _cmul(a[:N1, c * LANES:(c + 1) * LANES], a[N1:, c * LANES:(c + 1) * LANES], twr, twi)
        lhs.append(jnp.concatenate([tr, ti], axis=1))
    lhs = jnp.concatenate(lhs, axis=0)
    X = jnp.dot(lhs, r2f_ref[...], preferred_element_type=F32)
    yr, yi = _cmul(X[:, :LANES], X[:, LANES:],
                   kf_ref[0].astype(F32).reshape(nc * N1, LANES),
                   kf_ref[1].astype(F32).reshape(nc * N1, LANES))
    Bm = jnp.dot(jnp.concatenate([yr, yi], axis=1).astype(BF16), r2i_ref[...],
                 preferred_element_type=F32).astype(BF16)
    re_cols, im_cols = [], []
    for c in range(nc):
        br, bi = Bm[c * N1:(c + 1) * N1, :LANES], Bm[c * N1:(c + 1) * N1, LANES:]
        re_cols.append(br * twr + bi * twi)
        im_cols.append(bi * twr - br * twi)
    rhs = jnp.concatenate([jnp.concatenate(re_cols, axis=1),
                           jnp.concatenate(im_cols, axis=1)], axis=0)
    y = jnp.dot(m3_ref[...], rhs, preferred_element_type=F32)
    for s16 in range(nsub // 2):
        parts = ([], [])
        for half in range(2):
            c0 = (2 * s16 + half) * SUBLANES
            sout_ref = scratch[nsub + 2 * s16 + half]
            for c in range(SUBLANES):
                lo = (c0 + c) * LANES
                sout_ref[pl.ds(c, h, stride=SUBLANES), :] = y[:h, lo:lo + LANES]
                sout_ref[pl.ds(h8 + c, h, stride=SUBLANES), :] = y[h:, lo:lo + LANES]
            parts[0].append(sout_ref[0:h8].reshape(h, SUBLANES, LANES))
            parts[1].append(sout_ref[h8:2 * h8].reshape(h, SUBLANES, LANES))
        lo16 = s16 * 2 * SUBLANES
        for r in range(2):
            o_ref[r, :, lo16:lo16 + 2 * SUBLANES, :] = jnp.concatenate(parts[r], axis=1).astype(BF16)


def _fftconv(zt, kf, c):
    B, h, C, _ = zt.shape
    N1 = c["N1"]
    nc = min(FFT_ROWS // N1, C)
    kern = functools.partial(_fft_kernel, N1=N1, nc=nc)
    blk = pl.BlockSpec((2, h, nc, LANES), lambda cb, p: (p, 0, cb, 0))
    consts = [c["m1"], c["m3"], c["twr"], c["twi"], c["r2f"], c["r2i"]]
    return pl.pallas_call(
        kern,
        grid=(C // nc, B // 2),
        in_specs=[blk, pl.BlockSpec((2, nc, N1, LANES), lambda cb, p: (0, cb, 0, 0))]
        + [_const_spec(a.shape) for a in consts],
        out_specs=blk,
        out_shape=jax.ShapeDtypeStruct(zt.shape, BF16),
        scratch_shapes=[pltpu.VMEM((N1 * SUBLANES, LANES), F32)] * (2 * nc // SUBLANES),
        compiler_params=_params(2),
        name="fft_conv",
    )(zt, kf, *consts)


def _fspec_kernel(hf_ref, hb_ref, d_ref, m1f_ref, twr_ref, twi_ref, r2f_ref, o_ref,
                  *scratch, N1, nc, scale):
    h = N1 // 2
    h8 = h * SUBLANES
    twr, twi = twr_ref[...], twi_ref[...]
    cols = []
    for s8 in range(nc // SUBLANES):
        s_ref = scratch[s8]
        s_ref[0:h8] = _slab8(hf_ref, (), s8 * SUBLANES, h)
        s_ref[h8:2 * h8] = _slab8(hb_ref, (), s8 * SUBLANES, h)
        for c in range(SUBLANES):
            cols += [_chan_rows(s_ref, 0, c, h), _chan_rows(s_ref, h8, c, h)]
    twr, twi = twr.astype(BF16), twi.astype(BF16)
    a = jnp.dot(m1f_ref[...], jnp.concatenate(cols, axis=1).astype(BF16),
                preferred_element_type=F32).astype(BF16)
    lhs = []
    for k in range(2 * nc):
        tr, ti = _cmul(a[:N1, k * LANES:(k + 1) * LANES], a[N1:, k * LANES:(k + 1) * LANES], twr, twi)
        lhs.append(jnp.concatenate([tr, ti], axis=1))
    lhs = jnp.concatenate(lhs, axis=0)
    X = jnp.dot(lhs, r2f_ref[...], preferred_element_type=F32)
    for c in range(nc):
        xf = X[(2 * c) * N1:(2 * c + 1) * N1]
        xb = X[(2 * c + 1) * N1:(2 * c + 2) * N1]
        d = d_ref[c:c + 1, :]
        o_ref[0, c] = ((xf[:, :LANES] + xb[:, :LANES] + d) * scale).astype(BF16)
        o_ref[1, c] = ((xf[:, LANES:] - xb[:, LANES:]) * scale).astype(BF16)


def _filter_spectrum(hf, hb, d, c):
    h, C, _ = hf.shape
    N1 = c["N1"]
    nc = min(SPECTRUM_ROWS // N1, C)
    kern = functools.partial(_fspec_kernel, N1=N1, nc=nc, scale=1.0 / (N1 * FFT_N2))
    blk = pl.BlockSpec((h, nc, LANES), lambda cb: (0, cb, 0))
    consts = [c["m1f"], c["twr"], c["twi"], c["r2f"]]
    return pl.pallas_call(
        kern,
        grid=(C // nc,),
        in_specs=[blk, blk, pl.BlockSpec((nc, 1), lambda cb: (cb, 0))]
        + [_const_spec(a.shape) for a in consts],
        out_specs=pl.BlockSpec((2, nc, N1, LANES), lambda cb: (0, cb, 0, 0)),
        out_shape=jax.ShapeDtypeStruct((2, C, N1, LANES), BF16),
        scratch_shapes=[pltpu.VMEM((N1 * SUBLANES, LANES), F32)] * (nc // SUBLANES),
        compiler_params=_params(1),
        name="filter_spectrum",
    )(hf, hb, d, *consts)


def _t5_bucket(rel):
    half = N_BUCKETS // 2
    max_exact = half // 2
    ret = jnp.where(rel > 0, half, 0)
    n = jnp.abs(rel)
    nf = jnp.maximum(n, 1).astype(jnp.float32)
    large = max_exact + (jnp.log(nf / max_exact) / math.log(MAX_DISTANCE / max_exact)
                         * (half - max_exact)).astype(jnp.int32)
    large = jnp.minimum(large, half - 1)
    return ret + jnp.where(n < max_exact, n, large)


LOG2E = 1.4426950408889634


def _bias_kernel(rb_ref, bucket_ref, rel_ref, o_ref):
    bucket = bucket_ref[...]
    rel = rel_ref[...]
    key = lax.broadcasted_iota(jnp.int32, bucket.shape, 0)
    for h in range(N_HEADS_A):
        acc = jnp.zeros(bucket.shape, F32)
        for b in range(N_BUCKETS):
            acc = jnp.where(bucket == b, rb_ref[b, h], acc)
        acc = acc * LOG2E
        for v in range(4):
            ok = jnp.abs(rel) <= WINDOW
            if v & 1:
                ok = ok & (key >= BLOCK)
            if v & 2:
                ok = ok & (key < 2 * BLOCK)
            o_ref[v, h] = jnp.where(ok, acc, NEG)


def _bias_table(rel_bias):
    rel = (jnp.arange(3 * BLOCK)[:, None] - BLOCK) - jnp.arange(BLOCK)[None, :]
    rel = rel.astype(jnp.int32)
    bucket = _t5_bucket(rel).astype(jnp.int32)
    return pl.pallas_call(
        _bias_kernel,
        in_specs=[pl.BlockSpec(memory_space=pltpu.SMEM),
                  pl.BlockSpec(memory_space=pltpu.VMEM), pl.BlockSpec(memory_space=pltpu.VMEM)],
        out_specs=pl.BlockSpec(memory_space=pltpu.VMEM),
        out_shape=jax.ShapeDtypeStruct((4, N_HEADS_A, 3 * BLOCK, BLOCK), F32),
        name="rel_bias_table",
    )(rel_bias.astype(F32), bucket, rel)


def _group_ms(v, gmat):
    return jnp.dot((v * v).astype(BF16), gmat, preferred_element_type=F32)


def _ab_kernel(sink_ref, x_ref, xp_ref, xn_ref, g_ref, win_ref, wout_ref, gm_ref,
               kg_ref, bias_ref, vg_ref, ws_ref, bs_ref, o_ref, *, T):
    i = pl.program_id(1)
    nb = T // BLOCK
    nblocks = pl.num_programs(1) * nb
    gain = g_ref[...]
    x = x_ref[...]
    proj = jnp.dot(_rms(x, gain).astype(BF16), win_ref[...], preferred_element_type=F32)
    xh = jnp.concatenate([_rms(xp_ref[...], gain), _rms(xn_ref[...], gain)], axis=0)
    kvh = jnp.dot(xh.astype(BF16), win_ref[:, ATTN_W:ATTN_W + 2 * BLOCK],
                  preferred_element_type=F32)
    gm = gm_ref[...]
    gm_k = gm[:BLOCK, :BLOCK]

    k_all = jnp.concatenate([kvh[:BLOCK, :BLOCK], proj[:, ATTN_W:ATTN_W + BLOCK],
                             kvh[BLOCK:, :BLOCK]], axis=0)
    v_all = jnp.concatenate([kvh[:BLOCK, BLOCK:], proj[:, ATTN_W + BLOCK:ATTN_W + 2 * BLOCK],
                             kvh[BLOCK:, BLOCK:]], axis=0)
    kn = k_all * lax.rsqrt(_group_ms(k_all, gm_k) + EPS) * kg_ref[...]

    su = _gelu(proj[:, ATTN_W + 2 * BLOCK:ATTN_W + 2 * BLOCK + SGU_W])
    sv = _gelu(proj[:, ATTN_W + 2 * BLOCK + SGU_W:])
    svn = (sv * lax.rsqrt(_group_ms(sv, gm) + EPS) * vg_ref[...]).astype(BF16)
    low = lax.broadcasted_iota(jnp.int32, (1, BLOCK * nb), 1) % BLOCK < CH_B
    slabs = []
    for j in range(SGU_W // BLOCK):
        rhs = jnp.concatenate([svn[n * BLOCK:(n + 1) * BLOCK, j * BLOCK:(j + 1) * BLOCK]
                               for n in range(nb)], axis=1)
        a = jnp.dot(ws_ref[2 * j], rhs, preferred_element_type=F32)
        b = jnp.dot(ws_ref[2 * j + 1], rhs, preferred_element_type=F32)
        slabs.append(jnp.where(low, a, b))
    mixed = jnp.concatenate(
        [jnp.concatenate([slabs[j][:, n * BLOCK:(n + 1) * BLOCK] for j in range(SGU_W // BLOCK)],
                         axis=1) + bs_ref[...] for n in range(nb)], axis=0)
    sgu_out = jnp.dot((su * mixed).astype(BF16), wout_ref[ATTN_W:, :], preferred_element_type=F32)

    q_t = proj[:, :ATTN_W].T
    qt = []
    for h in range(N_HEADS_A):
        qh = q_t[h * HEAD_DIM:(h + 1) * HEAD_DIM]
        qt.append((qh * lax.rsqrt(jnp.mean(qh * qh, axis=0, keepdims=True) + EPS)).astype(BF16))
    qt = jnp.concatenate(qt, axis=0)
    vt = v_all.T.astype(BF16)
    knb = kn.astype(BF16)
    zeros_q = jnp.zeros((HEAD_DIM, GQA * BLOCK), BF16)
    ones_rows = jnp.ones((2 * SUBLANES, 3 * BLOCK), BF16)
    units = [(n, hk) for n in range(nb) for hk in range(N_KV_A)]
    sks = [jnp.concatenate([jnp.full((1, BLOCK), sink_ref[hk * GQA + g] * LOG2E, F32)
                            for g in range(GQA)], axis=1) for hk in range(N_KV_A)]
    scores = []
    for n, hk in units:
        blk = i * nb + n
        variant = jnp.where(blk == 0, 1, 0) + jnp.where(blk == nblocks - 1, 2, 0)
        qh = jnp.concatenate(
            [qt[(hk * GQA + g) * HEAD_DIM:(hk * GQA + g + 1) * HEAD_DIM, n * BLOCK:(n + 1) * BLOCK]
             for g in range(GQA)], axis=1)
        qz = jnp.concatenate([qh, zeros_q] if hk == 0 else [zeros_q, qh], axis=0)
        s = jnp.dot(knb[n * BLOCK:n * BLOCK + 3 * BLOCK, :], qz,
                    preferred_element_type=F32)
        scores.append(s + jnp.concatenate([bias_ref[variant, hk * GQA + g] for g in range(GQA)], axis=1))
    probs = []
    for (n, hk), s in zip(units, scores):
        m = jnp.maximum(jnp.max(s, axis=0, keepdims=True), sks[hk])
        probs.append((jnp.exp2(s - m).astype(BF16), jnp.exp2(sks[hk] - m)))
    outs = []
    for (n, hk), (p, psink) in zip(units, probs):
        lhs = jnp.concatenate([vt[hk * HEAD_DIM:(hk + 1) * HEAD_DIM, n * BLOCK:n * BLOCK + 3 * BLOCK],
                               ones_rows], axis=0)
        pv = jnp.dot(lhs, p, preferred_element_type=F32)
        outs.append(pv[:HEAD_DIM] / (pv[HEAD_DIM:HEAD_DIM + 1] + psink))
    out_cols = []
    for n in range(nb):
        out_cols.append(jnp.concatenate(
            [outs[n * N_KV_A + hk][:, g * BLOCK:(g + 1) * BLOCK]
             for hk in range(N_KV_A) for g in range(GQA)], axis=0))
    attn = jnp.concatenate(out_cols, axis=1).T.astype(BF16)
    o_ref[...] = x + sgu_out + jnp.dot(attn, wout_ref[:ATTN_W, :], preferred_element_type=F32)


def _ab_layer(x, sink, gain, win, wout, gm, kg, bias_tab, vg, ws, bs, T):
    B, L, D = x.shape
    hb = T // BLOCK
    last = L // BLOCK - 1
    kern = functools.partial(_ab_kernel, T=T)
    consts = [gain, win, wout, gm, kg, bias_tab, vg, ws, bs]
    return pl.pallas_call(
        kern,
        grid=(B, L // T),
        in_specs=[pl.BlockSpec(memory_space=pltpu.SMEM),
                  pl.BlockSpec((None, T, D), lambda b, i: (b, i, 0)),
                  pl.BlockSpec((None, BLOCK, D), lambda b, i: (b, jnp.maximum(i * hb - 1, 0), 0)),
                  pl.BlockSpec((None, BLOCK, D), lambda b, i: (b, jnp.minimum((i + 1) * hb, last), 0)),
                  ] + [_const_spec(a.shape) for a in consts],
        out_specs=pl.BlockSpec((None, T, D), lambda b, i: (b, i, 0)),
        out_shape=jax.ShapeDtypeStruct(x.shape, F32),
        compiler_params=_params(2),
        name="attn_sgu_mixer",
    )(sink, x, x, x, *consts)


def _prepare(p):
    depth = p["mix_norm"].shape[0]
    q = dict(depth=depth)
    q["mix_norm"] = [p["mix_norm"][l][None, :] for l in range(depth)]
    q["ffn_norm"] = [p["ffn_norm"][l][None, :] for l in range(depth)]
    q["ffn_wup"] = [p["ffn_w_up"][l].astype(BF16) for l in range(depth)]
    q["ffn_cw"] = [p["ffn_conv_w"][l] for l in range(depth)]
    q["ffn_cb"] = [p["ffn_conv_b"][l][None, :] for l in range(depth)]
    q["ffn_wdn"] = [p["ffn_w_down"][l].astype(BF16) for l in range(depth)]

    n_even = p["ab_w_in"].shape[0]
    q["ab_win"] = [p["ab_w_in"][i].astype(BF16) for i in range(n_even)]
    q["ab_wout"] = [p["ab_w_out"][i].astype(BF16) for i in range(n_even)]
    q["kg"] = [jnp.tile(p["k_norm"][i] * p["q_norm"][i] * (HEAD_DIM ** -0.5 * LOG2E), N_KV_A)[None, :]
               for i in range(n_even)]
    q["sink"] = [p["attn_sink"][i].astype(F32) for i in range(n_even)]
    q["vg"] = [p["sgu_v_norm"][i].reshape(1, SGU_W) for i in range(n_even)]
    q["ws"] = [p["sgu_w"][i].astype(BF16) for i in range(n_even)]
    q["bs"] = [jnp.repeat(p["sgu_b"][i].T, CH_B, axis=1) for i in range(n_even)]
    grp = np.arange(ATTN_W) // HEAD_DIM
    q["gm"] = jnp.asarray((grp[:, None] == grp[None, :]) / float(HEAD_DIM), dtype=BF16)

    n_odd = p["hy_w_in"].shape[0]
    q["hy_win"] = [p["hy_w_in"][i].astype(BF16) for i in range(n_odd)]
    q["hy_cw"] = [p["hy_conv_w"][i] for i in range(n_odd)]
    q["hy_cb"] = [p["hy_conv_b"][i][None, :] for i in range(n_odd)]
    q["hy_d"] = [p["hy_d"][i][:, None].astype(F32) for i in range(n_odd)]
    q["hy_wout"] = [p["hy_w_out"][i].astype(BF16) for i in range(n_odd)]
    slot = np.arange(LANES) % FILTER_WIDTH
    fr = np.linspace(1e-4, FILTER_BANDS - 1, FILTER_BANDS).astype(np.float32)
    frl = np.where((slot >= 1) & (slot <= 2 * FILTER_BANDS), fr[(slot - 1) % FILTER_BANDS], 0.0)
    ph = np.where((slot > FILTER_BANDS) & (slot <= 2 * FILTER_BANDS), 0.5 * np.pi, 0.0)
    q["frl"] = jnp.asarray(frl[None, :], dtype=F32)
    q["ph"] = jnp.asarray(ph[None, :], dtype=F32)
    q["delta"] = jnp.abs(jnp.linspace(MIN_DECAY, MAX_DECAY, D_MODEL, dtype=F32))[None, :]

    def twice(w):
        z = jnp.zeros_like(w)
        w = jnp.concatenate([jnp.concatenate([w, z], axis=1), jnp.concatenate([z, w], axis=1)], axis=0)
        hi = w.astype(BF16)
        return jnp.stack([hi, (w - hi.astype(F32)).astype(BF16)])

    filt = []
    for i in range(n_odd):
        w1p = jnp.zeros((FILTER_WIDTH, FILTER_WIDTH), F32).at[:FILTER_EMB].set(p["hy_f_w1"][i])
        row = lambda a: jnp.tile(a[None, :].astype(F32), (1, 2))
        filt.append((twice(w1p), row(p["hy_f_b1"][i]), row(p["hy_f_freq1"][i]),
                     twice(p["hy_f_w2"][i]), row(p["hy_f_b2"][i]), row(p["hy_f_freq2"][i]),
                     twice(p["hy_f_w3"][i]), row(p["hy_f_b3"][i]), row(p["hy_f_freq3"][i]),
                     twice(p["hy_f_wout"][i])))
    q["filt"] = filt
    q["bias_tab"] = _bias_table(p["rel_bias"])
    return q


def _hyena_conv(x, q, l, consts, T):
    i = l // 2
    L = x.shape[1]
    hf, hb = _filter(L, q["frl"], q["ph"], *q["filt"][i], q["delta"])
    kf = _filter_spectrum(hf, hb, q["hy_d"][i], consts)
    x0, zt = _hy_in(x, q["mix_norm"][l], q["hy_win"][i], q["hy_cw"][i], q["hy_cb"][i], T)
    return x0, _fftconv(zt, kf, consts)


def _trunk(x, q):
    B, L, D = x.shape
    consts = _dft_consts(L)
    t_mix, t_ffn = min(TOKENS_MIXER, L), min(TOKENS_FFN, L)
    for l in range(q["depth"]):
        i = l // 2
        ffn = (q["ffn_norm"][l], q["ffn_wup"][l], q["ffn_cw"][l], q["ffn_cb"][l], q["ffn_wdn"][l])
        if l % 2 == 0:
            x = _ab_layer(x, q["sink"][i], q["mix_norm"][l], q["ab_win"][i], q["ab_wout"][i],
                          q["gm"], q["kg"][i], q["bias_tab"], q["vg"][i], q["ws"][i],
                          q["bs"][i], t_mix)
            x = _ffn(x, *ffn, t_ffn)
        else:
            x0, yt = _hyena_conv(x, q, l, consts, t_mix)
            x = _hy_ffn(x, x0, yt, q["hy_wout"][i], *ffn, t_ffn)
    return x


def kernel(x_prompt, x_sample, rel_bias, mix_norm, ffn_norm, ab_w_in, q_norm, k_norm, attn_sink, sgu_v_norm, sgu_w, sgu_b, ab_w_out, hy_w_in, hy_conv_w, hy_conv_b, hy_f_w1, hy_f_b1, hy_f_freq1, hy_f_w2, hy_f_b2, hy_f_freq2, hy_f_w3, hy_f_b3, hy_f_freq3, hy_f_wout, hy_d, hy_w_out, ffn_w_up, ffn_conv_w, ffn_conv_b, ffn_w_down):
    p = dict(rel_bias=rel_bias, mix_norm=mix_norm, ffn_norm=ffn_norm, ab_w_in=ab_w_in,
             q_norm=q_norm, k_norm=k_norm, attn_sink=attn_sink, sgu_v_norm=sgu_v_norm,
             sgu_w=sgu_w, sgu_b=sgu_b, ab_w_out=ab_w_out, hy_w_in=hy_w_in, hy_conv_w=hy_conv_w,
             hy_conv_b=hy_conv_b, hy_f_w1=hy_f_w1, hy_f_b1=hy_f_b1, hy_f_freq1=hy_f_freq1,
             hy_f_w2=hy_f_w2, hy_f_b2=hy_f_b2, hy_f_freq2=hy_f_freq2, hy_f_w3=hy_f_w3,
             hy_f_b3=hy_f_b3, hy_f_freq3=hy_f_freq3, hy_f_wout=hy_f_wout, hy_d=hy_d,
             hy_w_out=hy_w_out, ffn_w_up=ffn_w_up, ffn_conv_w=ffn_conv_w, ffn_conv_b=ffn_conv_b,
             ffn_w_down=ffn_w_down)
    q = _prepare(p)
    return (_trunk(x_prompt, q), _trunk(x_sample, q))
```

```python
import functools
import math

import numpy as np
import jax
import jax.numpy as jnp
from jax import lax
from jax.experimental import pallas as pl
from jax.experimental.pallas import tpu as pltpu

F32 = jnp.float32
BF16 = jnp.bfloat16

D_MODEL = 1024
HEAD_DIM = 64
N_HEADS_A = 8
N_KV_A = 2
GQA = 4
ATTN_W = 512
WINDOW = 128
BLOCK = 128
N_BUCKETS = 32
MAX_DISTANCE = 128
SGU_W = 512
N_GROUPS_B = 8
CH_B = 64
FILTER_EMB = 33
FILTER_BANDS = 16
FILTER_WIDTH = 64
DECAY_TARGET = 1e-2
MIN_DECAY = math.log(DECAY_TARGET) / 1.5
MAX_DECAY = math.log(DECAY_TARGET) / 0.3
FFN_HIDDEN = 2816
EPS = 1e-6
NEG = -1e30

LANES = 128
SUBLANES = 8
TOKENS_FFN = 512
TOKENS_MIXER = 1024
FFN_CHUNK = 256
HY_CHUNK = 256
HALO = 8
ROW_BLOCK = 64
FFN_SLAB_SETS = FFN_HIDDEN // FFN_CHUNK
FFT_N2 = LANES
FFT_ROWS = 8192
SPECTRUM_ROWS = 4096
VMEM_LIMIT = 56 * 1024 * 1024


def _gelu(x):
    t = jnp.tanh(x * (0.7978845608028654 + (0.7978845608028654 * 0.044715) * (x * x)))
    hx = 0.5 * x
    return hx + hx * t


def _rms(x, gain):
    return x * lax.rsqrt(jnp.mean(x * x, axis=-1, keepdims=True) + EPS) * gain


def _const_spec(shape):
    nd = len(shape)
    return pl.BlockSpec(shape, lambda *_: (0,) * nd, pipeline_mode=pl.Buffered(1))


def _params(n_axes):
    return pltpu.CompilerParams(dimension_semantics=("arbitrary",) * n_axes,
                                vmem_limit_bytes=VMEM_LIMIT)


def _park_slabs(s_ref, base, h):
    for s in range(h.shape[1] // LANES):
        s_ref[base + s] = h[:, s * LANES:(s + 1) * LANES]


def _conv3_slab(s_ref, slab, w, b, r0, rows):
    lo = HALO + r0
    return (s_ref[slab, lo - 1:lo - 1 + rows, :] * w[0:1]
            + s_ref[slab, lo:lo + rows, :] * w[1:2]
            + s_ref[slab, lo + 1:lo + 1 + rows, :] * w[2:3]
            + b)


def _fill_normed_ext(xe_ref, x, xp, xn, gain, rows, first, last):
    pm = jnp.where(first, 0.0, 1.0)
    nm = jnp.where(last, 0.0, 1.0)
    xe_ref[0:HALO, :] = (_rms(xp, gain) * pm).astype(BF16)
    xe_ref[HALO:HALO + rows, :] = _rms(x, gain).astype(BF16)
    xe_ref[HALO + rows:HALO + rows + HALO, :] = (_rms(xn, gain) * nm).astype(BF16)


def _halo_specs(T, L, D):
    hb = T // HALO
    last = L // HALO - 1
    return [
        pl.BlockSpec((None, T, D), lambda b, i: (b, i, 0)),
        pl.BlockSpec((None, HALO, D), lambda b, i: (b, jnp.maximum(i * hb - 1, 0), 0)),
        pl.BlockSpec((None, HALO, D), lambda b, i: (b, jnp.minimum((i + 1) * hb, last), 0)),
    ]


def _ffn_body(x, xp, xn, g_ref, wup_ref, cw_ref, cb_ref, wdn_ref, o_ref, xe_ref, a_ref, s_ref,
              T, nchunk):
    i = pl.program_id(1)
    _fill_normed_ext(xe_ref, x, xp, xn, g_ref[...], T, i == 0, i == pl.num_programs(1) - 1)
    xe = xe_ref[...]
    ns = FFN_CHUNK // LANES
    for j in range(nchunk):
        base = (j % FFN_SLAB_SETS) * 2 * ns
        for part in range(2):
            lo = part * FFN_HIDDEN + j * FFN_CHUNK
            _park_slabs(s_ref, base + part * ns,
                        jnp.dot(xe, wup_ref[:, lo:lo + FFN_CHUNK], preferred_element_type=F32))
        for s in range(ns):
            col = j * FFN_CHUNK + s * LANES
            wg, bg = cw_ref[:, col:col + LANES], cb_ref[:, col:col + LANES]
            wu = cw_ref[:, FFN_HIDDEN + col:FFN_HIDDEN + col + LANES]
            bu = cb_ref[:, FFN_HIDDEN + col:FFN_HIDDEN + col + LANES]
            for r0 in range(0, T, ROW_BLOCK):
                g = _conv3_slab(s_ref, base + s, wg, bg, r0, ROW_BLOCK)
                u = _conv3_slab(s_ref, base + ns + s, wu, bu, r0, ROW_BLOCK)
                a_ref[r0:r0 + ROW_BLOCK, col:col + LANES] = (_gelu(g) * u).astype(BF16)
    o_ref[...] = x + jnp.dot(a_ref[...], wdn_ref[...], preferred_element_type=F32)


def _ffn_kernel(x_ref, xp_ref, xn_ref, g_ref, wup_ref, cw_ref, cb_ref, wdn_ref, o_ref,
                xe_ref, a_ref, s_ref, *, T, nchunk):
    _ffn_body(x_ref[...], xp_ref[...], xn_ref[...], g_ref, wup_ref, cw_ref, cb_ref, wdn_ref,
              o_ref, xe_ref, a_ref, s_ref, T, nchunk)


def _ffn_scratch(T, D):
    return [pltpu.VMEM((T + 2 * HALO, D), BF16), pltpu.VMEM((T, FFN_HIDDEN), BF16),
            pltpu.VMEM((FFN_SLAB_SETS * 2 * FFN_CHUNK // LANES, T + 2 * HALO, LANES), F32)]


def _hy_ffn_kernel(x_ref, xp_ref, xn_ref, x0_ref, x0p_ref, x0n_ref, yt_ref, ytp_ref, ytn_ref,
                   wo_ref, g_ref, wup_ref, cw_ref, cb_ref, wdn_ref, o_ref,
                   xe_ref, a_ref, s_ref, m_ref, *, T, nchunk):
    tok = lambda blk: blk.astype(F32).T
    m_ref[0:HALO, :] = (x0p_ref[...] * tok(ytp_ref[...])[BLOCK - HALO:, :]).astype(BF16)
    for n in range(T // BLOCK):
        lo = HALO + n * BLOCK
        m_ref[lo:lo + BLOCK, :] = (x0_ref[n * BLOCK:(n + 1) * BLOCK, :] * tok(yt_ref[n])).astype(BF16)
    m_ref[HALO + T:, :] = (x0n_ref[...] * tok(ytn_ref[...])[:HALO, :]).astype(BF16)
    mix = jnp.dot(m_ref[...], wo_ref[...], preferred_element_type=F32)
    _ffn_body(x_ref[...] + mix[HALO:HALO + T], xp_ref[...] + mix[:HALO], xn_ref[...] + mix[HALO + T:],
              g_ref, wup_ref, cw_ref, cb_ref, wdn_ref, o_ref, xe_ref, a_ref, s_ref, T, nchunk)


def _hy_ffn(x, x0, yt, wo, gain, wup_r, cw_r, cb_r, wdn, T):
    B, L, D = x.shape
    nchunk = FFN_HIDDEN // FFN_CHUNK
    nb = T // BLOCK
    last = L // BLOCK - 1
    kern = functools.partial(_hy_ffn_kernel, T=T, nchunk=nchunk)
    return pl.pallas_call(
        kern,
        grid=(B, L // T),
        in_specs=_halo_specs(T, L, D) + _halo_specs(T, L, D) + [
            pl.BlockSpec((None, nb, D, LANES), lambda b, i: (b, i, 0, 0)),
            pl.BlockSpec((None, None, D, LANES), lambda b, i: (b, jnp.maximum(i * nb - 1, 0), 0, 0)),
            pl.BlockSpec((None, None, D, LANES), lambda b, i: (b, jnp.minimum((i + 1) * nb, last), 0, 0)),
            _const_spec(wo.shape), _const_spec((1, D)), _const_spec(wup_r.shape),
            _const_spec(cw_r.shape), _const_spec(cb_r.shape), _const_spec(wdn.shape)],
        out_specs=pl.BlockSpec((None, T, D), lambda b, i: (b, i, 0)),
        out_shape=jax.ShapeDtypeStruct(x.shape, F32),
        scratch_shapes=_ffn_scratch(T, D) + [pltpu.VMEM((T + 2 * HALO, D), BF16)],
        compiler_params=_params(2),
        name="hyena_out_conv_ffn",
    )(x, x, x, x0, x0, x0, yt, yt, yt, wo, gain, wup_r, cw_r, cb_r, wdn)


def _ffn(x, gain, wup_r, cw_r, cb_r, wdn, T):
    B, L, D = x.shape
    nchunk = FFN_HIDDEN // FFN_CHUNK
    kern = functools.partial(_ffn_kernel, T=T, nchunk=nchunk)
    return pl.pallas_call(
        kern,
        grid=(B, L // T),
        in_specs=_halo_specs(T, L, D) + [
            _const_spec((1, D)), _const_spec(wup_r.shape), _const_spec(cw_r.shape),
            _const_spec(cb_r.shape), _const_spec(wdn.shape)],
        out_specs=pl.BlockSpec((None, T, D), lambda b, i: (b, i, 0)),
        out_shape=jax.ShapeDtypeStruct(x.shape, F32),
        scratch_shapes=_ffn_scratch(T, D),
        compiler_params=_params(2),
        name="conv_ffn",
    )(x, x, x, gain, wup_r, cw_r, cb_r, wdn)


def _hy_in_kernel(x_ref, xp_ref, xn_ref, g_ref, win_ref, cw_ref, cb_ref, x0_ref, zt_ref,
                  xe_ref, s_ref, *, T, nchunk):
    i = pl.program_id(1)
    _fill_normed_ext(xe_ref, x_ref[...], xp_ref[...], xn_ref[...], g_ref[...], T,
                     i == 0, i == pl.num_programs(1) - 1)
    xe = xe_ref[...]
    ns = HY_CHUNK // LANES
    for j in range(nchunk):
        base = (j % 2) * 3 * ns
        for k in range(3):
            lo = k * D_MODEL + j * HY_CHUNK
            _park_slabs(s_ref, base + k * ns,
                        jnp.dot(xe, win_ref[:, lo:lo + HY_CHUNK], preferred_element_type=F32))
        for s in range(ns):
            col = j * HY_CHUNK + s * LANES
            w = [cw_ref[:, k * D_MODEL + col:k * D_MODEL + col + LANES] for k in range(3)]
            b = [cb_ref[:, k * D_MODEL + col:k * D_MODEL + col + LANES] for k in range(3)]
            for n in range(T // BLOCK):
                r0 = n * BLOCK
                x0_ref[r0:r0 + BLOCK, col:col + LANES] = _conv3_slab(s_ref, base + s, w[0], b[0], r0, BLOCK)
                z = (_conv3_slab(s_ref, base + ns + s, w[1], b[1], r0, BLOCK)
                     * _conv3_slab(s_ref, base + 2 * ns + s, w[2], b[2], r0, BLOCK))
                zt_ref[n, col:col + LANES, :] = z.T.astype(BF16)


def _hy_in(x, gain, win_r, cw_r, cb_r, T):
    B, L, D = x.shape
    nchunk = D // HY_CHUNK
    kern = functools.partial(_hy_in_kernel, T=T, nchunk=nchunk)
    return pl.pallas_call(
        kern,
        grid=(B, L // T),
        in_specs=_halo_specs(T, L, D) + [
            _const_spec((1, D)), _const_spec(win_r.shape), _const_spec(cw_r.shape),
            _const_spec(cb_r.shape)],
        out_specs=[pl.BlockSpec((None, T, D), lambda b, i: (b, i, 0)),
                   pl.BlockSpec((None, T // BLOCK, D, LANES), lambda b, i: (b, i, 0, 0))],
        out_shape=[jax.ShapeDtypeStruct(x.shape, F32),
                   jax.ShapeDtypeStruct((B, L // BLOCK, D, LANES), BF16)],
        scratch_shapes=[pltpu.VMEM((T + 2 * HALO, D), BF16),
                        pltpu.VMEM((2 * 3 * HY_CHUNK // LANES, T + 2 * HALO, LANES), F32)],
        compiler_params=_params(2),
        name="hyena_in",
    )(x, x, x, gain, win_r, cw_r, cb_r)


def _dot_f32(a, w_ref):
    a_hi = a.astype(BF16)
    a_lo = (a - a_hi.astype(F32)).astype(BF16)
    w_hi, w_lo = w_ref[0], w_ref[1]
    return (jnp.dot(a_hi, w_hi, preferred_element_type=F32)
            + jnp.dot(a_lo, w_hi, preferred_element_type=F32)
            + jnp.dot(a_hi, w_lo, preferred_element_type=F32))


def _filter_kernel(frl_ref, ph_ref, delta_ref, *refs, L, R, n_layers):
    hr = R // 2
    ja = pl.program_id(0) * R + lax.broadcasted_iota(jnp.int32, (hr, 1), 0)
    jb = ja + hr
    lane = lax.broadcasted_iota(jnp.int32, (hr, LANES), 1)
    j = jnp.where(lane < FILTER_WIDTH, ja, jb).astype(F32)
    t = j / float(L - 1)
    w = (2.0 * math.pi) * j / float(L)
    feats = jnp.where(lane % FILTER_WIDTH == 0, t, jnp.cos(w * frl_ref[...] + ph_ref[...]))
    decays = [jnp.exp(-(jidx.astype(F32) / float(L - 1)) * delta_ref[...]) for jidx in (ja, jb)]
    per_layer = 10
    outs = refs[n_layers * per_layer:]
    for layer in range(n_layers):
        (w1_ref, b1_ref, f1_ref, w2_ref, b2_ref, f2_ref, w3_ref, b3_ref, f3_ref,
         wout_ref) = refs[layer * per_layer:(layer + 1) * per_layer]
        hf_ref, hb_ref = outs[2 * layer], outs[2 * layer + 1]
        h = jnp.sin(f1_ref[...] * (_dot_f32(feats, w1_ref) + b1_ref[...]))
        h = jnp.sin(f2_ref[...] * (_dot_f32(h, w2_ref) + b2_ref[...]))
        h = jnp.sin(f3_ref[...] * (_dot_f32(h, w3_ref) + b3_ref[...]))
        h = _dot_f32(h, wout_ref)
        for half, jidx in enumerate((ja, jb)):
            base = half * 2 * D_MODEL
            hf = h[:, base:base + D_MODEL] * decays[half]
            hb = jnp.where(jidx == 0, 0.0, h[:, base + D_MODEL:base + 2 * D_MODEL] * decays[half])
            for n in range(hr // BLOCK):
                blk = half * (hr // BLOCK) + n
                hf_ref[blk] = hf[n * BLOCK:(n + 1) * BLOCK, :].T
                hb_ref[blk] = hb[n * BLOCK:(n + 1) * BLOCK, :].T


def _filter(L, frl, ph, delta, layers):
    R = 512
    kern = functools.partial(_filter_kernel, L=L, R=R, n_layers=len(layers))
    consts = [frl, ph, delta] + [a for layer in layers for a in layer]
    out_spec = pl.BlockSpec((R // BLOCK, D_MODEL, LANES), lambda i: (i, 0, 0))
    shape = jax.ShapeDtypeStruct((L // BLOCK, D_MODEL, LANES), F32)
    outs = pl.pallas_call(
        kern,
        grid=(L // R,),
        in_specs=[_const_spec(a.shape) for a in consts],
        out_specs=[out_spec] * (2 * len(layers)),
        out_shape=[shape] * (2 * len(layers)),
        compiler_params=_params(1),
        name="hyena_filter",
    )(*consts)
    return [(outs[2 * k], outs[2 * k + 1]) for k in range(len(layers))]


def _dft_consts(L):
    N = 2 * L
    N2 = FFT_N2
    N1 = N // N2
    h = N1 // 2
    a1 = -2.0 * np.pi * np.outer(np.arange(N1), np.arange(N1)) / N1
    f1r, f1i = np.cos(a1), np.sin(a1)
    a2 = -2.0 * np.pi * np.outer(np.arange(N2), np.arange(N2)) / N2
    f2r, f2i = np.cos(a2), np.sin(a2)
    at = -2.0 * np.pi * np.outer(np.arange(N1), np.arange(N2)) / N
    m1 = np.block([[f1r[:, :h], -f1i[:, :h]], [f1i[:, :h], f1r[:, :h]]])
    m1f = np.concatenate([f1r[:, :h], f1i[:, :h]], axis=0)
    m3 = np.block([[f1r[:h], f1i[:h]], [-f1i[:h], f1r[:h]]])
    c = lambda a: jnp.asarray(a, dtype=BF16)
    f = lambda a: jnp.asarray(a, dtype=F32)
    return dict(N1=N1, m1=c(m1), m1f=c(m1f), m3=c(m3),
                r2f=c(np.block([[f2r, f2i], [-f2i, f2r]])),
                r2i=c(np.block([[f2r, -f2i], [f2i, f2r]])),
                twr=f(np.cos(at)), twi=f(np.sin(at)))


def _cmul(ar, ai, br, bi):
    return ar * br - ai * bi, ar * bi + ai * br


def _slab8(ref, idx, c0, n):
    v = ref[idx + (slice(None), slice(c0, c0 + SUBLANES), slice(None))]
    return v.reshape(n * SUBLANES, LANES)


def _chan_rows(s_ref, base, c, n):
    return s_ref[pl.ds(base + c, n, stride=SUBLANES), :]


def _fft_kernel(x_ref, kf_ref, m1_ref, m3_ref, twr_ref, twi_ref, r2f_ref, r2i_ref, o_ref,
                *scratch, N1, nc):
    h = N1 // 2
    h8 = h * SUBLANES
    nsub = nc // SUBLANES
    twr, twi = twr_ref[...], twi_ref[...]
    cols = []
    for s16 in range(nsub // 2):
        lo16 = s16 * 2 * SUBLANES
        xr = x_ref[0, :, lo16:lo16 + 2 * SUBLANES, :].astype(F32)
        xi = x_ref[1, :, lo16:lo16 + 2 * SUBLANES, :].astype(F32)
        for half in range(2):
            sin_ref = scratch[2 * s16 + half]
            rows = slice(half * SUBLANES, (half + 1) * SUBLANES)
            sin_ref[0:h8] = xr[:, rows, :].reshape(h8, LANES)
            sin_ref[h8:2 * h8] = xi[:, rows, :].reshape(h8, LANES)
            cols += [jnp.concatenate([_chan_rows(sin_ref, 0, c, h), _chan_rows(sin_ref, h8, c, h)], axis=0)
                     for c in range(SUBLANES)]
    twr, twi = twr.astype(BF16), twi.astype(BF16)
    a = jnp.dot(m1_ref[...], jnp.concatenate(cols, axis=1).astype(BF16),
                preferred_element_type=F32).astype(BF16)
    lhs = []
    for c in range(nc):
        tr, ti = _cmul(a[:N1, c * LANES:(c + 1) * LANES], a[N1:, c * LANES:(c + 1) * LANES], twr, twi)
        lhs.append(jnp.concatenate([tr, ti], axis=1))
    lhs = jnp.concatenate(lhs, axis=0)
    X = jnp.dot(lhs, r2f_ref[...], preferred_element_type=F32)
    yr, yi = _cmul(X[:, :LANES], X[:, LANES:],
                   kf_ref[0].reshape(nc * N1, LANES), kf_ref[1].reshape(nc * N1, LANES))
    Bm = jnp.dot(jnp.concatenate([yr, yi], axis=1).astype(BF16), r2i_ref[...],
                 preferred_element_type=F32).astype(BF16)
    re_cols, im_cols = [], []
    for c in range(nc):
        br, bi = Bm[c * N1:(c + 1) * N1, :LANES], Bm[c * N1:(c + 1) * N1, LANES:]
        re_cols.append(br * twr + bi * twi)
        im_cols.append(bi * twr - br * twi)
    rhs = jnp.concatenate([jnp.concatenate(re_cols, axis=1),
                           jnp.concatenate(im_cols, axis=1)], axis=0)
    y = jnp.dot(m3_ref[...], rhs, preferred_element_type=F32)
    for s16 in range(nsub // 2):
        parts = ([], [])
        for half in range(2):
            c0 = (2 * s16 + half) * SUBLANES
            sout_ref = scratch[nsub + 2 * s16 + half]
            for c in range(SUBLANES):
                lo = (c0 + c) * LANES
                sout_ref[pl.ds(c, h, stride=SUBLANES), :] = y[:h, lo:lo + LANES]
                sout_ref[pl.ds(h8 + c, h, stride=SUBLANES), :] = y[h:, lo:lo + LANES]
            parts[0].append(sout_ref[0:h8].reshape(h, SUBLANES, LANES))
            parts[1].append(sout_ref[h8:2 * h8].reshape(h, SUBLANES, LANES))
        lo16 = s16 * 2 * SUBLANES
        for r in range(2):
            o_ref[r, :, lo16:lo16 + 2 * SUBLANES, :] = jnp.concatenate(parts[r], axis=1).astype(BF16)


def _fftconv(zt, kf, c):
    B, h, C, _ = zt.shape
    N1 = c["N1"]
    nc = min(FFT_ROWS // N1, C)
    kern = functools.partial(_fft_kernel, N1=N1, nc=nc)
    blk = pl.BlockSpec((2, h, nc, LANES), lambda cb, p: (p, 0, cb, 0))
    consts = [c["m1"], c["m3"], c["twr"], c["twi"], c["r2f"], c["r2i"]]
    return pl.pallas_call(
        kern,
        grid=(C // nc, B // 2),
        in_specs=[blk, pl.BlockSpec((2, nc, N1, LANES), lambda cb, p: (0, cb, 0, 0))]
        + [_const_spec(a.shape) for a in consts],
        out_specs=blk,
        out_shape=jax.ShapeDtypeStruct(zt.shape, BF16),
        scratch_shapes=[pltpu.VMEM((N1 * SUBLANES, LANES), F32)] * (2 * nc // SUBLANES),
        compiler_params=_params(2),
        name="fft_conv",
    )(zt, kf, *consts)


def _fspec_kernel(hf_ref, hb_ref, d_ref, m1f_ref, twr_ref, twi_ref, r2f_ref, o_ref,
                  *scratch, N1, nc, scale):
    h = N1 // 2
    h8 = h * SUBLANES
    twr, twi = twr_ref[...], twi_ref[...]
    cols = []
    for s8 in range(nc // SUBLANES):
        s_ref = scratch[s8]
        s_ref[0:h8] = _slab8(hf_ref, (), s8 * SUBLANES, h)
        s_ref[h8:2 * h8] = _slab8(hb_ref, (), s8 * SUBLANES, h)
        for c in range(SUBLANES):
            cols += [_chan_rows(s_ref, 0, c, h), _chan_rows(s_ref, h8, c, h)]
    twr, twi = twr.astype(BF16), twi.astype(BF16)
    a = jnp.dot(m1f_ref[...], jnp.concatenate(cols, axis=1).astype(BF16),
                preferred_element_type=F32).astype(BF16)
    lhs = []
    for k in range(2 * nc):
        tr, ti = _cmul(a[:N1, k * LANES:(k + 1) * LANES], a[N1:, k * LANES:(k + 1) * LANES], twr, twi)
        lhs.append(jnp.concatenate([tr, ti], axis=1))
    lhs = jnp.concatenate(lhs, axis=0)
    X = jnp.dot(lhs, r2f_ref[...], preferred_element_type=F32)
    for c in range(nc):
        xf = X[(2 * c) * N1:(2 * c + 1) * N1]
        xb = X[(2 * c + 1) * N1:(2 * c + 2) * N1]
        d = d_ref[c:c + 1, :]
        o_ref[0, c] = (xf[:, :LANES] + xb[:, :LANES] + d) * scale
        o_ref[1, c] = (xf[:, LANES:] - xb[:, LANES:]) * scale


def _filter_spectrum(hf, hb, d, c):
    h, C, _ = hf.shape
    N1 = c["N1"]
    nc = min(SPECTRUM_ROWS // N1, C)
    kern = functools.partial(_fspec_kernel, N1=N1, nc=nc, scale=1.0 / (N1 * FFT_N2))
    blk = pl.BlockSpec((h, nc, LANES), lambda cb: (0, cb, 0))
    consts = [c["m1f"], c["twr"], c["twi"], c["r2f"]]
    return pl.pallas_call(
        kern,
        grid=(C // nc,),
        in_specs=[blk, blk, pl.BlockSpec((nc, 1), lambda cb: (cb, 0))]
        + [_const_spec(a.shape) for a in consts],
        out_specs=pl.BlockSpec((2, nc, N1, LANES), lambda cb: (0, cb, 0, 0)),
        out_shape=jax.ShapeDtypeStruct((2, C, N1, LANES), F32),
        scratch_shapes=[pltpu.VMEM((N1 * SUBLANES, LANES), F32)] * (nc // SUBLANES),
        compiler_params=_params(1),
        name="filter_spectrum",
    )(hf, hb, d, *consts)


def _t5_bucket(rel):
    half = N_BUCKETS // 2
    max_exact = half // 2
    ret = jnp.where(rel > 0, half, 0)
    n = jnp.abs(rel)
    nf = jnp.maximum(n, 1).astype(jnp.float32)
    large = max_exact + (jnp.log(nf / max_exact) / math.log(MAX_DISTANCE / max_exact)
                         * (half - max_exact)).astype(jnp.int32)
    large = jnp.minimum(large, half - 1)
    return ret + jnp.where(n < max_exact, n, large)


LOG2E = 1.4426950408889634


def _bias_kernel(rb_ref, bucket_ref, rel_ref, o_ref):
    bucket = bucket_ref[...]
    rel = rel_ref[...]
    key = lax.broadcasted_iota(jnp.int32, bucket.shape, 0)
    for h in range(N_HEADS_A):
        acc = jnp.zeros(bucket.shape, F32)
        for b in range(N_BUCKETS):
            acc = jnp.where(bucket == b, rb_ref[b, h], acc)
        acc = acc * LOG2E
        for v in range(4):
            ok = jnp.abs(rel) <= WINDOW
            if v & 1:
                ok = ok & (key >= BLOCK)
            if v & 2:
                ok = ok & (key < 2 * BLOCK)
            o_ref[v, h] = jnp.where(ok, acc, NEG)


def _bias_table(rel_bias):
    rel = (jnp.arange(3 * BLOCK)[:, None] - BLOCK) - jnp.arange(BLOCK)[None, :]
    rel = rel.astype(jnp.int32)
    bucket = _t5_bucket(rel).astype(jnp.int32)
    return pl.pallas_call(
        _bias_kernel,
        in_specs=[pl.BlockSpec(memory_space=pltpu.SMEM),
                  pl.BlockSpec(memory_space=pltpu.VMEM), pl.BlockSpec(memory_space=pltpu.VMEM)],
        out_specs=pl.BlockSpec(memory_space=pltpu.VMEM),
        out_shape=jax.ShapeDtypeStruct((4, N_HEADS_A, 3 * BLOCK, BLOCK), F32),
        name="rel_bias_table",
    )(rel_bias.astype(F32), bucket, rel)


def _group_ms(v, gmat):
    return jnp.dot((v * v).astype(BF16), gmat, preferred_element_type=F32)


def _ab_kernel(sink_ref, x_ref, xp_ref, xn_ref, g_ref, win_ref, wout_ref, gm_ref,
               kg_ref, bias_ref, vg_ref, ws_ref, bs_ref, o_ref, *, T):
    i = pl.program_id(1)
    nb = T // BLOCK
    nblocks = pl.num_programs(1) * nb
    gain = g_ref[...]
    x = x_ref[...]
    proj = jnp.dot(_rms(x, gain).astype(BF16), win_ref[...], preferred_element_type=F32)
    xh = jnp.concatenate([_rms(xp_ref[...], gain), _rms(xn_ref[...], gain)], axis=0)
    kvh = jnp.dot(xh.astype(BF16), win_ref[:, ATTN_W:ATTN_W + 2 * BLOCK],
                  preferred_element_type=F32)
    gm = gm_ref[...]
    gm_k = gm[:BLOCK, :BLOCK]

    k_all = jnp.concatenate([kvh[:BLOCK, :BLOCK], proj[:, ATTN_W:ATTN_W + BLOCK],
                             kvh[BLOCK:, :BLOCK]], axis=0)
    v_all = jnp.concatenate([kvh[:BLOCK, BLOCK:], proj[:, ATTN_W + BLOCK:ATTN_W + 2 * BLOCK],
                             kvh[BLOCK:, BLOCK:]], axis=0)
    kn = k_all * lax.rsqrt(_group_ms(k_all, gm_k) + EPS) * kg_ref[...]

    su = _gelu(proj[:, ATTN_W + 2 * BLOCK:ATTN_W + 2 * BLOCK + SGU_W])
    sv = _gelu(proj[:, ATTN_W + 2 * BLOCK + SGU_W:])
    svn = (sv * lax.rsqrt(_group_ms(sv, gm) + EPS) * vg_ref[...]).astype(BF16)
    low = lax.broadcasted_iota(jnp.int32, (1, BLOCK * nb), 1) % BLOCK < CH_B
    slabs = []
    for j in range(SGU_W // BLOCK):
        rhs = jnp.concatenate([svn[n * BLOCK:(n + 1) * BLOCK, j * BLOCK:(j + 1) * BLOCK]
                               for n in range(nb)], axis=1)
        a = jnp.dot(ws_ref[2 * j], rhs, preferred_element_type=F32)
        b = jnp.dot(ws_ref[2 * j + 1], rhs, preferred_element_type=F32)
        slabs.append(jnp.where(low, a, b))
    mixed = jnp.concatenate(
        [jnp.concatenate([slabs[j][:, n * BLOCK:(n + 1) * BLOCK] for j in range(SGU_W // BLOCK)],
                         axis=1) + bs_ref[...] for n in range(nb)], axis=0)
    sgu_out = jnp.dot((su * mixed).astype(BF16), wout_ref[ATTN_W:, :], preferred_element_type=F32)

    q_t = proj[:, :ATTN_W].T
    qt = []
    for h in range(N_HEADS_A):
        qh = q_t[h * HEAD_DIM:(h + 1) * HEAD_DIM]
        qt.append((qh * lax.rsqrt(jnp.mean(qh * qh, axis=0, keepdims=True) + EPS)).astype(BF16))
    qt = jnp.concatenate(qt, axis=0)
    vt = v_all.T.astype(BF16)
    knb = kn.astype(BF16)
    zeros_q = jnp.zeros((HEAD_DIM, GQA * BLOCK), BF16)
    ones_rows = jnp.ones((2 * SUBLANES, 3 * BLOCK), BF16)
    units = [(n, hk) for n in range(nb) for hk in range(N_KV_A)]
    sks = [jnp.concatenate([jnp.full((1, BLOCK), sink_ref[hk * GQA + g] * LOG2E, F32)
                            for g in range(GQA)], axis=1) for hk in range(N_KV_A)]
    scores = []
    for n, hk in units:
        blk = i * nb + n
        variant = jnp.where(blk == 0, 1, 0) + jnp.where(blk == nblocks - 1, 2, 0)
        qh = jnp.concatenate(
            [qt[(hk * GQA + g) * HEAD_DIM:(hk * GQA + g + 1) * HEAD_DIM, n * BLOCK:(n + 1) * BLOCK]
             for g in range(GQA)], axis=1)
        qz = jnp.concatenate([qh, zeros_q] if hk == 0 else [zeros_q, qh], axis=0)
        s = jnp.dot(knb[n * BLOCK:n * BLOCK + 3 * BLOCK, :], qz,
                    preferred_element_type=F32)
        scores.append(s + jnp.concatenate([bias_ref[variant, hk * GQA + g] for g in range(GQA)], axis=1))
    probs = []
    for (n, hk), s in zip(units, scores):
        m = jnp.maximum(jnp.max(s, axis=0, keepdims=True), sks[hk])
        probs.append((jnp.exp2(s - m).astype(BF16), jnp.exp2(sks[hk] - m)))
    outs = []
    for (n, hk), (p, psink) in zip(units, probs):
        lhs = jnp.concatenate([vt[hk * HEAD_DIM:(hk + 1) * HEAD_DIM, n * BLOCK:n * BLOCK + 3 * BLOCK],
                               ones_rows], axis=0)
        pv = jnp.dot(lhs, p, preferred_element_type=F32)
        outs.append(pv[:HEAD_DIM] / (pv[HEAD_DIM:HEAD_DIM + 1] + psink))
    out_cols = []
    for n in range(nb):
        out_cols.append(jnp.concatenate(
            [outs[n * N_KV_A + hk][:, g * BLOCK:(g + 1) * BLOCK]
             for hk in range(N_KV_A) for g in range(GQA)], axis=0))
    attn = jnp.concatenate(out_cols, axis=1).T.astype(BF16)
    o_ref[...] = x + sgu_out + jnp.dot(attn, wout_ref[:ATTN_W, :], preferred_element_type=F32)


def _ab_layer(x, sink, gain, win, wout, gm, kg, bias_tab, vg, ws, bs, T):
    B, L, D = x.shape
    hb = T // BLOCK
    last = L // BLOCK - 1
    kern = functools.partial(_ab_kernel, T=T)
    consts = [gain, win, wout, gm, kg, bias_tab, vg, ws, bs]
    return pl.pallas_call(
        kern,
        grid=(B, L // T),
        in_specs=[pl.BlockSpec(memory_space=pltpu.SMEM),
                  pl.BlockSpec((None, T, D), lambda b, i: (b, i, 0)),
                  pl.BlockSpec((None, BLOCK, D), lambda b, i: (b, jnp.maximum(i * hb - 1, 0), 0)),
                  pl.BlockSpec((None, BLOCK, D), lambda b, i: (b, jnp.minimum((i + 1) * hb, last), 0)),
                  ] + [_const_spec(a.shape) for a in consts],
        out_specs=pl.BlockSpec((None, T, D), lambda b, i: (b, i, 0)),
        out_shape=jax.ShapeDtypeStruct(x.shape, F32),
        compiler_params=_params(2),
        name="attn_sgu_mixer",
    )(sink, x, x, x, *consts)


def _prepare(p):
    depth = p["mix_norm"].shape[0]
    q = dict(depth=depth)
    q["mix_norm"] = [p["mix_norm"][l][None, :] for l in range(depth)]
    q["ffn_norm"] = [p["ffn_norm"][l][None, :] for l in range(depth)]
    q["ffn_wup"] = [p["ffn_w_up"][l].astype(BF16) for l in range(depth)]
    q["ffn_cw"] = [p["ffn_conv_w"][l] for l in range(depth)]
    q["ffn_cb"] = [p["ffn_conv_b"][l][None, :] for l in range(depth)]
    q["ffn_wdn"] = [p["ffn_w_down"][l].astype(BF16) for l in range(depth)]

    n_even = p["ab_w_in"].shape[0]
    q["ab_win"] = [p["ab_w_in"][i].astype(BF16) for i in range(n_even)]
    q["ab_wout"] = [p["ab_w_out"][i].astype(BF16) for i in range(n_even)]
    q["kg"] = [jnp.tile(p["k_norm"][i] * p["q_norm"][i] * (HEAD_DIM ** -0.5 * LOG2E), N_KV_A)[None, :]
               for i in range(n_even)]
    q["sink"] = [p["attn_sink"][i].astype(F32) for i in range(n_even)]
    q["vg"] = [p["sgu_v_norm"][i].reshape(1, SGU_W) for i in range(n_even)]
    q["ws"] = [p["sgu_w"][i].astype(BF16) for i in range(n_even)]
    q["bs"] = [jnp.repeat(p["sgu_b"][i].T, CH_B, axis=1) for i in range(n_even)]
    grp = np.arange(ATTN_W) // HEAD_DIM
    q["gm"] = jnp.asarray((grp[:, None] == grp[None, :]) / float(HEAD_DIM), dtype=BF16)

    n_odd = p["hy_w_in"].shape[0]
    q["hy_win"] = [p["hy_w_in"][i].astype(BF16) for i in range(n_odd)]
    q["hy_cw"] = [p["hy_conv_w"][i] for i in range(n_odd)]
    q["hy_cb"] = [p["hy_conv_b"][i][None, :] for i in range(n_odd)]
    q["hy_d"] = [p["hy_d"][i][:, None].astype(F32) for i in range(n_odd)]
    q["hy_wout"] = [p["hy_w_out"][i].astype(BF16) for i in range(n_odd)]
    slot = np.arange(LANES) % FILTER_WIDTH
    fr = np.linspace(1e-4, FILTER_BANDS - 1, FILTER_BANDS).astype(np.float32)
    frl = np.where((slot >= 1) & (slot <= 2 * FILTER_BANDS), fr[(slot - 1) % FILTER_BANDS], 0.0)
    ph = np.where((slot > FILTER_BANDS) & (slot <= 2 * FILTER_BANDS), 0.5 * np.pi, 0.0)
    q["frl"] = jnp.asarray(frl[None, :], dtype=F32)
    q["ph"] = jnp.asarray(ph[None, :], dtype=F32)
    q["delta"] = jnp.abs(jnp.linspace(MIN_DECAY, MAX_DECAY, D_MODEL, dtype=F32))[None, :]

    def twice(w):
        z = jnp.zeros_like(w)
        w = jnp.concatenate([jnp.concatenate([w, z], axis=1), jnp.concatenate([z, w], axis=1)], axis=0)
        hi = w.astype(BF16)
        return jnp.stack([hi, (w - hi.astype(F32)).astype(BF16)])

    filt = []
    for i in range(n_odd):
        w1p = jnp.zeros((FILTER_WIDTH, FILTER_WIDTH), F32).at[:FILTER_EMB].set(p["hy_f_w1"][i])
        row = lambda a: jnp.tile(a[None, :].astype(F32), (1, 2))
        filt.append((twice(w1p), row(p["hy_f_b1"][i]), row(p["hy_f_freq1"][i]),
                     twice(p["hy_f_w2"][i]), row(p["hy_f_b2"][i]), row(p["hy_f_freq2"][i]),
                     twice(p["hy_f_w3"][i]), row(p["hy_f_b3"][i]), row(p["hy_f_freq3"][i]),
                     twice(p["hy_f_wout"][i])))
    q["filt"] = filt
    q["bias_tab"] = _bias_table(p["rel_bias"])
    return q


def _hyena_conv(x, q, l, consts, T, taps):
    i = l // 2
    hf, hb = taps
    kf = _filter_spectrum(hf, hb, q["hy_d"][i], consts)
    x0, zt = _hy_in(x, q["mix_norm"][l], q["hy_win"][i], q["hy_cw"][i], q["hy_cb"][i], T)
    return x0, _fftconv(zt, kf, consts)


def _trunk(x, q):
    B, L, D = x.shape
    consts = _dft_consts(L)
    t_mix, t_ffn = min(TOKENS_MIXER, L), min(TOKENS_FFN, L)
    taps = _filter(L, q["frl"], q["ph"], q["delta"], q["filt"])
    for l in range(q["depth"]):
        i = l // 2
        ffn = (q["ffn_norm"][l], q["ffn_wup"][l], q["ffn_cw"][l], q["ffn_cb"][l], q["ffn_wdn"][l])
        if l % 2 == 0:
            x = _ab_layer(x, q["sink"][i], q["mix_norm"][l], q["ab_win"][i], q["ab_wout"][i],
                          q["gm"], q["kg"][i], q["bias_tab"], q["vg"][i], q["ws"][i],
                          q["bs"][i], t_mix)
            x = _ffn(x, *ffn, t_ffn)
        else:
            x0, yt = _hyena_conv(x, q, l, consts, t_mix, taps[i])
            x = _hy_ffn(x, x0, yt, q["hy_wout"][i], *ffn, t_ffn)
    return x


def kernel(x_prompt, x_sample, rel_bias, mix_norm, ffn_norm, ab_w_in, q_norm, k_norm, attn_sink, sgu_v_norm, sgu_w, sgu_b, ab_w_out, hy_w_in, hy_conv_w, hy_conv_b, hy_f_w1, hy_f_b1, hy_f_freq1, hy_f_w2, hy_f_b2, hy_f_freq2, hy_f_w3, hy_f_b3, hy_f_freq3, hy_f_wout, hy_d, hy_w_out, ffn_w_up, ffn_conv_w, ffn_conv_b, ffn_w_down):
    p = dict(rel_bias=rel_bias, mix_norm=mix_norm, ffn_norm=ffn_norm, ab_w_in=ab_w_in,
             q_norm=q_norm, k_norm=k_norm, attn_sink=attn_sink, sgu_v_norm=sgu_v_norm,
             sgu_w=sgu_w, sgu_b=sgu_b, ab_w_out=ab_w_out, hy_w_in=hy_w_in, hy_conv_w=hy_conv_w,
             hy_conv_b=hy_conv_b, hy_f_w1=hy_f_w1, hy_f_b1=hy_f_b1, hy_f_freq1=hy_f_freq1,
             hy_f_w2=hy_f_w2, hy_f_b2=hy_f_b2, hy_f_freq2=hy_f_freq2, hy_f_w3=hy_f_w3,
             hy_f_b3=hy_f_b3, hy_f_freq3=hy_f_freq3, hy_f_wout=hy_f_wout, hy_d=hy_d,
             hy_w_out=hy_w_out, ffn_w_up=ffn_w_up, ffn_conv_w=ffn_conv_w, ffn_conv_b=ffn_conv_b,
             ffn_w_down=ffn_w_down)
    q = _prepare(p)
    return (_trunk(x_prompt, q), _trunk(x_sample, q))
```
